```python
import jax, jax.numpy as jnp
from jax import lax
import numpy as np

D_MODEL = 2048
BATCH = 2
SEQ = 8192
DEPTH = 1

GRID_W = 64
CTX_LEN = 256
MIX_WIDTH = D_MODEL
RET_WIDTH = MIX_WIDTH // 2
RET_HEADS = 8
RET_HEAD_DIM = RET_WIDTH // RET_HEADS
CONV_CH = MIX_WIDTH - RET_WIDTH
CONV_TAPS = 3
IN_WIDTH = 4 * RET_WIDTH + 3 * CONV_CH
D_FF = ((8 * D_MODEL // 3 + 255) // 256) * 256
CHUNK = 128
ROPE_THETA = 10000.0
EPS = 1e-6
N_MOD = 6

kernel_name = 'hybrid_retention_shortconv_flow_block'


def rms_norm(x, gain):
    xf = x.astype(jnp.float32)
    y = xf * lax.rsqrt(jnp.mean(xf * xf, axis=-1, keepdims=True) + EPS)
    return (y * gain.astype(jnp.float32)).astype(x.dtype)


def modulate(h, shift, scale):
    return h * (1 + scale) + shift


def split_heads(t):
    b, n, _ = t.shape
    return t.reshape(b, n, RET_HEADS, RET_HEAD_DIM).transpose(0, 2, 1, 3)


def merge_heads(t):
    b, h, n, d = t.shape
    return t.transpose(0, 2, 1, 3).reshape(b, n, h * d)


def head_rms_norm(o):
    of = o.astype(jnp.float32)
    return (of * lax.rsqrt(jnp.mean(of * of, axis=-1, keepdims=True) + EPS)).astype(o.dtype)


def rope1d(x, pos):
    d = x.shape[-1]
    half = d // 2
    freqs = ROPE_THETA ** (-jnp.arange(0, d, 2, dtype=jnp.float32) / d)
    ang = pos[:, None] * freqs[None, :]
    cos = jnp.cos(ang).astype(x.dtype)
    sin = jnp.sin(ang).astype(x.dtype)
    x1, x2 = x[..., :half], x[..., half:]
    return jnp.concatenate([x1 * cos - x2 * sin, x1 * sin + x2 * cos], axis=-1)


def rope2d(x, row, col):
    half = x.shape[-1] // 2
    return jnp.concatenate([rope1d(x[..., :half], row), rope1d(x[..., half:], col)], axis=-1)


def conv3_along(x, w, axis):
    n = x.shape[axis]
    pad = [(0, 0)] * x.ndim
    pad[axis] = (1, 1)
    xp = jnp.pad(x, pad)
    taps = [lax.slice_in_dim(xp, i, i + n, axis=axis) for i in range(CONV_TAPS)]
    return taps[0] * w[0] + taps[1] * w[1] + taps[2] * w[2]


def dwconv3(x, w, rows, vertical):
    if rows is None:
        return conv3_along(x, w, 1)
    b, n, ch = x.shape
    xg = x.reshape(b, rows, GRID_W, ch)
    return conv3_along(xg, w, 1 if vertical else 2).reshape(b, n, ch)


def retention_dir(q, k, v, log_gamma, s0, strict):
    b, h, n, dk = q.shape
    dv = v.shape[-1]
    nc = n // CHUNK
    dt = q.dtype
    qc = q.reshape(b, h, nc, CHUNK, dk)
    kc = k.reshape(b, h, nc, CHUNK, dk)
    vc = v.reshape(b, h, nc, CHUNK, dv)
    idx = jnp.arange(CHUNK, dtype=jnp.float32)
    rel = idx[:, None] - idx[None, :]
    mask = (rel > 0) if strict else (rel >= 0)
    lg = log_gamma[:, None, None]
    dmat = jnp.where(mask[None], jnp.exp(lg * jnp.maximum(rel, 0.0)[None]), 0.0).astype(dt)
    scores = jnp.einsum('bhnid,bhnjd->bhnij', qc, kc) * dmat[None, :, None]
    inner = jnp.einsum('bhnij,bhnje->bhnie', scores, vc)
    k_decay = jnp.exp(log_gamma[:, None] * (CHUNK - 1.0 - idx)[None]).astype(dt)
    kv = jnp.einsum('bhnjd,hj,bhnje->nbhde', kc, k_decay, vc)
    chunk_decay = jnp.exp(log_gamma * CHUNK).astype(dt)[None, :, None, None]

    def step(s, kv_n):
        return chunk_decay * s + kv_n, s

    s_final, s_prev = lax.scan(step, s0, kv)
    q_decay = jnp.exp(log_gamma[:, None] * (idx + 1.0)[None]).astype(dt)
    cross = jnp.einsum('bhnid,nbhde->bhnie', qc, s_prev) * q_decay[None, :, None, :, None]
    return (inner + cross).reshape(b, h, n, dv), s_final


def context_final_state(k, v, log_gamma, reverse):
    l = k.shape[2]
    t = jnp.arange(l, dtype=jnp.float32)
    expo = t if reverse else (l - 1.0 - t)
    w = jnp.exp(log_gamma[:, None] * expo[None]).astype(k.dtype)
    return jnp.einsum('bhtd,ht,bhte->bhde', k, w, v)


def token_mixers(p, s0_f, s0_b, log_gf, log_gb, conv_w, w_out, rows, grid_pos):
    r, cc = RET_WIDTH, CONV_CH
    q = split_heads(p[..., 0:r])
    k = split_heads(p[..., r:2 * r])
    v = split_heads(p[..., 2 * r:3 * r])
    g = p[..., 3 * r:4 * r]
    bg = p[..., 4 * r:4 * r + cc]
    cg = p[..., 4 * r + cc:4 * r + 2 * cc]
    hv = p[..., 4 * r + 2 * cc:4 * r + 3 * cc]
    if grid_pos is not None:
        q = rope2d(q, grid_pos[0], grid_pos[1])
        k = rope2d(k, grid_pos[0], grid_pos[1])
    k = k * (RET_HEAD_DIM ** -0.5)
    o_f, s_f = retention_dir(q, k, v, log_gf, s0_f, False)
    o_b_rev, s_b = retention_dir(jnp.flip(q, 2), jnp.flip(k, 2), jnp.flip(v, 2), log_gb, s0_b, True)
    o = head_rms_norm(o_f + jnp.flip(o_b_rev, 2))
    ret = merge_heads(o) * jax.nn.silu(g)
    conv = bg * dwconv3(cg * hv, conv_w, rows, vertical=False)
    mix = jnp.concatenate([ret, conv], axis=-1) @ w_out
    return mix, s_f, s_b


def conv_ffn(h, w_up, conv_w, conv_b, w_down, rows):
    u = h @ w_up
    a, b = u[..., :D_FF], u[..., D_FF:]
    a = dwconv3(a, conv_w, rows, vertical=True) + conv_b
    return (jax.nn.silu(a) * b) @ w_down


def setup_inputs(seed: int = 0) -> dict:
    key = jax.random.key(seed)
    ks = jax.random.split(key, 20)
    f32 = jnp.float32

    def nrm(k, shape, scale):
        return jax.random.normal(k, shape, f32) * scale

    decay_logit = jnp.asarray(np.log(2.0 ** (5 + np.arange(RET_HEADS)) - 1.0).astype(np.float32))
    return {
        'x': nrm(ks[0], (BATCH, SEQ, D_MODEL), 1.0),
        'c': nrm(ks[1], (BATCH, D_MODEL), 1.0),
        'ctx': nrm(ks[2], (BATCH, CTX_LEN, D_MODEL), 1.0),
        'c_ctx': nrm(ks[3], (D_MODEL,), 1.0),
        'w_mod': nrm(ks[4], (DEPTH, D_MODEL, N_MOD * D_MODEL), D_MODEL ** -0.5),
        'b_mod': nrm(ks[5], (DEPTH, N_MOD * D_MODEL), 0.01),
        'norm1_g': 1.0 + nrm(ks[6], (DEPTH, D_MODEL), 0.02),
        'w_in': nrm(ks[7], (DEPTH, D_MODEL, IN_WIDTH), D_MODEL ** -0.5),
        'ret_decay_fwd': decay_logit[None] + nrm(ks[8], (DEPTH, RET_HEADS), 0.1),
        'ret_decay_bwd': decay_logit[None] + nrm(ks[9], (DEPTH, RET_HEADS), 0.1),
        'conv_w': nrm(ks[10], (DEPTH, CONV_TAPS, CONV_CH), CONV_TAPS ** -0.5),
        'w_out': nrm(ks[11], (DEPTH, MIX_WIDTH, D_MODEL), MIX_WIDTH ** -0.5),
        'norm2_g': 1.0 + nrm(ks[12], (DEPTH, D_MODEL), 0.02),
        'w_up': nrm(ks[13], (DEPTH, D_MODEL, 2 * D_FF), D_MODEL ** -0.5),
        'ffn_conv_w': nrm(ks[14], (DEPTH, CONV_TAPS, D_FF), CONV_TAPS ** -0.5),
        'ffn_conv_b': nrm(ks[15], (DEPTH, D_FF), 0.01),
        'w_down': nrm(ks[16], (DEPTH, D_FF, D_MODEL), D_FF ** -0.5),
        'final_g': 1.0 + nrm(ks[17], (D_MODEL,), 0.02),
    }


def reference(x, c, ctx, c_ctx, w_mod, b_mod, norm1_g, w_in, ret_decay_fwd, ret_decay_bwd,
              conv_w, w_out, norm2_g, w_up, ffn_conv_w, ffn_conv_b, w_down, final_g):
    b, n, _ = x.shape
    rows = n // GRID_W
    pos = jnp.arange(n, dtype=jnp.int32)
    grid_pos = ((pos // GRID_W).astype(jnp.float32), (pos % GRID_W).astype(jnp.float32))
    r = RET_WIDTH

    for layer in range(DEPTH):
        last = layer == DEPTH - 1
        mod = (jax.nn.silu(c) @ w_mod[layer] + b_mod[layer])[:, None, :]
        mod_c = jax.nn.silu(c_ctx) @ w_mod[layer] + b_mod[layer]
        sh1, sc1, g1, sh2, sc2, g2 = jnp.split(mod, N_MOD, axis=-1)
        csh1, csc1, cg1, csh2, csc2, cg2 = jnp.split(mod_c, N_MOD, axis=-1)
        log_gf = jax.nn.log_sigmoid(ret_decay_fwd[layer].astype(jnp.float32))
        log_gb = jax.nn.log_sigmoid(ret_decay_bwd[layer].astype(jnp.float32))

        hc = modulate(rms_norm(ctx, norm1_g[layer]), csh1, csc1)
        if last:
            pc = hc @ w_in[layer][:, r:3 * r]
            kc = split_heads(pc[..., :r]) * (RET_HEAD_DIM ** -0.5)
            vc = split_heads(pc[..., r:])
            s_f = context_final_state(kc, vc, log_gf, reverse=False)
            s_b = context_final_state(kc, vc, log_gb, reverse=True)
        else:
            zeros = jnp.zeros((b, RET_HEADS, RET_HEAD_DIM, RET_HEAD_DIM), hc.dtype)
            mix_c, s_f, s_b = token_mixers(hc @ w_in[layer], zeros, zeros, log_gf, log_gb,
                                           conv_w[layer], w_out[layer], None, None)
            ctx_mid = ctx + cg1 * mix_c
            hc2 = modulate(rms_norm(ctx_mid, norm2_g[layer]), csh2, csc2)
            ctx_next = ctx_mid + cg2 * conv_ffn(hc2, w_up[layer], ffn_conv_w[layer],
                                                ffn_conv_b[layer], w_down[layer], None)

        h = modulate(rms_norm(x, norm1_g[layer]), sh1, sc1)
        mix, _, _ = token_mixers(h @ w_in[layer], s_f, s_b, log_gf, log_gb,
                                 conv_w[layer], w_out[layer], rows, grid_pos)
        x = x + g1 * mix
        h2 = modulate(rms_norm(x, norm2_g[layer]), sh2, sc2)
        x = x + g2 * conv_ffn(h2, w_up[layer], ffn_conv_w[layer], ffn_conv_b[layer],
                              w_down[layer], rows)
        if not last:
            ctx = ctx_next

    return rms_norm(x, final_g)
```

```python
import functools

import jax
import jax.numpy as jnp
from jax import lax
from jax.experimental import pallas as pl
from jax.experimental.pallas import tpu as pltpu

F32 = jnp.float32
BF16 = jnp.bfloat16

GRID_W = 64
RET_HEADS = 8
HEAD_DIM = 128
RET_WIDTH = RET_HEADS * HEAD_DIM
CONV_CH = 1024
N_MOD = 6
EPS = 1e-6
ROPE_THETA = 10000.0

MOD_ROWS = 8
TOKEN_TILE = 512
RET_CHUNK = 256
FF_TILE = 512
MOD_COL_TILE = 1024
VMEM_LIMIT = 56 * 1024 * 1024


def _sigmoid(x):
    return 1.0 / (1.0 + jnp.exp(-x))


def _log_sigmoid(x):
    return jnp.minimum(x, 0.0) - jnp.log(1.0 + jnp.exp(-jnp.abs(x)))


def _norm_mod(x, gain, shift, scale):
    ms = jnp.mean(x * x, axis=-1, keepdims=True)
    y = x * lax.rsqrt(ms + EPS) * gain
    return y * (1.0 + scale) + shift


def _dot(a, b):
    return jnp.dot(a, b, preferred_element_type=F32)


def _dot_t0(a, b):
    return lax.dot_general(a, b, (((0,), (0,)), ((), ())), preferred_element_type=F32)


def _dot_nt(a, b):
    return lax.dot_general(a, b, (((1,), (1,)), ((), ())), preferred_element_type=F32)


def _mod_kernel(c_ref, w_ref, b_ref, o_ref):
    a = c_ref[...]
    a = (a * _sigmoid(a)).astype(BF16)
    o_ref[...] = _dot(a, w_ref[...].astype(BF16)) + b_ref[...]


def _mod_call(craw, w_mod, b_mod):
    d, n = w_mod.shape
    tn = MOD_COL_TILE
    return pl.pallas_call(
        _mod_kernel,
        grid=(n // tn,),
        in_specs=[
            pl.BlockSpec((MOD_ROWS, d), lambda j: (0, 0)),
            pl.BlockSpec((d, tn), lambda j: (0, j)),
            pl.BlockSpec((1, tn), lambda j: (0, j)),
        ],
        out_specs=pl.BlockSpec((MOD_ROWS, tn), lambda j: (0, j)),
        out_shape=jax.ShapeDtypeStruct((MOD_ROWS, n), F32),
        compiler_params=pltpu.CompilerParams(
            dimension_semantics=("arbitrary",), vmem_limit_bytes=VMEM_LIMIT),
        name="mod",
    )(craw, w_mod, b_mod)


def _ctx_kernel(ctx_row, x_ref, sh_ref, sc_ref, g_ref, wk_ref, wv_ref, decf_ref, decb_ref,
                sf_ref, sb_ref):
    x = x_ref[0]
    length = x.shape[0]
    h = _norm_mod(x, g_ref[...], sh_ref[ctx_row:ctx_row + 1, :],
                  sc_ref[ctx_row:ctx_row + 1, :]).astype(BF16)
    k = _dot(h, wk_ref[...]) * (HEAD_DIM ** -0.5)
    v = _dot(h, wv_ref[...]).astype(BF16)
    t = lax.broadcasted_iota(jnp.int32, k.shape, 0).astype(F32)
    wf = jnp.exp(_log_sigmoid(decf_ref[...]) * (length - 1.0 - t))
    wb = jnp.exp(_log_sigmoid(decb_ref[...]) * t)
    kf = (k * wf).astype(BF16)
    kb = (k * wb).astype(BF16)
    for hd in range(RET_HEADS):
        sl = slice(hd * HEAD_DIM, (hd + 1) * HEAD_DIM)
        sf_ref[0, hd] = _dot_t0(kf[:, sl], v[:, sl])
        sb_ref[0, hd] = _dot_t0(kb[:, sl], v[:, sl])


def _ctx_call(ctx, mod, norm_g, w_in, dec_f, dec_b):
    b, length, d = ctx.shape
    r = RET_WIDTH
    state = jax.ShapeDtypeStruct((b, RET_HEADS, HEAD_DIM, HEAD_DIM), F32)
    state_spec = pl.BlockSpec((1, RET_HEADS, HEAD_DIM, HEAD_DIM), lambda i: (i, 0, 0, 0))
    return pl.pallas_call(
        functools.partial(_ctx_kernel, b),
        grid=(b,),
        in_specs=[
            pl.BlockSpec((1, length, d), lambda i: (i, 0, 0)),
            pl.BlockSpec((MOD_ROWS, d), lambda i: (0, 0)),
            pl.BlockSpec((MOD_ROWS, d), lambda i: (0, 1)),
            pl.BlockSpec((1, d), lambda i: (0, 0)),
            pl.BlockSpec((d, r), lambda i: (0, 1)),
            pl.BlockSpec((d, r), lambda i: (0, 2)),
            pl.BlockSpec((1, r), lambda i: (0, 0)),
            pl.BlockSpec((1, r), lambda i: (0, 0)),
        ],
        out_specs=[state_spec, state_spec],
        out_shape=[state, state],
        compiler_params=pltpu.CompilerParams(
            dimension_semantics=("arbitrary",), vmem_limit_bytes=VMEM_LIMIT),
        name="ctx",
    )(ctx, mod, mod, norm_g, w_in, w_in, dec_f, dec_b)


def _rope_store(acc, cos, sin, out_ref, scale):
    lane = lax.broadcasted_iota(jnp.int32, (1, HEAD_DIM), 1)
    lo = (lane % 64) < 32
    outs = []
    for hd in range(RET_HEADS):
        sl = slice(hd * HEAD_DIM, (hd + 1) * HEAD_DIM)
        xh = acc[:, sl]
        partner = jnp.where(lo, pltpu.roll(xh, 96, 1), pltpu.roll(xh, 32, 1))
        y = xh * cos + partner * sin
        if scale != 1.0:
            y = y * scale
        out_ref[0, :, sl] = y.astype(out_ref.dtype)
        outs.append(y)
    return outs


def _inproj_kernel(x_ref, sh_ref, sc_ref, g_ref, w_ref, cos_ref, sin_ref, cw_ref, decb_ref,
                   sb0_ref,
                   q_ref, k_ref, v_ref, gate_ref, conv_ref, sbound_ref,
                   h_scr, kd_scr, b_scr, c_scr, s_scr, dec_scr):
    b = pl.program_id(0)
    i = pl.program_id(1)
    j = pl.program_id(2)
    tm = h_scr.shape[0]

    @pl.when(j == 0)
    def _():
        h = _norm_mod(x_ref[0], g_ref[...], sh_ref[pl.ds(b, 1), :], sc_ref[pl.ds(b, 1), :])
        h_scr[...] = h.astype(BF16)

        @pl.when(i == 0)
        def _():
            s_scr[...] = sb0_ref[0]
            t = lax.broadcasted_iota(jnp.int32, dec_scr.shape, 0).astype(F32)
            dec_scr[...] = jnp.exp(_log_sigmoid(decb_ref[...]) * t)

    acc = _dot(h_scr[...], w_ref[...])

    @pl.when(j == 0)
    def _():
        _rope_store(acc, cos_ref[...], sin_ref[...], q_ref, 1.0)

    @pl.when(j == 1)
    def _():
        ks = _rope_store(acc, cos_ref[...], sin_ref[...], k_ref, HEAD_DIM ** -0.5)
        for hd in range(RET_HEADS):
            sl = slice(hd * HEAD_DIM, (hd + 1) * HEAD_DIM)
            kd_scr[:, sl] = (ks[hd] * dec_scr[:, sl]).astype(BF16)

    @pl.when(j == 2)
    def _():
        vb = acc.astype(BF16)
        v_ref[0] = vb
        tile_dec = jnp.exp(_log_sigmoid(decb_ref[...]) * float(tm))
        for hd in range(RET_HEADS):
            sl = slice(hd * HEAD_DIM, (hd + 1) * HEAD_DIM)
            s_old = s_scr[hd]
            sbound_ref[0, 0, hd] = s_old
            s_scr[hd] = tile_dec[:, sl] * s_old + _dot_t0(kd_scr[:, sl], vb[:, sl])

    @pl.when(j == 3)
    def _():
        gate_ref[0] = (acc * _sigmoid(acc)).astype(gate_ref.dtype)

    @pl.when(j == 4)
    def _():
        b_scr[...] = acc

    @pl.when(j == 5)
    def _():
        c_scr[...] = acc

    @pl.when(j == 6)
    def _():
        u = c_scr[...] * acc
        col = lax.broadcasted_iota(jnp.int32, (tm, 1), 0) % GRID_W
        up = jnp.where(col == 0, 0.0, pltpu.roll(u, 1, 0))
        un = jnp.where(col == GRID_W - 1, 0.0, pltpu.roll(u, tm - 1, 0))
        y = b_scr[...] * (cw_ref[0:1, :] * up + cw_ref[1:2, :] * u + cw_ref[2:3, :] * un)
        conv_ref[0] = y.astype(conv_ref.dtype)


def _inproj_call(x, mod, norm_g, w_in, cos_t, sin_t, conv_w, dec_b, sb0):
    b, n, d = x.shape
    tm = TOKEN_TILE
    nt = n // tm
    r = RET_WIDTH
    ngroups = w_in.shape[1] // r

    def tok(bi, i, j):
        return (bi, nt - 1 - i, 0)

    seq = jax.ShapeDtypeStruct((b, n, r), BF16)
    seq_spec = pl.BlockSpec((1, tm, r), tok)
    return pl.pallas_call(
        _inproj_kernel,
        grid=(b, nt, ngroups),
        in_specs=[
            pl.BlockSpec((1, tm, d), tok),
            pl.BlockSpec((MOD_ROWS, d), lambda bi, i, j: (0, 0)),
            pl.BlockSpec((MOD_ROWS, d), lambda bi, i, j: (0, 1)),
            pl.BlockSpec((1, d), lambda bi, i, j: (0, 0)),
            pl.BlockSpec((d, r), lambda bi, i, j: (0, j)),
            pl.BlockSpec((tm, HEAD_DIM), lambda bi, i, j: (nt - 1 - i, 0)),
            pl.BlockSpec((tm, HEAD_DIM), lambda bi, i, j: (nt - 1 - i, 0)),
            pl.BlockSpec((3, CONV_CH), lambda bi, i, j: (0, 0)),
            pl.BlockSpec((1, r), lambda bi, i, j: (0, 0)),
            pl.BlockSpec((1, RET_HEADS, HEAD_DIM, HEAD_DIM), lambda bi, i, j: (bi, 0, 0, 0)),
        ],
        out_specs=[
            seq_spec, seq_spec, seq_spec, seq_spec, seq_spec,
            pl.BlockSpec((1, 1, RET_HEADS, HEAD_DIM, HEAD_DIM),
                         lambda bi, i, j: (bi, nt - 1 - i, 0, 0, 0)),
        ],
        out_shape=[seq, seq, seq, seq, seq,
                   jax.ShapeDtypeStruct((b, nt, RET_HEADS, HEAD_DIM, HEAD_DIM), F32)],
        scratch_shapes=[
            pltpu.VMEM((tm, d), BF16),
            pltpu.VMEM((tm, r), BF16),
            pltpu.VMEM((tm, CONV_CH), F32),
            pltpu.VMEM((tm, CONV_CH), F32),
            pltpu.VMEM((RET_HEADS, HEAD_DIM, HEAD_DIM), F32),
            pltpu.VMEM((tm, r), F32),
        ],
        compiler_params=pltpu.CompilerParams(
            dimension_semantics=("arbitrary", "arbitrary", "arbitrary"),
            vmem_limit_bytes=VMEM_LIMIT),
        name="inproj",
    )(x, mod, mod, norm_g, w_in, cos_t, sin_t, conv_w, dec_b, sb0)


def _mix_kernel(q_ref, k_ref, v_ref, gate_ref, conv_ref, x_ref, sbound_ref, sf0_ref,
                wo_r_ref, wo_c_ref, g1_ref, decf_ref, decb_ref,
                o_ref,
                sf_scr, dmat_scr, qdec_scr, kdec_scr, cdec_scr, ret_scr):
    b = pl.program_id(0)
    i = pl.program_id(1)
    tm = ret_scr.shape[0]
    c = dmat_scr.shape[1]
    nc = tm // c

    @pl.when(i == 0)
    def _():
        sf_scr[...] = sf0_ref[0]

    @pl.when((b == 0) & (i == 0))
    def _():
        lgf = _log_sigmoid(decf_ref[...])
        lgb = _log_sigmoid(decb_ref[...])
        ri = lax.broadcasted_iota(jnp.int32, (c, c), 0)
        ci = lax.broadcasted_iota(jnp.int32, (c, c), 1)
        rel = (ri - ci).astype(F32)
        t = lax.broadcasted_iota(jnp.int32, (c, HEAD_DIM), 0).astype(F32)
        for hd in range(RET_HEADS):
            f = lgf[hd:hd + 1, :]
            g = lgb[hd:hd + 1, :]
            dmat_scr[hd] = jnp.where(rel >= 0, jnp.exp(f * jnp.maximum(rel, 0.0)),
                                     jnp.exp(g * jnp.maximum(-rel, 0.0)))
            fh = f[:, :HEAD_DIM]
            gh = g[:, :HEAD_DIM]
            qdec_scr[hd, :, :HEAD_DIM] = jnp.exp(fh * (t + 1.0))
            qdec_scr[hd, :, HEAD_DIM:] = jnp.exp(gh * (float(c) - t))
            kdec_scr[hd, :, :HEAD_DIM] = jnp.exp(fh * (float(c) - 1.0 - t))
            kdec_scr[hd, :, HEAD_DIM:] = jnp.exp(gh * t)
            cdec_scr[hd, :, :HEAD_DIM] = jnp.broadcast_to(jnp.exp(fh * float(c)), (8, HEAD_DIM))
            cdec_scr[hd, :, HEAD_DIM:] = jnp.broadcast_to(jnp.exp(gh * float(c)), (8, HEAD_DIM))

    for hd in range(RET_HEADS):
        sl = slice(hd * HEAD_DIM, (hd + 1) * HEAD_DIM)
        cf = cdec_scr[hd, 0:1, :HEAD_DIM]
        cb = cdec_scr[hd, 0:1, HEAD_DIM:]
        ks, vs, kds = [], [], []
        for ch in range(nc):
            rows = slice(ch * c, (ch + 1) * c)
            kh = k_ref[0, rows, sl]
            ks.append(kh)
            vs.append(v_ref[0, rows, sl])
            kd = jnp.concatenate([kh, kh], axis=1).astype(F32) * kdec_scr[hd]
            kds.append(kd.astype(BF16))
        sb = [None] * nc
        sb[nc - 1] = sbound_ref[0, 0, hd]
        for ch in range(nc - 1, 0, -1):
            sb[ch - 1] = cb * sb[ch] + _dot_t0(kds[ch][:, HEAD_DIM:], vs[ch])
        sf = sf_scr[hd]
        for ch in range(nc):
            rows = slice(ch * c, (ch + 1) * c)
            qh = q_ref[0, rows, sl]
            p = (_dot_nt(qh, ks[ch]) * dmat_scr[hd]).astype(BF16)
            inner = _dot(p, vs[ch])
            q2 = (jnp.concatenate([qh, qh], axis=1).astype(F32) * qdec_scr[hd]).astype(BF16)
            s2 = jnp.concatenate([sf, sb[ch]], axis=0).astype(BF16)
            o = inner + _dot(q2, s2)
            ms = jnp.mean(o * o, axis=-1, keepdims=True)
            o = o * lax.rsqrt(ms + EPS) * gate_ref[0, rows, sl].astype(F32)
            ret_scr[rows, sl] = o.astype(BF16)
            sf = cf * sf + _dot_t0(kds[ch][:, :HEAD_DIM], vs[ch])
        sf_scr[hd] = sf

    mix = _dot(ret_scr[...], wo_r_ref[...]) + _dot(conv_ref[0], wo_c_ref[...])
    o_ref[0] = x_ref[0] + g1_ref[pl.ds(b, 1), :] * mix


def _mix_call(q, k, v, gate, conv, x, sbound, sf0, w_out, mod, dec_f_rows, dec_b_rows):
    b, n, d = x.shape
    tm = TOKEN_TILE
    nt = n // tm
    r = RET_WIDTH
    c = RET_CHUNK

    def tok(bi, i):
        return (bi, i, 0)

    seq_spec = pl.BlockSpec((1, tm, r), tok)
    return pl.pallas_call(
        _mix_kernel,
        grid=(b, nt),
        in_specs=[
            seq_spec, seq_spec, seq_spec, seq_spec, seq_spec,
            pl.BlockSpec((1, tm, d), tok),
            pl.BlockSpec((1, 1, RET_HEADS, HEAD_DIM, HEAD_DIM), lambda bi, i: (bi, i, 0, 0, 0)),
            pl.BlockSpec((1, RET_HEADS, HEAD_DIM, HEAD_DIM), lambda bi, i: (bi, 0, 0, 0)),
            pl.BlockSpec((r, d), lambda bi, i: (0, 0)),
            pl.BlockSpec((CONV_CH, d), lambda bi, i: (1, 0)),
            pl.BlockSpec((MOD_ROWS, d), lambda bi, i: (0, 2)),
            pl.BlockSpec((RET_HEADS, c), lambda bi, i: (0, 0)),
            pl.BlockSpec((RET_HEADS, c), lambda bi, i: (0, 0)),
        ],
        out_specs=pl.BlockSpec((1, tm, d), tok),
        out_shape=jax.ShapeDtypeStruct((b, n, d), F32),
        scratch_shapes=[
            pltpu.VMEM((RET_HEADS, HEAD_DIM, HEAD_DIM), F32),
            pltpu.VMEM((RET_HEADS, c, c), F32),
            pltpu.VMEM((RET_HEADS, c, 2 * HEAD_DIM), F32),
            pltpu.VMEM((RET_HEADS, c, 2 * HEAD_DIM), F32),
            pltpu.VMEM((RET_HEADS, 8, 2 * HEAD_DIM), F32),
            pltpu.VMEM((tm, r), BF16),
        ],
        compiler_params=pltpu.CompilerParams(
            dimension_semantics=("arbitrary", "arbitrary"), vmem_limit_bytes=VMEM_LIMIT),
        name="mix",
    )(q, k, v, gate, conv, x, sbound, sf0, w_out, w_out, mod, dec_f_rows, dec_b_rows)


def _ffn_kernel(x_ref, xp_ref, xn_ref, sh_ref, sc_ref, gt_ref, g2_ref, gf_ref,
                wa_ref, wb_ref, cw_ref, cb_ref, wd_ref,
                o_ref,
                h_scr, acc_scr):
    b = pl.program_id(0)
    i = pl.program_id(1)
    f = pl.program_id(2)
    nt = pl.num_programs(1)
    nf = pl.num_programs(2)
    tm = acc_scr.shape[0]
    halo = GRID_W

    @pl.when(f == 0)
    def _():
        sh = sh_ref[pl.ds(b, 1), :]
        sc = sc_ref[pl.ds(b, 1), :]
        g = g2_ref[...]
        has_prev = (i > 0).astype(F32)
        has_next = (i < nt - 1).astype(F32)
        h_scr[0:halo, :] = (_norm_mod(xp_ref[0], g, sh, sc) * has_prev).astype(BF16)
        h_scr[halo:halo + tm, :] = _norm_mod(x_ref[0], g, sh, sc).astype(BF16)
        h_scr[halo + tm:, :] = (_norm_mod(xn_ref[0], g, sh, sc) * has_next).astype(BF16)
        acc_scr[...] = jnp.zeros_like(acc_scr)

    a_ext = _dot(h_scr[...], wa_ref[...])
    a = cw_ref[0:1, :] * a_ext[0:tm] + cw_ref[1:2, :] * a_ext[halo:halo + tm] \
        + cw_ref[2:3, :] * a_ext[2 * halo:2 * halo + tm] + cb_ref[...]
    gate = _dot(h_scr[halo:halo + tm, :], wb_ref[...])
    act = (a * _sigmoid(a) * gate).astype(BF16)
    acc_scr[...] += _dot(act, wd_ref[...])

    @pl.when(f == nf - 1)
    def _():
        y = x_ref[0] + gt_ref[pl.ds(b, 1), :] * acc_scr[...]
        ms = jnp.mean(y * y, axis=-1, keepdims=True)
        o_ref[0] = y * lax.rsqrt(ms + EPS) * gf_ref[...]


def _ffn_call(x, mod, norm_g, final_g, w_up, ffn_conv_w, ffn_conv_b, w_down):
    b, n, d = x.shape
    d_ff = w_down.shape[0]
    tm = TOKEN_TILE
    tf = FF_TILE
    nt = n // tm
    nf = d_ff // tf
    halo = GRID_W
    hb = tm // halo
    nhb = n // halo

    return pl.pallas_call(
        _ffn_kernel,
        grid=(b, nt, nf),
        in_specs=[
            pl.BlockSpec((1, tm, d), lambda bi, i, f: (bi, i, 0)),
            pl.BlockSpec((1, halo, d), lambda bi, i, f: (bi, jnp.maximum(i * hb - 1, 0), 0)),
            pl.BlockSpec((1, halo, d), lambda bi, i, f: (bi, jnp.minimum((i + 1) * hb, nhb - 1), 0)),
            pl.BlockSpec((MOD_ROWS, d), lambda bi, i, f: (0, 3)),
            pl.BlockSpec((MOD_ROWS, d), lambda bi, i, f: (0, 4)),
            pl.BlockSpec((MOD_ROWS, d), lambda bi, i, f: (0, 5)),
            pl.BlockSpec((1, d), lambda bi, i, f: (0, 0)),
            pl.BlockSpec((1, d), lambda bi, i, f: (0, 0)),
            pl.BlockSpec((d, tf), lambda bi, i, f: (0, f)),
            pl.BlockSpec((d, tf), lambda bi, i, f: (0, nf + f)),
            pl.BlockSpec((3, tf), lambda bi, i, f: (0, f)),
            pl.BlockSpec((1, tf), lambda bi, i, f: (0, f)),
            pl.BlockSpec((tf, d), lambda bi, i, f: (f, 0)),
        ],
        out_specs=pl.BlockSpec((1, tm, d), lambda bi, i, f: (bi, i, 0)),
        out_shape=jax.ShapeDtypeStruct((b, n, d), F32),
        scratch_shapes=[
            pltpu.VMEM((tm + 2 * halo, d), BF16),
            pltpu.VMEM((tm, d), F32),
        ],
        compiler_params=pltpu.CompilerParams(
            dimension_semantics=("arbitrary", "arbitrary", "arbitrary"),
            vmem_limit_bytes=VMEM_LIMIT),
        name="ffn",
    )(x, x, x, mod, mod, mod, norm_g, final_g, w_up, w_up, ffn_conv_w, ffn_conv_b, w_down)


def _rope_tables(n):
    pos = jnp.arange(n, dtype=jnp.int32)
    row = (pos // GRID_W).astype(F32)
    col = (pos % GRID_W).astype(F32)
    dh = HEAD_DIM // 2
    freqs = ROPE_THETA ** (-jnp.arange(0, dh, 2, dtype=F32) / dh)
    parts_c, parts_s = [], []
    for p in (row, col):
        ang = p[:, None] * freqs[None, :]
        cos = jnp.cos(ang)
        sin = jnp.sin(ang)
        parts_c += [cos, cos]
        parts_s += [-sin, sin]
    return jnp.concatenate(parts_c, axis=-1), jnp.concatenate(parts_s, axis=-1)


def kernel(x, c, ctx, c_ctx, w_mod, b_mod, norm1_g, w_in, ret_decay_fwd, ret_decay_bwd,
           conv_w, w_out, norm2_g, w_up, ffn_conv_w, ffn_conv_b, w_down, final_g):
    b, n, d = x.shape
    depth = w_mod.shape[0]
    assert depth == 1 and n % TOKEN_TILE == 0 and TOKEN_TILE % RET_CHUNK == 0
    cos_t, sin_t = _rope_tables(n)
    craw = jnp.concatenate(
        [c, c_ctx[None, :], jnp.zeros((MOD_ROWS - b - 1, d), F32)], axis=0)

    layer = 0
    mod = _mod_call(craw, w_mod[layer], b_mod[layer][None, :])
    w_in_b = w_in[layer].astype(BF16)
    w_out_b = w_out[layer].astype(BF16)
    w_up_b = w_up[layer].astype(BF16)
    w_down_b = w_down[layer].astype(BF16)
    dec_f = jnp.repeat(ret_decay_fwd[layer], HEAD_DIM)[None, :]
    dec_b = jnp.repeat(ret_decay_bwd[layer], HEAD_DIM)[None, :]
    dec_f_rows = jnp.broadcast_to(ret_decay_fwd[layer][:, None], (RET_HEADS, RET_CHUNK))
    dec_b_rows = jnp.broadcast_to(ret_decay_bwd[layer][:, None], (RET_HEADS, RET_CHUNK))
    g1 = norm1_g[layer][None, :]

    sf0, sb0 = _ctx_call(ctx, mod, g1, w_in_b, dec_f, dec_b)
    q, k, v, gate, conv, sbound = _inproj_call(
        x, mod, g1, w_in_b, cos_t, sin_t, conv_w[layer], dec_b, sb0)
    x_mid = _mix_call(q, k, v, gate, conv, x, sbound, sf0, w_out_b, mod,
                      dec_f_rows, dec_b_rows)
    return _ffn_call(x_mid, mod, norm2_g[layer][None, :], final_g[None, :], w_up_b,
                     ffn_conv_w[layer], ffn_conv_b[layer][None, :], w_down_b)
```

```python
import functools

import jax
import jax.numpy as jnp
import numpy as np
from jax import lax
from jax.experimental import pallas as pl
from jax.experimental.pallas import tpu as pltpu

F32 = jnp.float32
BF16 = jnp.bfloat16

GRID_W = 64
RET_HEADS = 8
HEAD_DIM = 128
RET_WIDTH = RET_HEADS * HEAD_DIM
CONV_CH = 1024
N_MOD = 6
EPS = 1e-6
ROPE_THETA = 10000.0

MOD_ROWS = 8
INPROJ_TILE = 256
MIX_TILE = 512
FFN_TILE = 512
RET_CHUNK = 256
FF_TILE = 512
MOD_COL_TILE = 1024
VMEM_LIMIT = 56 * 1024 * 1024


def _sigmoid(x):
    return 1.0 / (1.0 + jnp.exp(-x))


def _log_sigmoid(x):
    return jnp.minimum(x, 0.0) - jnp.log(1.0 + jnp.exp(-jnp.abs(x)))


def _norm_mod(x, gain, shift, scale):
    ms = jnp.mean(x * x, axis=-1, keepdims=True)
    y = x * lax.rsqrt(ms + EPS) * gain
    return y * (1.0 + scale) + shift


def _dot(a, b):
    return jnp.dot(a, b, preferred_element_type=F32)


def _dot_t0(a, b):
    return lax.dot_general(a, b, (((0,), (0,)), ((), ())), preferred_element_type=F32)


def _dot_nt(a, b):
    return lax.dot_general(a, b, (((1,), (1,)), ((), ())), preferred_element_type=F32)


def _mod_kernel(c_ref, w_ref, b_ref, o_ref):
    a = c_ref[...]
    a = (a * _sigmoid(a)).astype(BF16)
    o_ref[...] = _dot(a, w_ref[...].astype(BF16)) + b_ref[...]


def _mod_call(craw, w_mod, b_mod):
    d, n = w_mod.shape
    tn = MOD_COL_TILE
    return pl.pallas_call(
        _mod_kernel,
        grid=(n // tn,),
        in_specs=[
            pl.BlockSpec((MOD_ROWS, d), lambda j: (0, 0)),
            pl.BlockSpec((d, tn), lambda j: (0, j)),
            pl.BlockSpec((1, tn), lambda j: (0, j)),
        ],
        out_specs=pl.BlockSpec((MOD_ROWS, tn), lambda j: (0, j)),
        out_shape=jax.ShapeDtypeStruct((MOD_ROWS, n), F32),
        compiler_params=pltpu.CompilerParams(
            dimension_semantics=("arbitrary",), vmem_limit_bytes=VMEM_LIMIT),
        name="mod",
    )(craw, w_mod, b_mod)


def _ctx_kernel(ctx_row, x_ref, sh_ref, sc_ref, g_ref, wk_ref, wv_ref, decf_ref, decb_ref,
                sf_ref, sb_ref):
    x = x_ref[0]
    length = x.shape[0]
    h = _norm_mod(x, g_ref[...], sh_ref[ctx_row:ctx_row + 1, :],
                  sc_ref[ctx_row:ctx_row + 1, :]).astype(BF16)
    k = _dot(h, wk_ref[...]) * (HEAD_DIM ** -0.5)
    v = _dot(h, wv_ref[...]).astype(BF16)
    t = lax.broadcasted_iota(jnp.int32, k.shape, 0).astype(F32)
    wf = jnp.exp(_log_sigmoid(decf_ref[...]) * (length - 1.0 - t))
    wb = jnp.exp(_log_sigmoid(decb_ref[...]) * t)
    kf = (k * wf).astype(BF16)
    kb = (k * wb).astype(BF16)
    for hd in range(RET_HEADS):
        sl = slice(hd * HEAD_DIM, (hd + 1) * HEAD_DIM)
        sf_ref[0, hd] = _dot_t0(kf[:, sl], v[:, sl])
        sb_ref[0, hd] = _dot_t0(kb[:, sl], v[:, sl])


def _ctx_call(ctx, mod, norm_g, w_in, dec_f, dec_b):
    b, length, d = ctx.shape
    r = RET_WIDTH
    state = jax.ShapeDtypeStruct((b, RET_HEADS, HEAD_DIM, HEAD_DIM), F32)
    state_spec = pl.BlockSpec((1, RET_HEADS, HEAD_DIM, HEAD_DIM), lambda i: (i, 0, 0, 0))
    return pl.pallas_call(
        functools.partial(_ctx_kernel, b),
        grid=(b,),
        in_specs=[
            pl.BlockSpec((1, length, d), lambda i: (i, 0, 0)),
            pl.BlockSpec((MOD_ROWS, d), lambda i: (0, 0)),
            pl.BlockSpec((MOD_ROWS, d), lambda i: (0, 1)),
            pl.BlockSpec((1, d), lambda i: (0, 0)),
            pl.BlockSpec((d, r), lambda i: (0, 1)),
            pl.BlockSpec((d, r), lambda i: (0, 2)),
            pl.BlockSpec((1, r), lambda i: (0, 0)),
            pl.BlockSpec((1, r), lambda i: (0, 0)),
        ],
        out_specs=[state_spec, state_spec],
        out_shape=[state, state],
        compiler_params=pltpu.CompilerParams(
            dimension_semantics=("arbitrary",), vmem_limit_bytes=VMEM_LIMIT),
        name="ctx",
    )(ctx, mod, mod, norm_g, w_in, w_in, dec_f, dec_b)


def _rope_store(acc, cos, sin, out_ref, scale):
    lane = lax.broadcasted_iota(jnp.int32, (1, HEAD_DIM), 1)
    lo = (lane % 64) < 32
    outs = []
    for hd in range(RET_HEADS):
        sl = slice(hd * HEAD_DIM, (hd + 1) * HEAD_DIM)
        xh = acc[:, sl]
        partner = jnp.where(lo, pltpu.roll(xh, 96, 1), pltpu.roll(xh, 32, 1))
        y = xh * cos + partner * sin
        if scale != 1.0:
            y = y * scale
        out_ref[0, :, sl] = y.astype(out_ref.dtype)
        outs.append(y)
    return outs


def _inproj_kernel(x_ref, sh_ref, sc_ref, g_ref, w_ref, cos_ref, sin_ref, cw_ref, decb_ref,
                   sb0_ref,
                   q_ref, k_ref, v_ref, gate_ref, conv_ref, sbound_ref,
                   s_scr, dec_scr):
    b = pl.program_id(0)
    i = pl.program_id(1)
    tm = dec_scr.shape[0]
    r = RET_WIDTH

    @pl.when(i == 0)
    def _():
        s_scr[...] = sb0_ref[0]
        t = lax.broadcasted_iota(jnp.int32, dec_scr.shape, 0).astype(F32)
        dec_scr[...] = jnp.exp(_log_sigmoid(decb_ref[...]) * t)

    h = _norm_mod(x_ref[0], g_ref[...], sh_ref[pl.ds(b, 1), :], sc_ref[pl.ds(b, 1), :]).astype(BF16)

    def proj(group):
        return _dot(h, w_ref[:, group * r:(group + 1) * r])

    cos = cos_ref[...]
    sin = sin_ref[...]
    _rope_store(proj(0), cos, sin, q_ref, 1.0)
    ks = _rope_store(proj(1), cos, sin, k_ref, HEAD_DIM ** -0.5)
    vb = proj(2).astype(BF16)
    v_ref[0] = vb
    tile_dec = jnp.exp(_log_sigmoid(decb_ref[...]) * float(tm))
    for hd in range(RET_HEADS):
        sl = slice(hd * HEAD_DIM, (hd + 1) * HEAD_DIM)
        kd = (ks[hd] * dec_scr[:, sl]).astype(BF16)
        s_old = s_scr[hd]
        sbound_ref[0, 0, hd] = s_old
        s_scr[hd] = tile_dec[:, sl] * s_old + _dot_t0(kd, vb[:, sl])

    g = proj(3)
    gate_ref[0] = (g * _sigmoid(g)).astype(gate_ref.dtype)

    u = proj(5) * proj(6)
    col = lax.broadcasted_iota(jnp.int32, (tm, 1), 0) % GRID_W
    up = jnp.where(col == 0, 0.0, pltpu.roll(u, 1, 0))
    un = jnp.where(col == GRID_W - 1, 0.0, pltpu.roll(u, tm - 1, 0))
    y = proj(4) * (cw_ref[0:1, :] * up + cw_ref[1:2, :] * u + cw_ref[2:3, :] * un)
    conv_ref[0] = y.astype(conv_ref.dtype)


def _inproj_call(x, mod, norm_g, w_in, cos_t, sin_t, conv_w, dec_b, sb0):
    b, n, d = x.shape
    tm = INPROJ_TILE
    nt = n // tm
    r = RET_WIDTH

    def tok(bi, i):
        return (bi, nt - 1 - i, 0)

    seq = jax.ShapeDtypeStruct((b, n, r), BF16)
    seq_spec = pl.BlockSpec((1, tm, r), tok)
    return pl.pallas_call(
        _inproj_kernel,
        grid=(b, nt),
        in_specs=[
            pl.BlockSpec((1, tm, d), tok),
            pl.BlockSpec((MOD_ROWS, d), lambda bi, i: (0, 0)),
            pl.BlockSpec((MOD_ROWS, d), lambda bi, i: (0, 1)),
            pl.BlockSpec((1, d), lambda bi, i: (0, 0)),
            pl.BlockSpec(w_in.shape, lambda bi, i: (0, 0), pipeline_mode=pl.Buffered(1)),
            pl.BlockSpec((tm, HEAD_DIM), lambda bi, i: (nt - 1 - i, 0)),
            pl.BlockSpec((tm, HEAD_DIM), lambda bi, i: (nt - 1 - i, 0)),
            pl.BlockSpec((3, CONV_CH), lambda bi, i: (0, 0)),
            pl.BlockSpec((1, r), lambda bi, i: (0, 0)),
            pl.BlockSpec((1, RET_HEADS, HEAD_DIM, HEAD_DIM), lambda bi, i: (bi, 0, 0, 0)),
        ],
        out_specs=[
            seq_spec, seq_spec, seq_spec, seq_spec, seq_spec,
            pl.BlockSpec((1, 1, RET_HEADS, HEAD_DIM, HEAD_DIM),
                         lambda bi, i: (bi, nt - 1 - i, 0, 0, 0)),
        ],
        out_shape=[seq, seq, seq, seq, seq,
                   jax.ShapeDtypeStruct((b, nt, RET_HEADS, HEAD_DIM, HEAD_DIM), F32)],
        scratch_shapes=[
            pltpu.VMEM((RET_HEADS, HEAD_DIM, HEAD_DIM), F32),
            pltpu.VMEM((tm, r), F32),
        ],
        compiler_params=pltpu.CompilerParams(
            dimension_semantics=("arbitrary", "arbitrary"),
            vmem_limit_bytes=VMEM_LIMIT),
        name="inproj",
    )(x, mod, mod, norm_g, w_in, cos_t, sin_t, conv_w, dec_b, sb0)


def _mix_kernel(q_ref, k_ref, v_ref, gate_ref, conv_ref, x_ref, sbound_ref, sf0_ref,
                wo_r_ref, wo_c_ref, g1_ref, decf_ref, decb_ref,
                o_ref,
                sf_scr, dmat_scr, qdec_scr, kdec_scr, cdec_scr, ret_scr):
    b = pl.program_id(0)
    i = pl.program_id(1)
    tm = ret_scr.shape[0]
    c = dmat_scr.shape[1]
    nc = tm // c

    @pl.when(i == 0)
    def _():
        sf_scr[...] = sf0_ref[0]

    @pl.when((b == 0) & (i == 0))
    def _():
        lgf = _log_sigmoid(decf_ref[...])
        lgb = _log_sigmoid(decb_ref[...])
        ri = lax.broadcasted_iota(jnp.int32, (c, c), 0)
        ci = lax.broadcasted_iota(jnp.int32, (c, c), 1)
        rel = (ri - ci).astype(F32)
        t = lax.broadcasted_iota(jnp.int32, (c, HEAD_DIM), 0).astype(F32)
        for hd in range(RET_HEADS):
            f = lgf[hd:hd + 1, :]
            g = lgb[hd:hd + 1, :]
            dmat_scr[hd] = jnp.where(rel >= 0, jnp.exp(f * jnp.maximum(rel, 0.0)),
                                     jnp.exp(g * jnp.maximum(-rel, 0.0)))
            fh = f[:, :HEAD_DIM]
            gh = g[:, :HEAD_DIM]
            qdec_scr[hd, :, :HEAD_DIM] = jnp.exp(fh * (t + 1.0))
            qdec_scr[hd, :, HEAD_DIM:] = jnp.exp(gh * (float(c) - t))
            kdec_scr[hd, :, :HEAD_DIM] = jnp.exp(fh * (float(c) - 1.0 - t))
            kdec_scr[hd, :, HEAD_DIM:] = jnp.exp(gh * t)
            cdec_scr[hd, :, :HEAD_DIM] = jnp.broadcast_to(jnp.exp(fh * float(c)), (8, HEAD_DIM))
            cdec_scr[hd, :, HEAD_DIM:] = jnp.broadcast_to(jnp.exp(gh * float(c)), (8, HEAD_DIM))

    for hd in range(RET_HEADS):
        sl = slice(hd * HEAD_DIM, (hd + 1) * HEAD_DIM)
        cf = cdec_scr[hd, 0:1, :HEAD_DIM]
        cb = cdec_scr[hd, 0:1, HEAD_DIM:]
        ks, vs, kds = [], [], []
        for ch in range(nc):
            rows = slice(ch * c, (ch + 1) * c)
            kh = k_ref[0, rows, sl]
            ks.append(kh)
            vs.append(v_ref[0, rows, sl])
            kd = jnp.concatenate([kh, kh], axis=1).astype(F32) * kdec_scr[hd]
            kds.append(kd.astype(BF16))
        sb = [None] * nc
        sb[nc - 1] = sbound_ref[0, 0, hd]
        for ch in range(nc - 1, 0, -1):
            sb[ch - 1] = cb * sb[ch] + _dot_t0(kds[ch][:, HEAD_DIM:], vs[ch])
        sf = sf_scr[hd]
        for ch in range(nc):
            rows = slice(ch * c, (ch + 1) * c)
            qh = q_ref[0, rows, sl]
            p = (_dot_nt(qh, ks[ch]) * dmat_scr[hd]).astype(BF16)
            inner = _dot(p, vs[ch])
            q2 = (jnp.concatenate([qh, qh], axis=1).astype(F32) * qdec_scr[hd]).astype(BF16)
            s2 = jnp.concatenate([sf, sb[ch]], axis=0).astype(BF16)
            o = inner + _dot(q2, s2)
            ms = jnp.mean(o * o, axis=-1, keepdims=True)
            o = o * lax.rsqrt(ms + EPS) * gate_ref[0, rows, sl].astype(F32)
            ret_scr[rows, sl] = o.astype(BF16)
            sf = cf * sf + _dot_t0(kds[ch][:, :HEAD_DIM], vs[ch])
        sf_scr[hd] = sf

    mix = _dot(ret_scr[...], wo_r_ref[...]) + _dot(conv_ref[0], wo_c_ref[...])
    o_ref[0] = x_ref[0] + g1_ref[pl.ds(b, 1), :] * mix


def _mix_call(q, k, v, gate, conv, x, sbound, sf0, w_out, mod, dec_f_rows, dec_b_rows):
    b, n, d = x.shape
    tm = MIX_TILE
    nt = n // tm
    bound_per_tile = sbound.shape[1] // nt
    r = RET_WIDTH
    c = RET_CHUNK

    def tok(bi, i):
        return (bi, i, 0)

    seq_spec = pl.BlockSpec((1, tm, r), tok)
    return pl.pallas_call(
        _mix_kernel,
        grid=(b, nt),
        in_specs=[
            seq_spec, seq_spec, seq_spec, seq_spec, seq_spec,
            pl.BlockSpec((1, tm, d), tok),
            pl.BlockSpec((1, 1, RET_HEADS, HEAD_DIM, HEAD_DIM),
                         lambda bi, i: (bi, (i + 1) * bound_per_tile - 1, 0, 0, 0)),
            pl.BlockSpec((1, RET_HEADS, HEAD_DIM, HEAD_DIM), lambda bi, i: (bi, 0, 0, 0)),
            pl.BlockSpec((r, d), lambda bi, i: (0, 0)),
            pl.BlockSpec((CONV_CH, d), lambda bi, i: (1, 0)),
            pl.BlockSpec((MOD_ROWS, d), lambda bi, i: (0, 2)),
            pl.BlockSpec((RET_HEADS, c), lambda bi, i: (0, 0)),
            pl.BlockSpec((RET_HEADS, c), lambda bi, i: (0, 0)),
        ],
        out_specs=pl.BlockSpec((1, tm, d), tok),
        out_shape=jax.ShapeDtypeStruct((b, n, d), F32),
        scratch_shapes=[
            pltpu.VMEM((RET_HEADS, HEAD_DIM, HEAD_DIM), F32),
            pltpu.VMEM((RET_HEADS, c, c), F32),
            pltpu.VMEM((RET_HEADS, c, 2 * HEAD_DIM), F32),
            pltpu.VMEM((RET_HEADS, c, 2 * HEAD_DIM), F32),
            pltpu.VMEM((RET_HEADS, 8, 2 * HEAD_DIM), F32),
            pltpu.VMEM((tm, r), BF16),
        ],
        compiler_params=pltpu.CompilerParams(
            dimension_semantics=("arbitrary", "arbitrary"), vmem_limit_bytes=VMEM_LIMIT),
        name="mix",
    )(q, k, v, gate, conv, x, sbound, sf0, w_out, w_out, mod, dec_f_rows, dec_b_rows)


def _ffn_kernel(x_ref, xp_ref, xn_ref, sh_ref, sc_ref, gt_ref, g2_ref, gf_ref,
                wa_ref, wb_ref, cw_ref, cb_ref, wd_ref,
                o_ref,
                h_scr, acc_scr):
    b = pl.program_id(0)
    i = pl.program_id(1)
    f = pl.program_id(2)
    nt = pl.num_programs(1)
    nf = pl.num_programs(2)
    tm = acc_scr.shape[0]
    halo = GRID_W

    @pl.when(f == 0)
    def _():
        sh = sh_ref[pl.ds(b, 1), :]
        sc = sc_ref[pl.ds(b, 1), :]
        g = g2_ref[...]
        has_prev = (i > 0).astype(F32)
        has_next = (i < nt - 1).astype(F32)
        h_scr[0:halo, :] = (_norm_mod(xp_ref[0], g, sh, sc) * has_prev).astype(BF16)
        h_scr[halo:halo + tm, :] = _norm_mod(x_ref[0], g, sh, sc).astype(BF16)
        h_scr[halo + tm:, :] = (_norm_mod(xn_ref[0], g, sh, sc) * has_next).astype(BF16)
        acc_scr[...] = jnp.zeros_like(acc_scr)

    a_ext = _dot(h_scr[...], wa_ref[...])
    a = cw_ref[0:1, :] * a_ext[0:tm] + cw_ref[1:2, :] * a_ext[halo:halo + tm] \
        + cw_ref[2:3, :] * a_ext[2 * halo:2 * halo + tm] + cb_ref[...]
    gate = _dot(h_scr[halo:halo + tm, :], wb_ref[...])
    act = (a * _sigmoid(a) * gate).astype(BF16)
    acc_scr[...] += _dot(act, wd_ref[...])

    @pl.when(f == nf - 1)
    def _():
        y = x_ref[0] + gt_ref[pl.ds(b, 1), :] * acc_scr[...]
        ms = jnp.mean(y * y, axis=-1, keepdims=True)
        o_ref[0] = y * lax.rsqrt(ms + EPS) * gf_ref[...]


def _ffn_call(x, mod, norm_g, final_g, w_up, ffn_conv_w, ffn_conv_b, w_down):
    b, n, d = x.shape
    d_ff = w_down.shape[0]
    tm = FFN_TILE
    tf = FF_TILE
    nt = n // tm
    nf = d_ff // tf
    halo = GRID_W
    hb = tm // halo
    nhb = n // halo

    return pl.pallas_call(
        _ffn_kernel,
        grid=(b, nt, nf),
        in_specs=[
            pl.BlockSpec((1, tm, d), lambda bi, i, f: (bi, i, 0)),
            pl.BlockSpec((1, halo, d), lambda bi, i, f: (bi, jnp.maximum(i * hb - 1, 0), 0)),
            pl.BlockSpec((1, halo, d), lambda bi, i, f: (bi, jnp.minimum((i + 1) * hb, nhb - 1), 0)),
            pl.BlockSpec((MOD_ROWS, d), lambda bi, i, f: (0, 3)),
            pl.BlockSpec((MOD_ROWS, d), lambda bi, i, f: (0, 4)),
            pl.BlockSpec((MOD_ROWS, d), lambda bi, i, f: (0, 5)),
            pl.BlockSpec((1, d), lambda bi, i, f: (0, 0)),
            pl.BlockSpec((1, d), lambda bi, i, f: (0, 0)),
            pl.BlockSpec((d, tf), lambda bi, i, f: (0, f)),
            pl.BlockSpec((d, tf), lambda bi, i, f: (0, nf + f)),
            pl.BlockSpec((3, tf), lambda bi, i, f: (0, f)),
            pl.BlockSpec((1, tf), lambda bi, i, f: (0, f)),
            pl.BlockSpec((tf, d), lambda bi, i, f: (f, 0)),
        ],
        out_specs=pl.BlockSpec((1, tm, d), lambda bi, i, f: (bi, i, 0)),
        out_shape=jax.ShapeDtypeStruct((b, n, d), F32),
        scratch_shapes=[
            pltpu.VMEM((tm + 2 * halo, d), BF16),
            pltpu.VMEM((tm, d), F32),
        ],
        compiler_params=pltpu.CompilerParams(
            dimension_semantics=("arbitrary", "arbitrary", "arbitrary"),
            vmem_limit_bytes=VMEM_LIMIT),
        name="ffn",
    )(x, x, x, mod, mod, mod, norm_g, final_g, w_up, w_up, ffn_conv_w, ffn_conv_b, w_down)


def _rope_tables(n):
    pos = np.arange(n, dtype=np.int32)
    row = (pos // GRID_W).astype(np.float32)
    col = (pos % GRID_W).astype(np.float32)
    dh = HEAD_DIM // 2
    freqs = np.float32(ROPE_THETA) ** (-np.arange(0, dh, 2, dtype=np.float32) / np.float32(dh))
    parts_c, parts_s = [], []
    for p in (row, col):
        ang = (p[:, None] * freqs[None, :]).astype(np.float32)
        cos = np.cos(ang).astype(np.float32)
        sin = np.sin(ang).astype(np.float32)
        parts_c += [cos, cos]
        parts_s += [-sin, sin]
    return (jnp.asarray(np.concatenate(parts_c, axis=-1)),
            jnp.asarray(np.concatenate(parts_s, axis=-1)))


def kernel(x, c, ctx, c_ctx, w_mod, b_mod, norm1_g, w_in, ret_decay_fwd, ret_decay_bwd,
           conv_w, w_out, norm2_g, w_up, ffn_conv_w, ffn_conv_b, w_down, final_g):
    b, n, d = x.shape
    depth = w_mod.shape[0]
    assert depth == 1 and n % MIX_TILE == 0 and n % FFN_TILE == 0
    assert MIX_TILE % INPROJ_TILE == 0 and MIX_TILE % RET_CHUNK == 0
    cos_t, sin_t = _rope_tables(n)
    craw = jnp.concatenate(
        [c, c_ctx[None, :], jnp.zeros((MOD_ROWS - b - 1, d), F32)], axis=0)

    layer = 0
    mod = _mod_call(craw, w_mod[layer], b_mod[layer][None, :])
    w_in_b = w_in[layer].astype(BF16)
    w_out_b = w_out[layer].astype(BF16)
    w_up_b = w_up[layer].astype(BF16)
    w_down_b = w_down[layer].astype(BF16)
    dec_f = jnp.repeat(ret_decay_fwd[layer], HEAD_DIM)[None, :]
    dec_b = jnp.repeat(ret_decay_bwd[layer], HEAD_DIM)[None, :]
    dec_f_rows = jnp.broadcast_to(ret_decay_fwd[layer][:, None], (RET_HEADS, RET_CHUNK))
    dec_b_rows = jnp.broadcast_to(ret_decay_bwd[layer][:, None], (RET_HEADS, RET_CHUNK))
    g1 = norm1_g[layer][None, :]

    sf0, sb0 = _ctx_call(ctx, mod, g1, w_in_b, dec_f, dec_b)
    q, k, v, gate, conv, sbound = _inproj_call(
        x, mod, g1, w_in_b, cos_t, sin_t, conv_w[layer], dec_b, sb0)
    x_mid = _mix_call(q, k, v, gate, conv, x, sbound, sf0, w_out_b, mod,
                      dec_f_rows, dec_b_rows)
    return _ffn_call(x_mid, mod, norm2_g[layer][None, :], final_g[None, :], w_up_b,
                     ffn_conv_w[layer], ffn_conv_b[layer][None, :], w_down_b)
```

```python
import functools

import jax
import jax.numpy as jnp
import numpy as np
from jax import lax
from jax.experimental import pallas as pl
from jax.experimental.pallas import tpu as pltpu

F32 = jnp.float32
BF16 = jnp.bfloat16

GRID_W = 64
RET_HEADS = 8
HEAD_DIM = 128
RET_WIDTH = RET_HEADS * HEAD_DIM
CONV_CH = 1024
N_MOD = 6
EPS = 1e-6
ROPE_THETA = 10000.0

MOD_ROWS = 8
INPROJ_TILE = 256
MIX_TILE = 512
FFN_COLS = 8
RET_CHUNK = 256
FF_TILE = 256
MOD_COL_TILE = 1024
VMEM_LIMIT = 56 * 1024 * 1024


def _sigmoid(x):
    return 1.0 / (1.0 + jnp.exp(-x))


def _log_sigmoid(x):
    return jnp.minimum(x, 0.0) - jnp.log(1.0 + jnp.exp(-jnp.abs(x)))


def _norm_mod(x, gain, shift, scale):
    ms = jnp.mean(x * x, axis=-1, keepdims=True)
    y = x * lax.rsqrt(ms + EPS) * gain
    return y * (1.0 + scale) + shift


def _dot(a, b):
    return jnp.dot(a, b, preferred_element_type=F32)


def _dot_t0(a, b):
    return lax.dot_general(a, b, (((0,), (0,)), ((), ())), preferred_element_type=F32)


def _dot_nt(a, b):
    return lax.dot_general(a, b, (((1,), (1,)), ((), ())), preferred_element_type=F32)


def _mod_kernel(c_ref, w_ref, b_ref, o_ref):
    a = c_ref[...]
    a = (a * _sigmoid(a)).astype(BF16)
    o_ref[...] = _dot(a, w_ref[...].astype(BF16)) + b_ref[...]


def _mod_call(craw, w_mod, b_mod):
    d, n = w_mod.shape
    tn = MOD_COL_TILE
    return pl.pallas_call(
        _mod_kernel,
        grid=(n // tn,),
        in_specs=[
            pl.BlockSpec((MOD_ROWS, d), lambda j: (0, 0)),
            pl.BlockSpec((d, tn), lambda j: (0, j)),
            pl.BlockSpec((1, tn), lambda j: (0, j)),
        ],
        out_specs=pl.BlockSpec((MOD_ROWS, tn), lambda j: (0, j)),
        out_shape=jax.ShapeDtypeStruct((MOD_ROWS, n), F32),
        compiler_params=pltpu.CompilerParams(
            dimension_semantics=("arbitrary",), vmem_limit_bytes=VMEM_LIMIT),
        name="mod",
    )(craw, w_mod, b_mod)


def _ctx_kernel(ctx_row, x_ref, sh_ref, sc_ref, g_ref, wk_ref, wv_ref, decf_ref, decb_ref,
                sf_ref, sb_ref):
    x = x_ref[0]
    length = x.shape[0]
    h = _norm_mod(x, g_ref[...], sh_ref[ctx_row:ctx_row + 1, :],
                  sc_ref[ctx_row:ctx_row + 1, :]).astype(BF16)
    k = _dot(h, wk_ref[...]) * (HEAD_DIM ** -0.5)
    v = _dot(h, wv_ref[...]).astype(BF16)
    t = lax.broadcasted_iota(jnp.int32, k.shape, 0).astype(F32)
    wf = jnp.exp(_log_sigmoid(decf_ref[...]) * (length - 1.0 - t))
    wb = jnp.exp(_log_sigmoid(decb_ref[...]) * t)
    kf = (k * wf).astype(BF16)
    kb = (k * wb).astype(BF16)
    for hd in range(RET_HEADS):
        sl = slice(hd * HEAD_DIM, (hd + 1) * HEAD_DIM)
        sf_ref[0, hd] = _dot_t0(kf[:, sl], v[:, sl])
        sb_ref[0, hd] = _dot_t0(kb[:, sl], v[:, sl])


def _ctx_call(ctx, mod, norm_g, w_in, dec_f, dec_b):
    b, length, d = ctx.shape
    r = RET_WIDTH
    state = jax.ShapeDtypeStruct((b, RET_HEADS, HEAD_DIM, HEAD_DIM), F32)
    state_spec = pl.BlockSpec((1, RET_HEADS, HEAD_DIM, HEAD_DIM), lambda i: (i, 0, 0, 0))
    return pl.pallas_call(
        functools.partial(_ctx_kernel, b),
        grid=(b,),
        in_specs=[
            pl.BlockSpec((1, length, d), lambda i: (i, 0, 0)),
            pl.BlockSpec((MOD_ROWS, d), lambda i: (0, 0)),
            pl.BlockSpec((MOD_ROWS, d), lambda i: (0, 1)),
            pl.BlockSpec((1, d), lambda i: (0, 0)),
            pl.BlockSpec((d, r), lambda i: (0, 1)),
            pl.BlockSpec((d, r), lambda i: (0, 2)),
            pl.BlockSpec((1, r), lambda i: (0, 0)),
            pl.BlockSpec((1, r), lambda i: (0, 0)),
        ],
        out_specs=[state_spec, state_spec],
        out_shape=[state, state],
        compiler_params=pltpu.CompilerParams(
            dimension_semantics=("arbitrary",), vmem_limit_bytes=VMEM_LIMIT),
        name="ctx",
    )(ctx, mod, mod, norm_g, w_in, w_in, dec_f, dec_b)


def _rope_store(acc, cos, sin, out_ref, scale):
    lane = lax.broadcasted_iota(jnp.int32, (1, HEAD_DIM), 1)
    lo = (lane % 64) < 32
    outs = []
    for hd in range(RET_HEADS):
        sl = slice(hd * HEAD_DIM, (hd + 1) * HEAD_DIM)
        xh = acc[:, sl]
        partner = jnp.where(lo, pltpu.roll(xh, 96, 1), pltpu.roll(xh, 32, 1))
        y = xh * cos + partner * sin
        if scale != 1.0:
            y = y * scale
        out_ref[0, :, sl] = y.astype(out_ref.dtype)
        outs.append(y)
    return outs


def _inproj_kernel(x_ref, sh_ref, sc_ref, g_ref, w_ref, cos_ref, sin_ref, cw_ref, decb_ref,
                   sb0_ref,
                   q_ref, k_ref, v_ref, gate_ref, conv_ref, sbound_ref,
                   s_scr, dec_scr):
    b = pl.program_id(0)
    i = pl.program_id(1)
    tm = dec_scr.shape[0]
    r = RET_WIDTH

    @pl.when(i == 0)
    def _():
        s_scr[...] = sb0_ref[0]
        t = lax.broadcasted_iota(jnp.int32, dec_scr.shape, 0).astype(F32)
        dec_scr[...] = jnp.exp(_log_sigmoid(decb_ref[...]) * t)

    h = _norm_mod(x_ref[0], g_ref[...], sh_ref[pl.ds(b, 1), :], sc_ref[pl.ds(b, 1), :]).astype(BF16)

    def proj(group):
        return _dot(h, w_ref[:, group * r:(group + 1) * r])

    cos = cos_ref[...]
    sin = sin_ref[...]
    _rope_store(proj(0), cos, sin, q_ref, 1.0)
    ks = _rope_store(proj(1), cos, sin, k_ref, HEAD_DIM ** -0.5)
    vb = proj(2).astype(BF16)
    v_ref[0] = vb
    tile_dec = jnp.exp(_log_sigmoid(decb_ref[...]) * float(tm))
    for hd in range(RET_HEADS):
        sl = slice(hd * HEAD_DIM, (hd + 1) * HEAD_DIM)
        kd = (ks[hd] * dec_scr[:, sl]).astype(BF16)
        s_old = s_scr[hd]
        sbound_ref[0, 0, hd] = s_old
        s_scr[hd] = tile_dec[:, sl] * s_old + _dot_t0(kd, vb[:, sl])

    g = proj(3)
    gate_ref[0] = (g * _sigmoid(g)).astype(gate_ref.dtype)

    u = proj(5) * proj(6)
    col = lax.broadcasted_iota(jnp.int32, (tm, 1), 0) % GRID_W
    up = jnp.where(col == 0, 0.0, pltpu.roll(u, 1, 0))
    un = jnp.where(col == GRID_W - 1, 0.0, pltpu.roll(u, tm - 1, 0))
    y = proj(4) * (cw_ref[0:1, :] * up + cw_ref[1:2, :] * u + cw_ref[2:3, :] * un)
    conv_ref[0] = y.astype(conv_ref.dtype)


def _inproj_call(x, mod, norm_g, w_in, cos_t, sin_t, conv_w, dec_b, sb0):
    b, n, d = x.shape
    tm = INPROJ_TILE
    nt = n // tm
    r = RET_WIDTH

    def tok(bi, i):
        return (bi, nt - 1 - i, 0)

    seq = jax.ShapeDtypeStruct((b, n, r), BF16)
    seq_spec = pl.BlockSpec((1, tm, r), tok)
    return pl.pallas_call(
        _inproj_kernel,
        grid=(b, nt),
        in_specs=[
            pl.BlockSpec((1, tm, d), tok),
            pl.BlockSpec((MOD_ROWS, d), lambda bi, i: (0, 0)),
            pl.BlockSpec((MOD_ROWS, d), lambda bi, i: (0, 1)),
            pl.BlockSpec((1, d), lambda bi, i: (0, 0)),
            pl.BlockSpec(w_in.shape, lambda bi, i: (0, 0), pipeline_mode=pl.Buffered(1)),
            pl.BlockSpec((tm, HEAD_DIM), lambda bi, i: (nt - 1 - i, 0)),
            pl.BlockSpec((tm, HEAD_DIM), lambda bi, i: (nt - 1 - i, 0)),
            pl.BlockSpec((3, CONV_CH), lambda bi, i: (0, 0)),
            pl.BlockSpec((1, r), lambda bi, i: (0, 0)),
            pl.BlockSpec((1, RET_HEADS, HEAD_DIM, HEAD_DIM), lambda bi, i: (bi, 0, 0, 0)),
        ],
        out_specs=[
            seq_spec, seq_spec, seq_spec, seq_spec, seq_spec,
            pl.BlockSpec((1, 1, RET_HEADS, HEAD_DIM, HEAD_DIM),
                         lambda bi, i: (bi, nt - 1 - i, 0, 0, 0)),
        ],
        out_shape=[seq, seq, seq, seq, seq,
                   jax.ShapeDtypeStruct((b, nt, RET_HEADS, HEAD_DIM, HEAD_DIM), F32)],
        scratch_shapes=[
            pltpu.VMEM((RET_HEADS, HEAD_DIM, HEAD_DIM), F32),
            pltpu.VMEM((tm, r), F32),
        ],
        compiler_params=pltpu.CompilerParams(
            dimension_semantics=("arbitrary", "arbitrary"),
            vmem_limit_bytes=VMEM_LIMIT),
        name="inproj",
    )(x, mod, mod, norm_g, w_in, cos_t, sin_t, conv_w, dec_b, sb0)


def _mix_kernel(q_ref, k_ref, v_ref, gate_ref, conv_ref, x_ref, sbound_ref, sf0_ref,
                wo_r_ref, wo_c_ref, g1_ref, decf_ref, decb_ref,
                o_ref,
                sf_scr, dmat_scr, qdec_scr, kdec_scr, cdec_scr, ret_scr):
    b = pl.program_id(0)
    i = pl.program_id(1)
    tm = ret_scr.shape[0]
    c = dmat_scr.shape[1]
    nc = tm // c

    @pl.when(i == 0)
    def _():
        sf_scr[...] = sf0_ref[0]

    @pl.when((b == 0) & (i == 0))
    def _():
        lgf = _log_sigmoid(decf_ref[...])
        lgb = _log_sigmoid(decb_ref[...])
        ri = lax.broadcasted_iota(jnp.int32, (c, c), 0)
        ci = lax.broadcasted_iota(jnp.int32, (c, c), 1)
        rel = (ri - ci).astype(F32)
        t = lax.broadcasted_iota(jnp.int32, (c, HEAD_DIM), 0).astype(F32)
        for hd in range(RET_HEADS):
            f = lgf[hd:hd + 1, :]
            g = lgb[hd:hd + 1, :]
            dmat_scr[hd] = jnp.where(rel >= 0, jnp.exp(f * jnp.maximum(rel, 0.0)),
                                     jnp.exp(g * jnp.maximum(-rel, 0.0)))
            fh = f[:, :HEAD_DIM]
            gh = g[:, :HEAD_DIM]
            qdec_scr[hd, :, :HEAD_DIM] = jnp.exp(fh * (t + 1.0))
            qdec_scr[hd, :, HEAD_DIM:] = jnp.exp(gh * (float(c) - t))
            kdec_scr[hd, :, :HEAD_DIM] = jnp.exp(fh * (float(c) - 1.0 - t))
            kdec_scr[hd, :, HEAD_DIM:] = jnp.exp(gh * t)
            cdec_scr[hd, :, :HEAD_DIM] = jnp.broadcast_to(jnp.exp(fh * float(c)), (8, HEAD_DIM))
            cdec_scr[hd, :, HEAD_DIM:] = jnp.broadcast_to(jnp.exp(gh * float(c)), (8, HEAD_DIM))

    for hd in range(RET_HEADS):
        sl = slice(hd * HEAD_DIM, (hd + 1) * HEAD_DIM)
        cf = cdec_scr[hd, 0:1, :HEAD_DIM]
        cb = cdec_scr[hd, 0:1, HEAD_DIM:]
        ks, vs, kds = [], [], []
        for ch in range(nc):
            rows = slice(ch * c, (ch + 1) * c)
            kh = k_ref[0, rows, sl]
            ks.append(kh)
            vs.append(v_ref[0, rows, sl])
            kd = jnp.concatenate([kh, kh], axis=1).astype(F32) * kdec_scr[hd]
            kds.append(kd.astype(BF16))
        sb = [None] * nc
        sb[nc - 1] = sbound_ref[0, 0, hd]
        for ch in range(nc - 1, 0, -1):
            sb[ch - 1] = cb * sb[ch] + _dot_t0(kds[ch][:, HEAD_DIM:], vs[ch])
        sf = sf_scr[hd]
        for ch in range(nc):
            rows = slice(ch * c, (ch + 1) * c)
            qh = q_ref[0, rows, sl]
            p = (_dot_nt(qh, ks[ch]) * dmat_scr[hd]).astype(BF16)
            inner = _dot(p, vs[ch])
            q2 = (jnp.concatenate([qh, qh], axis=1).astype(F32) * qdec_scr[hd]).astype(BF16)
            s2 = jnp.concatenate([sf, sb[ch]], axis=0).astype(BF16)
            o = inner + _dot(q2, s2)
            ms = jnp.mean(o * o, axis=-1, keepdims=True)
            o = o * lax.rsqrt(ms + EPS) * gate_ref[0, rows, sl].astype(F32)
            ret_scr[rows, sl] = o.astype(BF16)
            sf = cf * sf + _dot_t0(kds[ch][:, :HEAD_DIM], vs[ch])
        sf_scr[hd] = sf

    mix = _dot(ret_scr[...], wo_r_ref[...]) + _dot(conv_ref[0], wo_c_ref[...])
    o_ref[0] = x_ref[0] + g1_ref[pl.ds(b, 1), :] * mix


def _mix_call(q, k, v, gate, conv, x, sbound, sf0, w_out, mod, dec_f_rows, dec_b_rows):
    b, n, d = x.shape
    tm = MIX_TILE
    nt = n // tm
    bound_per_tile = sbound.shape[1] // nt
    r = RET_WIDTH
    c = RET_CHUNK

    def tok(bi, i):
        return (bi, i, 0)

    seq_spec = pl.BlockSpec((1, tm, r), tok)
    return pl.pallas_call(
        _mix_kernel,
        grid=(b, nt),
        in_specs=[
            seq_spec, seq_spec, seq_spec, seq_spec, seq_spec,
            pl.BlockSpec((1, tm, d), tok),
            pl.BlockSpec((1, 1, RET_HEADS, HEAD_DIM, HEAD_DIM),
                         lambda bi, i: (bi, (i + 1) * bound_per_tile - 1, 0, 0, 0)),
            pl.BlockSpec((1, RET_HEADS, HEAD_DIM, HEAD_DIM), lambda bi, i: (bi, 0, 0, 0)),
            pl.BlockSpec((r, d), lambda bi, i: (0, 0)),
            pl.BlockSpec((CONV_CH, d), lambda bi, i: (1, 0)),
            pl.BlockSpec((MOD_ROWS, d), lambda bi, i: (0, 2)),
            pl.BlockSpec((RET_HEADS, c), lambda bi, i: (0, 0)),
            pl.BlockSpec((RET_HEADS, c), lambda bi, i: (0, 0)),
        ],
        out_specs=pl.BlockSpec((1, tm, d), tok),
        out_shape=jax.ShapeDtypeStruct((b, n, d), F32),
        scratch_shapes=[
            pltpu.VMEM((RET_HEADS, HEAD_DIM, HEAD_DIM), F32),
            pltpu.VMEM((RET_HEADS, c, c), F32),
            pltpu.VMEM((RET_HEADS, c, 2 * HEAD_DIM), F32),
            pltpu.VMEM((RET_HEADS, c, 2 * HEAD_DIM), F32),
            pltpu.VMEM((RET_HEADS, 8, 2 * HEAD_DIM), F32),
            pltpu.VMEM((tm, r), BF16),
        ],
        compiler_params=pltpu.CompilerParams(
            dimension_semantics=("arbitrary", "arbitrary"), vmem_limit_bytes=VMEM_LIMIT),
        name="mix",
    )(q, k, v, gate, conv, x, sbound, sf0, w_out, w_out, mod, dec_f_rows, dec_b_rows)


def _ffn_kernel(x_ref, sh_ref, sc_ref, gt_ref, g2_ref, gf_ref,
                wa_ref, wb_ref, cw_ref, cb_ref, wd_ref,
                o_ref,
                h_scr):
    b = pl.program_id(0)
    f = pl.program_id(2)
    nf = pl.num_programs(2)
    tm, d = h_scr.shape
    tf = wa_ref.shape[1]
    blk = o_ref.shape[1:]
    step = blk[1]

    def body(first, last):
        if first:
            h = _norm_mod(x_ref[0].reshape(tm, d), g2_ref[...], sh_ref[pl.ds(b, 1), :],
                          sc_ref[pl.ds(b, 1), :]).astype(BF16)
            h_scr[...] = h
        else:
            h = h_scr[...]
        a = _dot(h, wa_ref[...])
        zero = jnp.zeros((step, tf), F32)
        a_up = jnp.concatenate([zero, a[:tm - step]], axis=0)
        a_dn = jnp.concatenate([a[step:], zero], axis=0)
        a = cw_ref[0:1, :] * a_up + cw_ref[1:2, :] * a + cw_ref[2:3, :] * a_dn + cb_ref[...]
        act = (a * _sigmoid(a) * _dot(h, wb_ref[...])).astype(BF16)
        y = _dot(act, wd_ref[...])
        acc = y if first else o_ref[0].reshape(tm, d) + y
        if last:
            z = x_ref[0].reshape(tm, d) + gt_ref[pl.ds(b, 1), :] * acc
            ms = jnp.mean(z * z, axis=-1, keepdims=True)
            acc = z * lax.rsqrt(ms + EPS) * gf_ref[...]
        o_ref[0] = acc.reshape(blk)

    @pl.when(f == 0)
    def _():
        body(True, False)

    @pl.when((f > 0) & (f < nf - 1))
    def _():
        body(False, False)

    @pl.when(f == nf - 1)
    def _():
        body(False, True)


def _ffn_call(x, mod, norm_g, final_g, w_up, ffn_conv_w, ffn_conv_b, w_down):
    b, n, d = x.shape
    d_ff = w_down.shape[0]
    rows = n // GRID_W
    cols = FFN_COLS
    tf = FF_TILE
    nf = d_ff // tf
    assert nf >= 2
    xg = x.reshape(b, rows, GRID_W, d)
    tile = pl.BlockSpec((1, rows, cols, d), lambda bi, i, f: (bi, 0, i, 0))

    out = pl.pallas_call(
        _ffn_kernel,
        grid=(b, GRID_W // cols, nf),
        in_specs=[
            tile,
            pl.BlockSpec((MOD_ROWS, d), lambda bi, i, f: (0, 3)),
            pl.BlockSpec((MOD_ROWS, d), lambda bi, i, f: (0, 4)),
            pl.BlockSpec((MOD_ROWS, d), lambda bi, i, f: (0, 5)),
            pl.BlockSpec((1, d), lambda bi, i, f: (0, 0)),
            pl.BlockSpec((1, d), lambda bi, i, f: (0, 0)),
            pl.BlockSpec((d, tf), lambda bi, i, f: (0, f)),
            pl.BlockSpec((d, tf), lambda bi, i, f: (0, nf + f)),
            pl.BlockSpec((3, tf), lambda bi, i, f: (0, f)),
            pl.BlockSpec((1, tf), lambda bi, i, f: (0, f)),
            pl.BlockSpec((tf, d), lambda bi, i, f: (f, 0)),
        ],
        out_specs=tile,
        out_shape=jax.ShapeDtypeStruct((b, rows, GRID_W, d), F32),
        scratch_shapes=[
            pltpu.VMEM((rows * cols, d), BF16),
        ],
        compiler_params=pltpu.CompilerParams(
            dimension_semantics=("arbitrary", "arbitrary", "arbitrary"),
            vmem_limit_bytes=VMEM_LIMIT),
        name="ffn",
    )(xg, mod, mod, mod, norm_g, final_g, w_up, w_up, ffn_conv_w, ffn_conv_b, w_down)
    return out.reshape(b, n, d)


def _rope_tables(n):
    pos = np.arange(n, dtype=np.int32)
    row = (pos // GRID_W).astype(np.float32)
    col = (pos % GRID_W).astype(np.float32)
    dh = HEAD_DIM // 2
    freqs = np.float32(ROPE_THETA) ** (-np.arange(0, dh, 2, dtype=np.float32) / np.float32(dh))
    parts_c, parts_s = [], []
    for p in (row, col):
        ang = (p[:, None] * freqs[None, :]).astype(np.float32)
        cos = np.cos(ang).astype(np.float32)
        sin = np.sin(ang).astype(np.float32)
        parts_c += [cos, cos]
        parts_s += [-sin, sin]
    return (jnp.asarray(np.concatenate(parts_c, axis=-1)),
            jnp.asarray(np.concatenate(parts_s, axis=-1)))


def kernel(x, c, ctx, c_ctx, w_mod, b_mod, norm1_g, w_in, ret_decay_fwd, ret_decay_bwd,
           conv_w, w_out, norm2_g, w_up, ffn_conv_w, ffn_conv_b, w_down, final_g):
    b, n, d = x.shape
    depth = w_mod.shape[0]
    assert depth == 1 and n % MIX_TILE == 0 and n % GRID_W == 0
    assert MIX_TILE % INPROJ_TILE == 0 and MIX_TILE % RET_CHUNK == 0
    cos_t, sin_t = _rope_tables(n)
    craw = jnp.concatenate(
        [c, c_ctx[None, :], jnp.zeros((MOD_ROWS - b - 1, d), F32)], axis=0)

    layer = 0
    mod = _mod_call(craw, w_mod[layer], b_mod[layer][None, :])
    w_in_b = w_in[layer].astype(BF16)
    w_out_b = w_out[layer].astype(BF16)
    w_up_b = w_up[layer].astype(BF16)
    w_down_b = w_down[layer].astype(BF16)
    dec_f = jnp.repeat(ret_decay_fwd[layer], HEAD_DIM)[None, :]
    dec_b = jnp.repeat(ret_decay_bwd[layer], HEAD_DIM)[None, :]
    dec_f_rows = jnp.broadcast_to(ret_decay_fwd[layer][:, None], (RET_HEADS, RET_CHUNK))
    dec_b_rows = jnp.broadcast_to(ret_decay_bwd[layer][:, None], (RET_HEADS, RET_CHUNK))
    g1 = norm1_g[layer][None, :]

    sf0, sb0 = _ctx_call(ctx, mod, g1, w_in_b, dec_f, dec_b)
    q, k, v, gate, conv, sbound = _inproj_call(
        x, mod, g1, w_in_b, cos_t, sin_t, conv_w[layer], dec_b, sb0)
    x_mid = _mix_call(q, k, v, gate, conv, x, sbound, sf0, w_out_b, mod,
                      dec_f_rows, dec_b_rows)
    return _ffn_call(x_mid, mod, norm2_g[layer][None, :], final_g[None, :], w_up_b,
                     ffn_conv_w[layer], ffn_conv_b[layer][None, :], w_down_b)
```

```python
import functools

import jax
import jax.numpy as jnp
import numpy as np
from jax import lax
from jax.experimental import pallas as pl
from jax.experimental.pallas import tpu as pltpu

F32 = jnp.float32
BF16 = jnp.bfloat16

GRID_W = 64
RET_HEADS = 8
HEAD_DIM = 128
RET_WIDTH = RET_HEADS * HEAD_DIM
CONV_CH = 1024
N_MOD = 6
EPS = 1e-6
ROPE_THETA = 10000.0

MOD_ROWS = 8
INPROJ_TILE = 256
MIX_TILE = 512
FFN_COLS = 8
RET_CHUNK = 256
FF_TILE = 256
MOD_COL_TILE = 1024
VMEM_LIMIT = 56 * 1024 * 1024


def _sigmoid(x):
    return 1.0 / (1.0 + jnp.exp(-x))


def _log_sigmoid(x):
    return jnp.minimum(x, 0.0) - jnp.log(1.0 + jnp.exp(-jnp.abs(x)))


def _norm_mod(x, gain, shift, scale):
    ms = jnp.mean(x * x, axis=-1, keepdims=True)
    y = x * lax.rsqrt(ms + EPS) * gain
    return y * (1.0 + scale) + shift


def _dot(a, b):
    return jnp.dot(a, b, preferred_element_type=F32)


def _dot_t0(a, b):
    return lax.dot_general(a, b, (((0,), (0,)), ((), ())), preferred_element_type=F32)


def _mod_kernel(c_ref, w_ref, b_ref, o_ref):
    a = c_ref[...]
    a = (a * _sigmoid(a)).astype(BF16)
    o_ref[...] = _dot(a, w_ref[...].astype(BF16)) + b_ref[...]


def _mod_call(craw, w_mod, b_mod):
    d, n = w_mod.shape
    tn = MOD_COL_TILE
    return pl.pallas_call(
        _mod_kernel,
        grid=(n // tn,),
        in_specs=[
            pl.BlockSpec((MOD_ROWS, d), lambda j: (0, 0)),
            pl.BlockSpec((d, tn), lambda j: (0, j)),
            pl.BlockSpec((1, tn), lambda j: (0, j)),
        ],
        out_specs=pl.BlockSpec((MOD_ROWS, tn), lambda j: (0, j)),
        out_shape=jax.ShapeDtypeStruct((MOD_ROWS, n), F32),
        compiler_params=pltpu.CompilerParams(
            dimension_semantics=("arbitrary",), vmem_limit_bytes=VMEM_LIMIT),
        name="mod",
    )(craw, w_mod, b_mod)


def _ctx_kernel(ctx_row, x_ref, sh_ref, sc_ref, g_ref, wk_ref, wv_ref, decf_ref, decb_ref,
                sf_ref, sb_ref):
    x = x_ref[0]
    length = x.shape[0]
    h = _norm_mod(x, g_ref[...], sh_ref[ctx_row:ctx_row + 1, :],
                  sc_ref[ctx_row:ctx_row + 1, :]).astype(BF16)
    k = _dot(h, wk_ref[...]) * (HEAD_DIM ** -0.5)
    v = _dot(h, wv_ref[...]).astype(BF16)
    t = lax.broadcasted_iota(jnp.int32, k.shape, 0).astype(F32)
    wf = jnp.exp(_log_sigmoid(decf_ref[...]) * (length - 1.0 - t))
    wb = jnp.exp(_log_sigmoid(decb_ref[...]) * t)
    kf = (k * wf).astype(BF16)
    kb = (k * wb).astype(BF16)
    for hd in range(RET_HEADS):
        sl = slice(hd * HEAD_DIM, (hd + 1) * HEAD_DIM)
        sf_ref[0, hd] = _dot_t0(kf[:, sl], v[:, sl])
        sb_ref[0, hd] = _dot_t0(kb[:, sl], v[:, sl])


def _ctx_call(ctx, mod, norm_g, w_in, dec_f, dec_b):
    b, length, d = ctx.shape
    r = RET_WIDTH
    state = jax.ShapeDtypeStruct((b, RET_HEADS, HEAD_DIM, HEAD_DIM), F32)
    state_spec = pl.BlockSpec((1, RET_HEADS, HEAD_DIM, HEAD_DIM), lambda i: (i, 0, 0, 0))
    return pl.pallas_call(
        functools.partial(_ctx_kernel, b),
        grid=(b,),
        in_specs=[
            pl.BlockSpec((1, length, d), lambda i: (i, 0, 0)),
            pl.BlockSpec((MOD_ROWS, d), lambda i: (0, 0)),
            pl.BlockSpec((MOD_ROWS, d), lambda i: (0, 1)),
            pl.BlockSpec((1, d), lambda i: (0, 0)),
            pl.BlockSpec((d, r), lambda i: (0, 1)),
            pl.BlockSpec((d, r), lambda i: (0, 2)),
            pl.BlockSpec((1, r), lambda i: (0, 0)),
            pl.BlockSpec((1, r), lambda i: (0, 0)),
        ],
        out_specs=[state_spec, state_spec],
        out_shape=[state, state],
        compiler_params=pltpu.CompilerParams(
            dimension_semantics=("arbitrary",), vmem_limit_bytes=VMEM_LIMIT),
        name="ctx",
    )(ctx, mod, mod, norm_g, w_in, w_in, dec_f, dec_b)


def _rope(acc, cos, sin, scale):
    lane = lax.broadcasted_iota(jnp.int32, (1, HEAD_DIM), 1)
    lo = (lane % 64) < 32
    outs = []
    for hd in range(RET_HEADS):
        xh = acc[:, hd * HEAD_DIM:(hd + 1) * HEAD_DIM]
        partner = jnp.where(lo, pltpu.roll(xh, 96, 1), pltpu.roll(xh, 32, 1))
        y = xh * cos + partner * sin
        outs.append(y if scale == 1.0 else y * scale)
    return outs


def _inproj_kernel(x_ref, sh_ref, sc_ref, g_ref, w_ref, cos_ref, sin_ref, cw_ref, decb_ref,
                   sb0_ref,
                   q_ref, kt_ref, v_ref, gate_ref, conv_ref, sbound_ref,
                   s_scr, dec_scr):
    b = pl.program_id(0)
    i = pl.program_id(1)
    tm = dec_scr.shape[1]
    r = RET_WIDTH

    @pl.when(i == 0)
    def _():
        s_scr[...] = sb0_ref[0]
        t = lax.broadcasted_iota(jnp.int32, dec_scr.shape, 1).astype(F32)
        dec_scr[...] = jnp.exp(_log_sigmoid(decb_ref[...]) * t)

    h = _norm_mod(x_ref[0], g_ref[...], sh_ref[pl.ds(b, 1), :], sc_ref[pl.ds(b, 1), :]).astype(BF16)

    def proj(group):
        return _dot(h, w_ref[:, group * r:(group + 1) * r])

    cos = cos_ref[...]
    sin = sin_ref[...]
    for hd, qh in enumerate(_rope(proj(0), cos, sin, 1.0)):
        q_ref[0, :, hd * HEAD_DIM:(hd + 1) * HEAD_DIM] = qh.astype(q_ref.dtype)
    ks = _rope(proj(1), cos, sin, HEAD_DIM ** -0.5)
    vb = proj(2).astype(BF16)
    v_ref[0] = vb
    tile_dec = jnp.exp(_log_sigmoid(decb_ref[:, :HEAD_DIM]) * float(tm))
    for hd in range(RET_HEADS):
        kt = ks[hd].T
        kt_ref[0, hd] = kt.astype(kt_ref.dtype)
        ktd = (kt * dec_scr[hd:hd + 1, :]).astype(BF16)
        s_old = s_scr[hd]
        sbound_ref[0, 0, hd] = s_old
        s_scr[hd] = tile_dec[hd:hd + 1, :] * s_old + _dot(ktd, vb[:, hd * HEAD_DIM:(hd + 1) * HEAD_DIM])

    g = proj(3)
    gate_ref[0] = (g * _sigmoid(g)).astype(gate_ref.dtype)

    u = proj(5) * proj(6)
    col = lax.broadcasted_iota(jnp.int32, (tm, 1), 0) % GRID_W
    up = jnp.where(col == 0, 0.0, pltpu.roll(u, 1, 0))
    un = jnp.where(col == GRID_W - 1, 0.0, pltpu.roll(u, tm - 1, 0))
    y = proj(4) * (cw_ref[0:1, :] * up + cw_ref[1:2, :] * u + cw_ref[2:3, :] * un)
    conv_ref[0] = y.astype(conv_ref.dtype)


def _inproj_call(x, mod, norm_g, w_in, cos_t, sin_t, conv_w, dec_b_rows, sb0):
    b, n, d = x.shape
    tm = INPROJ_TILE
    nt = n // tm
    r = RET_WIDTH
    assert dec_b_rows.shape == (RET_HEADS, tm)

    def tok(bi, i):
        return (bi, nt - 1 - i, 0)

    seq = jax.ShapeDtypeStruct((b, n, r), BF16)
    seq_spec = pl.BlockSpec((1, tm, r), tok)
    keys_t = jax.ShapeDtypeStruct((b, RET_HEADS, HEAD_DIM, n), BF16)
    keys_t_spec = pl.BlockSpec((1, RET_HEADS, HEAD_DIM, tm), lambda bi, i: (bi, 0, 0, nt - 1 - i))
    return pl.pallas_call(
        _inproj_kernel,
        grid=(b, nt),
        in_specs=[
            pl.BlockSpec((1, tm, d), tok),
            pl.BlockSpec((MOD_ROWS, d), lambda bi, i: (0, 0)),
            pl.BlockSpec((MOD_ROWS, d), lambda bi, i: (0, 1)),
            pl.BlockSpec((1, d), lambda bi, i: (0, 0)),
            pl.BlockSpec(w_in.shape, lambda bi, i: (0, 0), pipeline_mode=pl.Buffered(1)),
            pl.BlockSpec((tm, HEAD_DIM), lambda bi, i: (nt - 1 - i, 0)),
            pl.BlockSpec((tm, HEAD_DIM), lambda bi, i: (nt - 1 - i, 0)),
            pl.BlockSpec((3, CONV_CH), lambda bi, i: (0, 0)),
            pl.BlockSpec((RET_HEADS, tm), lambda bi, i: (0, 0)),
            pl.BlockSpec((1, RET_HEADS, HEAD_DIM, HEAD_DIM), lambda bi, i: (bi, 0, 0, 0)),
        ],
        out_specs=[
            seq_spec, keys_t_spec, seq_spec, seq_spec, seq_spec,
            pl.BlockSpec((1, 1, RET_HEADS, HEAD_DIM, HEAD_DIM),
                         lambda bi, i: (bi, nt - 1 - i, 0, 0, 0)),
        ],
        out_shape=[seq, keys_t, seq, seq, seq,
                   jax.ShapeDtypeStruct((b, nt, RET_HEADS, HEAD_DIM, HEAD_DIM), F32)],
        scratch_shapes=[
            pltpu.VMEM((RET_HEADS, HEAD_DIM, HEAD_DIM), F32),
            pltpu.VMEM((RET_HEADS, tm), F32),
        ],
        compiler_params=pltpu.CompilerParams(
            dimension_semantics=("arbitrary", "arbitrary"),
            vmem_limit_bytes=VMEM_LIMIT),
        name="inproj",
    )(x, mod, mod, norm_g, w_in, cos_t, sin_t, conv_w, dec_b_rows, sb0)


def _mix_kernel(q_ref, kt_ref, v_ref, gate_ref, conv_ref, x_ref, sbound_ref, sf0_ref,
                wo_r_ref, wo_c_ref, g1_ref, decf_ref, decb_ref,
                o_ref,
                sf_scr, dmat_scr, qdec_scr, kdec_scr, cdec_scr, ret_scr):
    b = pl.program_id(0)
    i = pl.program_id(1)
    tm = ret_scr.shape[0]
    c = dmat_scr.shape[1]
    nc = tm // c

    @pl.when(i == 0)
    def _():
        sf_scr[...] = sf0_ref[0]

    @pl.when((b == 0) & (i == 0))
    def _():
        lgf = _log_sigmoid(decf_ref[...])
        lgb = _log_sigmoid(decb_ref[...])
        ri = lax.broadcasted_iota(jnp.int32, (c, c), 0)
        ci = lax.broadcasted_iota(jnp.int32, (c, c), 1)
        rel = (ri - ci).astype(F32)
        t = lax.broadcasted_iota(jnp.int32, (c, HEAD_DIM), 0).astype(F32)
        tl = lax.broadcasted_iota(jnp.int32, (1, c), 1).astype(F32)
        for hd in range(RET_HEADS):
            f = lgf[hd:hd + 1, :]
            g = lgb[hd:hd + 1, :]
            dmat_scr[hd] = jnp.where(rel >= 0, jnp.exp(f * jnp.maximum(rel, 0.0)),
                                     jnp.exp(g * jnp.maximum(-rel, 0.0)))
            fh = f[:, :HEAD_DIM]
            gh = g[:, :HEAD_DIM]
            qdec_scr[hd, :, :HEAD_DIM] = jnp.exp(fh * (t + 1.0))
            qdec_scr[hd, :, HEAD_DIM:] = jnp.exp(gh * (float(c) - t))
            kdec_scr[hd, 0:1, :] = jnp.exp(f * (float(c) - 1.0 - tl))
            kdec_scr[hd, 1:2, :] = jnp.exp(g * tl)
            cdec_scr[hd, :, :HEAD_DIM] = jnp.broadcast_to(jnp.exp(fh * float(c)), (8, HEAD_DIM))
            cdec_scr[hd, :, HEAD_DIM:] = jnp.broadcast_to(jnp.exp(gh * float(c)), (8, HEAD_DIM))

    for hd in range(RET_HEADS):
        sl = slice(hd * HEAD_DIM, (hd + 1) * HEAD_DIM)
        cf = cdec_scr[hd, 0:1, :HEAD_DIM]
        cb = cdec_scr[hd, 0:1, HEAD_DIM:]
        kts, vs, kfs, kbs = [], [], [], []
        for ch in range(nc):
            rows = slice(ch * c, (ch + 1) * c)
            kt = kt_ref[0, hd, :, rows]
            kts.append(kt)
            vs.append(v_ref[0, rows, sl])
            ktf = kt.astype(F32)
            kfs.append((ktf * kdec_scr[hd, 0:1, :]).astype(BF16))
            kbs.append((ktf * kdec_scr[hd, 1:2, :]).astype(BF16))
        sb = [None] * nc
        sb[nc - 1] = sbound_ref[0, 0, hd]
        for ch in range(nc - 1, 0, -1):
            sb[ch - 1] = cb * sb[ch] + _dot(kbs[ch], vs[ch])
        sf = sf_scr[hd]
        for ch in range(nc):
            rows = slice(ch * c, (ch + 1) * c)
            qh = q_ref[0, rows, sl]
            p = (_dot(qh, kts[ch]) * dmat_scr[hd]).astype(BF16)
            inner = _dot(p, vs[ch])
            q2 = (jnp.concatenate([qh, qh], axis=1).astype(F32) * qdec_scr[hd]).astype(BF16)
            s2 = jnp.concatenate([sf, sb[ch]], axis=0).astype(BF16)
            o = inner + _dot(q2, s2)
            ms = jnp.mean(o * o, axis=-1, keepdims=True)
            o = o * lax.rsqrt(ms + EPS) * gate_ref[0, rows, sl].astype(F32)
            ret_scr[rows, sl] = o.astype(BF16)
            sf = cf * sf + _dot(kfs[ch], vs[ch])
        sf_scr[hd] = sf

    mix = _dot(ret_scr[...], wo_r_ref[...]) + _dot(conv_ref[0], wo_c_ref[...])
    o_ref[0] = x_ref[0] + g1_ref[pl.ds(b, 1), :] * mix


def _mix_call(q, kt, v, gate, conv, x, sbound, sf0, w_out, mod, dec_f_rows, dec_b_rows):
    b, n, d = x.shape
    tm = MIX_TILE
    nt = n // tm
    bound_per_tile = sbound.shape[1] // nt
    r = RET_WIDTH
    c = RET_CHUNK

    def tok(bi, i):
        return (bi, i, 0)

    seq_spec = pl.BlockSpec((1, tm, r), tok)
    return pl.pallas_call(
        _mix_kernel,
        grid=(b, nt),
        in_specs=[
            seq_spec,
            pl.BlockSpec((1, RET_HEADS, HEAD_DIM, tm), lambda bi, i: (bi, 0, 0, i)),
            seq_spec, seq_spec, seq_spec,
            pl.BlockSpec((1, tm, d), tok),
            pl.BlockSpec((1, 1, RET_HEADS, HEAD_DIM, HEAD_DIM),
                         lambda bi, i: (bi, (i + 1) * bound_per_tile - 1, 0, 0, 0)),
            pl.BlockSpec((1, RET_HEADS, HEAD_DIM, HEAD_DIM), lambda bi, i: (bi, 0, 0, 0)),
            pl.BlockSpec((r, d), lambda bi, i: (0, 0)),
            pl.BlockSpec((CONV_CH, d), lambda bi, i: (1, 0)),
            pl.BlockSpec((MOD_ROWS, d), lambda bi, i: (0, 2)),
            pl.BlockSpec((RET_HEADS, c), lambda bi, i: (0, 0)),
            pl.BlockSpec((RET_HEADS, c), lambda bi, i: (0, 0)),
        ],
        out_specs=pl.BlockSpec((1, tm, d), tok),
        out_shape=jax.ShapeDtypeStruct((b, n, d), F32),
        scratch_shapes=[
            pltpu.VMEM((RET_HEADS, HEAD_DIM, HEAD_DIM), F32),
            pltpu.VMEM((RET_HEADS, c, c), F32),
            pltpu.VMEM((RET_HEADS, c, 2 * HEAD_DIM), F32),
            pltpu.VMEM((RET_HEADS, 8, c), F32),
            pltpu.VMEM((RET_HEADS, 8, 2 * HEAD_DIM), F32),
            pltpu.VMEM((tm, r), BF16),
        ],
        compiler_params=pltpu.CompilerParams(
            dimension_semantics=("arbitrary", "arbitrary"), vmem_limit_bytes=VMEM_LIMIT),
        name="mix",
    )(q, kt, v, gate, conv, x, sbound, sf0, w_out, w_out, mod, dec_f_rows, dec_b_rows)


def _ffn_kernel(x_ref, sh_ref, sc_ref, gt_ref, g2_ref, gf_ref,
                wa_ref, wb_ref, cw_ref, cb_ref, wd_ref,
                o_ref,
                h_scr):
    b = pl.program_id(0)
    f = pl.program_id(2)
    nf = pl.num_programs(2)
    tm, d = h_scr.shape
    tf = wa_ref.shape[1]
    blk = o_ref.shape[1:]
    step = blk[1]

    def body(first, last):
        if first:
            h = _norm_mod(x_ref[0].reshape(tm, d), g2_ref[...], sh_ref[pl.ds(b, 1), :],
                          sc_ref[pl.ds(b, 1), :]).astype(BF16)
            h_scr[...] = h
        else:
            h = h_scr[...]
        a = _dot(h, wa_ref[...])
        zero = jnp.zeros((step, tf), F32)
        a_up = jnp.concatenate([zero, a[:tm - step]], axis=0)
        a_dn = jnp.concatenate([a[step:], zero], axis=0)
        a = cw_ref[0:1, :] * a_up + cw_ref[1:2, :] * a + cw_ref[2:3, :] * a_dn + cb_ref[...]
        act = (a * _sigmoid(a) * _dot(h, wb_ref[...])).astype(BF16)
        y = _dot(act, wd_ref[...])
        acc = y if first else o_ref[0].reshape(tm, d) + y
        if last:
            z = x_ref[0].reshape(tm, d) + gt_ref[pl.ds(b, 1), :] * acc
            ms = jnp.mean(z * z, axis=-1, keepdims=True)
            acc = z * lax.rsqrt(ms + EPS) * gf_ref[...]
        o_ref[0] = acc.reshape(blk)

    @pl.when(f == 0)
    def _():
        body(True, False)

    @pl.when((f > 0) & (f < nf - 1))
    def _():
        body(False, False)

    @pl.when(f == nf - 1)
    def _():
        body(False, True)


def _ffn_call(x, mod, norm_g, final_g, w_up, ffn_conv_w, ffn_conv_b, w_down):
    b, n, d = x.shape
    d_ff = w_down.shape[0]
    rows = n // GRID_W
    cols = FFN_COLS
    tf = FF_TILE
    nf = d_ff // tf
    assert nf >= 2 and w_up.shape == (2 * nf, d, tf)
    xg = x.reshape(b, rows, GRID_W, d)
    tile = pl.BlockSpec((1, rows, cols, d), lambda bi, i, f: (bi, 0, i, 0))

    out = pl.pallas_call(
        _ffn_kernel,
        grid=(b, GRID_W // cols, nf),
        in_specs=[
            tile,
            pl.BlockSpec((MOD_ROWS, d), lambda bi, i, f: (0, 3)),
            pl.BlockSpec((MOD_ROWS, d), lambda bi, i, f: (0, 4)),
            pl.BlockSpec((MOD_ROWS, d), lambda bi, i, f: (0, 5)),
            pl.BlockSpec((1, d), lambda bi, i, f: (0, 0)),
            pl.BlockSpec((1, d), lambda bi, i, f: (0, 0)),
            pl.BlockSpec((None, d, tf), lambda bi, i, f: (f, 0, 0)),
            pl.BlockSpec((None, d, tf), lambda bi, i, f: (nf + f, 0, 0)),
            pl.BlockSpec((3, tf), lambda bi, i, f: (0, f)),
            pl.BlockSpec((1, tf), lambda bi, i, f: (0, f)),
            pl.BlockSpec((tf, d), lambda bi, i, f: (f, 0)),
        ],
        out_specs=tile,
        out_shape=jax.ShapeDtypeStruct((b, rows, GRID_W, d), F32),
        scratch_shapes=[
            pltpu.VMEM((rows * cols, d), BF16),
        ],
        compiler_params=pltpu.CompilerParams(
            dimension_semantics=("arbitrary", "arbitrary", "arbitrary"),
            vmem_limit_bytes=VMEM_LIMIT),
        name="ffn",
    )(xg, mod, mod, mod, norm_g, final_g, w_up, w_up, ffn_conv_w, ffn_conv_b, w_down)
    return out.reshape(b, n, d)


def _rope_tables(n):
    pos = np.arange(n, dtype=np.int32)
    row = (pos // GRID_W).astype(np.float32)
    col = (pos % GRID_W).astype(np.float32)
    dh = HEAD_DIM // 2
    freqs = np.float32(ROPE_THETA) ** (-np.arange(0, dh, 2, dtype=np.float32) / np.float32(dh))
    parts_c, parts_s = [], []
    for p in (row, col):
        ang = (p[:, None] * freqs[None, :]).astype(np.float32)
        cos = np.cos(ang).astype(np.float32)
        sin = np.sin(ang).astype(np.float32)
        parts_c += [cos, cos]
        parts_s += [-sin, sin]
    return (jnp.asarray(np.concatenate(parts_c, axis=-1)),
            jnp.asarray(np.concatenate(parts_s, axis=-1)))


def kernel(x, c, ctx, c_ctx, w_mod, b_mod, norm1_g, w_in, ret_decay_fwd, ret_decay_bwd,
           conv_w, w_out, norm2_g, w_up, ffn_conv_w, ffn_conv_b, w_down, final_g):
    b, n, d = x.shape
    depth = w_mod.shape[0]
    assert depth == 1 and n % MIX_TILE == 0 and n % GRID_W == 0
    assert MIX_TILE % INPROJ_TILE == 0 and MIX_TILE % RET_CHUNK == 0 and INPROJ_TILE == RET_CHUNK
    cos_t, sin_t = _rope_tables(n)
    craw = jnp.concatenate(
        [c, c_ctx[None, :], jnp.zeros((MOD_ROWS - b - 1, d), F32)], axis=0)

    layer = 0
    mod = _mod_call(craw, w_mod[layer], b_mod[layer][None, :])
    w_in_b = w_in[layer].astype(BF16)
    w_out_b = w_out[layer].astype(BF16)
    w_up_b = w_up[layer].astype(BF16).reshape(d, -1, FF_TILE).transpose(1, 0, 2)
    w_down_b = w_down[layer].astype(BF16)
    dec_f = jnp.repeat(ret_decay_fwd[layer], HEAD_DIM)[None, :]
    dec_b = jnp.repeat(ret_decay_bwd[layer], HEAD_DIM)[None, :]
    dec_f_rows = jnp.broadcast_to(ret_decay_fwd[layer][:, None], (RET_HEADS, RET_CHUNK))
    dec_b_rows = jnp.broadcast_to(ret_decay_bwd[layer][:, None], (RET_HEADS, RET_CHUNK))
    g1 = norm1_g[layer][None, :]

    sf0, sb0 = _ctx_call(ctx, mod, g1, w_in_b, dec_f, dec_b)
    q, kt, v, gate, conv, sbound = _inproj_call(
        x, mod, g1, w_in_b, cos_t, sin_t, conv_w[layer], dec_b_rows, sb0)
    x_mid = _mix_call(q, kt, v, gate, conv, x, sbound, sf0, w_out_b, mod,
                      dec_f_rows, dec_b_rows)
    return _ffn_call(x_mid, mod, norm2_g[layer][None, :], final_g[None, :], w_up_b,
                     ffn_conv_w[layer], ffn_conv_b[layer][None, :], w_down_b)
```

```python
import functools

import jax
import jax.numpy as jnp
import numpy as np
from jax import lax
from jax.experimental import pallas as pl
from jax.experimental.pallas import tpu as pltpu

F32 = jnp.float32
BF16 = jnp.bfloat16

GRID_W = 64
RET_HEADS = 8
HEAD_DIM = 128
RET_WIDTH = RET_HEADS * HEAD_DIM
CONV_CH = 1024
N_MOD = 6
EPS = 1e-6
ROPE_THETA = 10000.0

MOD_ROWS = 8
INPROJ_TILE = 256
MIX_TILE = 512
FFN_COLS = 8
RET_CHUNK = 256
FF_TILE = 512
MOD_COL_TILE = 1024
VMEM_LIMIT = 60 * 1024 * 1024


def _sigmoid(x):
    return 1.0 / (1.0 + jnp.exp(-x))


def _log_sigmoid(x):
    return jnp.minimum(x, 0.0) - jnp.log(1.0 + jnp.exp(-jnp.abs(x)))


def _norm_mod(x, gain, shift, scale):
    ms = jnp.mean(x * x, axis=-1, keepdims=True)
    y = x * lax.rsqrt(ms + EPS) * gain
    return y * (1.0 + scale) + shift


def _dot(a, b):
    return jnp.dot(a, b, preferred_element_type=F32)


def _dot_t0(a, b):
    return lax.dot_general(a, b, (((0,), (0,)), ((), ())), preferred_element_type=F32)


def _mod_kernel(c_ref, w_ref, b_ref, o_ref):
    a = c_ref[...]
    a = (a * _sigmoid(a)).astype(BF16)
    o_ref[...] = _dot(a, w_ref[...].astype(BF16)) + b_ref[...]


def _mod_call(craw, w_mod, b_mod):
    d, n = w_mod.shape
    tn = MOD_COL_TILE
    return pl.pallas_call(
        _mod_kernel,
        grid=(n // tn,),
        in_specs=[
            pl.BlockSpec((MOD_ROWS, d), lambda j: (0, 0)),
            pl.BlockSpec((d, tn), lambda j: (0, j)),
            pl.BlockSpec((1, tn), lambda j: (0, j)),
        ],
        out_specs=pl.BlockSpec((MOD_ROWS, tn), lambda j: (0, j)),
        out_shape=jax.ShapeDtypeStruct((MOD_ROWS, n), F32),
        compiler_params=pltpu.CompilerParams(
            dimension_semantics=("arbitrary",), vmem_limit_bytes=VMEM_LIMIT),
        name="mod",
    )(craw, w_mod, b_mod)


def _ctx_kernel(ctx_row, x_ref, sh_ref, sc_ref, g_ref, wk_ref, wv_ref, decf_ref, decb_ref,
                sf_ref, sb_ref):
    x = x_ref[0]
    length = x.shape[0]
    h = _norm_mod(x, g_ref[...], sh_ref[ctx_row:ctx_row + 1, :],
                  sc_ref[ctx_row:ctx_row + 1, :]).astype(BF16)
    k = _dot(h, wk_ref[...]) * (HEAD_DIM ** -0.5)
    v = _dot(h, wv_ref[...]).astype(BF16)
    t = lax.broadcasted_iota(jnp.int32, k.shape, 0).astype(F32)
    wf = jnp.exp(_log_sigmoid(decf_ref[...]) * (length - 1.0 - t))
    wb = jnp.exp(_log_sigmoid(decb_ref[...]) * t)
    kf = (k * wf).astype(BF16)
    kb = (k * wb).astype(BF16)
    for hd in range(RET_HEADS):
        sl = slice(hd * HEAD_DIM, (hd + 1) * HEAD_DIM)
        sf_ref[0, hd] = _dot_t0(kf[:, sl], v[:, sl])
        sb_ref[0, hd] = _dot_t0(kb[:, sl], v[:, sl])


def _ctx_call(ctx, mod, norm_g, w_in, dec_f, dec_b):
    b, length, d = ctx.shape
    r = RET_WIDTH
    state = jax.ShapeDtypeStruct((b, RET_HEADS, HEAD_DIM, HEAD_DIM), F32)
    state_spec = pl.BlockSpec((1, RET_HEADS, HEAD_DIM, HEAD_DIM), lambda i: (i, 0, 0, 0))
    return pl.pallas_call(
        functools.partial(_ctx_kernel, b),
        grid=(b,),
        in_specs=[
            pl.BlockSpec((1, length, d), lambda i: (i, 0, 0)),
            pl.BlockSpec((MOD_ROWS, d), lambda i: (0, 0)),
            pl.BlockSpec((MOD_ROWS, d), lambda i: (0, 1)),
            pl.BlockSpec((1, d), lambda i: (0, 0)),
            pl.BlockSpec((d, r), lambda i: (0, 1)),
            pl.BlockSpec((d, r), lambda i: (0, 2)),
            pl.BlockSpec((1, r), lambda i: (0, 0)),
            pl.BlockSpec((1, r), lambda i: (0, 0)),
        ],
        out_specs=[state_spec, state_spec],
        out_shape=[state, state],
        compiler_params=pltpu.CompilerParams(
            dimension_semantics=("arbitrary",), vmem_limit_bytes=VMEM_LIMIT),
        name="ctx",
    )(ctx, mod, mod, norm_g, w_in, w_in, dec_f, dec_b)


def _rope(acc, cos, sin, scale):
    lane = lax.broadcasted_iota(jnp.int32, (1, HEAD_DIM), 1)
    lo = (lane % 64) < 32
    outs = []
    for hd in range(RET_HEADS):
        xh = acc[:, hd * HEAD_DIM:(hd + 1) * HEAD_DIM]
        partner = jnp.where(lo, pltpu.roll(xh, 96, 1), pltpu.roll(xh, 32, 1))
        y = xh * cos + partner * sin
        outs.append(y if scale == 1.0 else y * scale)
    return outs


def _inproj_kernel(x_ref, sh_ref, sc_ref, g_ref, w_ref, cos_ref, sin_ref, cw_ref, decb_ref,
                   sb0_ref,
                   q_ref, kt_ref, v_ref, gate_ref, conv_ref, sbound_ref,
                   s_scr, dec_scr):
    b = pl.program_id(0)
    i = pl.program_id(1)
    tm = dec_scr.shape[1]
    r = RET_WIDTH

    @pl.when(i == 0)
    def _():
        s_scr[...] = sb0_ref[0]
        t = lax.broadcasted_iota(jnp.int32, dec_scr.shape, 1).astype(F32)
        dec_scr[...] = jnp.exp(_log_sigmoid(decb_ref[...]) * t)

    h = _norm_mod(x_ref[0], g_ref[...], sh_ref[pl.ds(b, 1), :], sc_ref[pl.ds(b, 1), :]).astype(BF16)

    def proj(group):
        return _dot(h, w_ref[:, group * r:(group + 1) * r])

    cos = cos_ref[...]
    sin = sin_ref[...]
    for hd, qh in enumerate(_rope(proj(0), cos, sin, 1.0)):
        q_ref[0, :, hd * HEAD_DIM:(hd + 1) * HEAD_DIM] = qh.astype(q_ref.dtype)
    ks = _rope(proj(1), cos, sin, HEAD_DIM ** -0.5)
    vb = proj(2).astype(BF16)
    v_ref[0] = vb
    tile_dec = jnp.exp(_log_sigmoid(decb_ref[:, :HEAD_DIM]) * float(tm))
    for hd in range(RET_HEADS):
        kt = ks[hd].T
        kt_ref[0, hd] = kt.astype(kt_ref.dtype)
        ktd = (kt * dec_scr[hd:hd + 1, :]).astype(BF16)
        s_old = s_scr[hd]
        sbound_ref[0, 0, hd] = s_old
        s_scr[hd] = tile_dec[hd:hd + 1, :] * s_old + _dot(ktd, vb[:, hd * HEAD_DIM:(hd + 1) * HEAD_DIM])

    g = proj(3)
    gate_ref[0] = (g * _sigmoid(g)).astype(gate_ref.dtype)

    u = proj(5) * proj(6)
    col = lax.broadcasted_iota(jnp.int32, (tm, 1), 0) % GRID_W
    up = jnp.where(col == 0, 0.0, pltpu.roll(u, 1, 0))
    un = jnp.where(col == GRID_W - 1, 0.0, pltpu.roll(u, tm - 1, 0))
    y = proj(4) * (cw_ref[0:1, :] * up + cw_ref[1:2, :] * u + cw_ref[2:3, :] * un)
    conv_ref[0] = y.astype(conv_ref.dtype)


def _inproj_call(x, mod, norm_g, w_in, cos_t, sin_t, conv_w, dec_b_rows, sb0):
    b, n, d = x.shape
    tm = INPROJ_TILE
    nt = n // tm
    r = RET_WIDTH
    assert dec_b_rows.shape == (RET_HEADS, tm)

    def tok(bi, i):
        return (bi, nt - 1 - i, 0)

    seq = jax.ShapeDtypeStruct((b, n, r), BF16)
    seq_spec = pl.BlockSpec((1, tm, r), tok)
    keys_t = jax.ShapeDtypeStruct((b, RET_HEADS, HEAD_DIM, n), BF16)
    keys_t_spec = pl.BlockSpec((1, RET_HEADS, HEAD_DIM, tm), lambda bi, i: (bi, 0, 0, nt - 1 - i))
    return pl.pallas_call(
        _inproj_kernel,
        grid=(b, nt),
        in_specs=[
            pl.BlockSpec((1, tm, d), tok),
            pl.BlockSpec((MOD_ROWS, d), lambda bi, i: (0, 0)),
            pl.BlockSpec((MOD_ROWS, d), lambda bi, i: (0, 1)),
            pl.BlockSpec((1, d), lambda bi, i: (0, 0)),
            pl.BlockSpec(w_in.shape, lambda bi, i: (0, 0), pipeline_mode=pl.Buffered(1)),
            pl.BlockSpec((tm, HEAD_DIM), lambda bi, i: (nt - 1 - i, 0)),
            pl.BlockSpec((tm, HEAD_DIM), lambda bi, i: (nt - 1 - i, 0)),
            pl.BlockSpec((3, CONV_CH), lambda bi, i: (0, 0)),
            pl.BlockSpec((RET_HEADS, tm), lambda bi, i: (0, 0)),
            pl.BlockSpec((1, RET_HEADS, HEAD_DIM, HEAD_DIM), lambda bi, i: (bi, 0, 0, 0)),
        ],
        out_specs=[
            seq_spec, keys_t_spec, seq_spec, seq_spec, seq_spec,
            pl.BlockSpec((1, 1, RET_HEADS, HEAD_DIM, HEAD_DIM),
                         lambda bi, i: (bi, nt - 1 - i, 0, 0, 0)),
        ],
        out_shape=[seq, keys_t, seq, seq, seq,
                   jax.ShapeDtypeStruct((b, nt, RET_HEADS, HEAD_DIM, HEAD_DIM), F32)],
        scratch_shapes=[
            pltpu.VMEM((RET_HEADS, HEAD_DIM, HEAD_DIM), F32),
            pltpu.VMEM((RET_HEADS, tm), F32),
        ],
        compiler_params=pltpu.CompilerParams(
            dimension_semantics=("arbitrary", "arbitrary"),
            vmem_limit_bytes=VMEM_LIMIT),
        name="inproj",
    )(x, mod, mod, norm_g, w_in, cos_t, sin_t, conv_w, dec_b_rows, sb0)


def _mix_kernel(q_ref, kt_ref, v_ref, gate_ref, conv_ref, x_ref, sbound_ref, sf0_ref,
                wo_r_ref, wo_c_ref, g1_ref, decf_ref, decb_ref,
                o_ref,
                sf_scr, dmat_scr, qdec_scr, kdec_scr, cdec_scr, ret_scr):
    b = pl.program_id(0)
    i = pl.program_id(1)
    tm = ret_scr.shape[0]
    c = dmat_scr.shape[1]
    nc = tm // c

    @pl.when(i == 0)
    def _():
        sf_scr[...] = sf0_ref[0]

    @pl.when((b == 0) & (i == 0))
    def _():
        lgf = _log_sigmoid(decf_ref[...])
        lgb = _log_sigmoid(decb_ref[...])
        ri = lax.broadcasted_iota(jnp.int32, (c, c), 0)
        ci = lax.broadcasted_iota(jnp.int32, (c, c), 1)
        rel = (ri - ci).astype(F32)
        t = lax.broadcasted_iota(jnp.int32, (c, HEAD_DIM), 0).astype(F32)
        tl = lax.broadcasted_iota(jnp.int32, (1, c), 1).astype(F32)
        for hd in range(RET_HEADS):
            f = lgf[hd:hd + 1, :]
            g = lgb[hd:hd + 1, :]
            dmat_scr[hd] = jnp.where(rel >= 0, jnp.exp(f * jnp.maximum(rel, 0.0)),
                                     jnp.exp(g * jnp.maximum(-rel, 0.0)))
            fh = f[:, :HEAD_DIM]
            gh = g[:, :HEAD_DIM]
            qdec_scr[hd, :, :HEAD_DIM] = jnp.exp(fh * (t + 1.0))
            qdec_scr[hd, :, HEAD_DIM:] = jnp.exp(gh * (float(c) - t))
            kdec_scr[hd, 0:1, :] = jnp.exp(f * (float(c) - 1.0 - tl))
            kdec_scr[hd, 1:2, :] = jnp.exp(g * tl)
            cdec_scr[hd, :, :HEAD_DIM] = jnp.broadcast_to(jnp.exp(fh * float(c)), (8, HEAD_DIM))
            cdec_scr[hd, :, HEAD_DIM:] = jnp.broadcast_to(jnp.exp(gh * float(c)), (8, HEAD_DIM))

    for hd in range(RET_HEADS):
        sl = slice(hd * HEAD_DIM, (hd + 1) * HEAD_DIM)
        cf = cdec_scr[hd, 0:1, :HEAD_DIM]
        cb = cdec_scr[hd, 0:1, HEAD_DIM:]
        kts, vs, kfs, kbs = [], [], [], []
        for ch in range(nc):
            rows = slice(ch * c, (ch + 1) * c)
            kt = kt_ref[0, hd, :, rows]
            kts.append(kt)
            vs.append(v_ref[0, rows, sl])
            ktf = kt.astype(F32)
            kfs.append((ktf * kdec_scr[hd, 0:1, :]).astype(BF16))
            kbs.append((ktf * kdec_scr[hd, 1:2, :]).astype(BF16))
        sb = [None] * nc
        sb[nc - 1] = sbound_ref[0, 0, hd]
        for ch in range(nc - 1, 0, -1):
            sb[ch - 1] = cb * sb[ch] + _dot(kbs[ch], vs[ch])
        sf = sf_scr[hd]
        for ch in range(nc):
            rows = slice(ch * c, (ch + 1) * c)
            qh = q_ref[0, rows, sl]
            p = (_dot(qh, kts[ch]) * dmat_scr[hd]).astype(BF16)
            inner = _dot(p, vs[ch])
            q2 = (jnp.concatenate([qh, qh], axis=1).astype(F32) * qdec_scr[hd]).astype(BF16)
            s2 = jnp.concatenate([sf, sb[ch]], axis=0).astype(BF16)
            o = inner + _dot(q2, s2)
            ms = jnp.mean(o * o, axis=-1, keepdims=True)
            o = o * lax.rsqrt(ms + EPS) * gate_ref[0, rows, sl].astype(F32)
            ret_scr[rows, sl] = o.astype(BF16)
            sf = cf * sf + _dot(kfs[ch], vs[ch])
        sf_scr[hd] = sf

    mix = _dot(ret_scr[...], wo_r_ref[...]) + _dot(conv_ref[0], wo_c_ref[...])
    o_ref[0] = x_ref[0] + g1_ref[pl.ds(b, 1), :] * mix


def _mix_call(q, kt, v, gate, conv, x, sbound, sf0, w_out, mod, dec_f_rows, dec_b_rows):
    b, n, d = x.shape
    tm = MIX_TILE
    nt = n // tm
    bound_per_tile = sbound.shape[1] // nt
    r = RET_WIDTH
    c = RET_CHUNK

    def tok(bi, i):
        return (bi, i, 0)

    seq_spec = pl.BlockSpec((1, tm, r), tok)
    return pl.pallas_call(
        _mix_kernel,
        grid=(b, nt),
        in_specs=[
            seq_spec,
            pl.BlockSpec((1, RET_HEADS, HEAD_DIM, tm), lambda bi, i: (bi, 0, 0, i)),
            seq_spec, seq_spec, seq_spec,
            pl.BlockSpec((1, tm, d), tok),
            pl.BlockSpec((1, 1, RET_HEADS, HEAD_DIM, HEAD_DIM),
                         lambda bi, i: (bi, (i + 1) * bound_per_tile - 1, 0, 0, 0)),
            pl.BlockSpec((1, RET_HEADS, HEAD_DIM, HEAD_DIM), lambda bi, i: (bi, 0, 0, 0)),
            pl.BlockSpec((r, d), lambda bi, i: (0, 0)),
            pl.BlockSpec((CONV_CH, d), lambda bi, i: (1, 0)),
            pl.BlockSpec((MOD_ROWS, d), lambda bi, i: (0, 2)),
            pl.BlockSpec((RET_HEADS, c), lambda bi, i: (0, 0)),
            pl.BlockSpec((RET_HEADS, c), lambda bi, i: (0, 0)),
        ],
        out_specs=pl.BlockSpec((1, tm, d), tok),
        out_shape=jax.ShapeDtypeStruct((b, n, d), F32),
        scratch_shapes=[
            pltpu.VMEM((RET_HEADS, HEAD_DIM, HEAD_DIM), F32),
            pltpu.VMEM((RET_HEADS, c, c), F32),
            pltpu.VMEM((RET_HEADS, c, 2 * HEAD_DIM), F32),
            pltpu.VMEM((RET_HEADS, 8, c), F32),
            pltpu.VMEM((RET_HEADS, 8, 2 * HEAD_DIM), F32),
            pltpu.VMEM((tm, r), BF16),
        ],
        compiler_params=pltpu.CompilerParams(
            dimension_semantics=("arbitrary", "arbitrary"), vmem_limit_bytes=VMEM_LIMIT),
        name="mix",
    )(q, kt, v, gate, conv, x, sbound, sf0, w_out, w_out, mod, dec_f_rows, dec_b_rows)


def _ffn_kernel(x_ref, sh_ref, sc_ref, gt_ref, g2_ref, gf_ref,
                wa_ref, wb_ref, cw_ref, cb_ref, wd_ref,
                o_ref,
                h_scr):
    b = pl.program_id(0)
    f = pl.program_id(2)
    nf = pl.num_programs(2)
    tm, d = h_scr.shape
    tf = wa_ref.shape[1]
    blk = o_ref.shape[1:]
    step = blk[1]

    def body(first, last):
        if first:
            h = _norm_mod(x_ref[0].reshape(tm, d), g2_ref[...], sh_ref[pl.ds(b, 1), :],
                          sc_ref[pl.ds(b, 1), :]).astype(BF16)
            h_scr[...] = h
        else:
            h = h_scr[...]
        a = _dot(h, wa_ref[...])
        zero = jnp.zeros((step, tf), F32)
        a_up = jnp.concatenate([zero, a[:tm - step]], axis=0)
        a_dn = jnp.concatenate([a[step:], zero], axis=0)
        a = cw_ref[0:1, :] * a_up + cw_ref[1:2, :] * a + cw_ref[2:3, :] * a_dn + cb_ref[...]
        act = (a * _sigmoid(a) * _dot(h, wb_ref[...])).astype(BF16)
        y = _dot(act, wd_ref[...])
        acc = y if first else o_ref[0].reshape(tm, d) + y
        if last:
            z = x_ref[0].reshape(tm, d) + gt_ref[pl.ds(b, 1), :] * acc
            ms = jnp.mean(z * z, axis=-1, keepdims=True)
            acc = z * lax.rsqrt(ms + EPS) * gf_ref[...]
        o_ref[0] = acc.reshape(blk)

    @pl.when(f == 0)
    def _():
        body(True, False)

    @pl.when((f > 0) & (f < nf - 1))
    def _():
        body(False, False)

    @pl.when(f == nf - 1)
    def _():
        body(False, True)


def _ffn_call(x, mod, norm_g, final_g, w_up, ffn_conv_w, ffn_conv_b, w_down):
    b, n, d = x.shape
    d_ff = w_down.shape[0]
    rows = n // GRID_W
    cols = FFN_COLS
    tf = FF_TILE
    nf = d_ff // tf
    assert nf >= 2
    xg = x.reshape(b, rows, GRID_W, d)
    tile = pl.BlockSpec((1, rows, cols, d), lambda bi, i, f: (bi, 0, i, 0))

    out = pl.pallas_call(
        _ffn_kernel,
        grid=(b, GRID_W // cols, nf),
        in_specs=[
            tile,
            pl.BlockSpec((MOD_ROWS, d), lambda bi, i, f: (0, 3)),
            pl.BlockSpec((MOD_ROWS, d), lambda bi, i, f: (0, 4)),
            pl.BlockSpec((MOD_ROWS, d), lambda bi, i, f: (0, 5)),
            pl.BlockSpec((1, d), lambda bi, i, f: (0, 0)),
            pl.BlockSpec((1, d), lambda bi, i, f: (0, 0)),
            pl.BlockSpec((d, tf), lambda bi, i, f: (0, f)),
            pl.BlockSpec((d, tf), lambda bi, i, f: (0, nf + f)),
            pl.BlockSpec((3, tf), lambda bi, i, f: (0, f)),
            pl.BlockSpec((1, tf), lambda bi, i, f: (0, f)),
            pl.BlockSpec((tf, d), lambda bi, i, f: (f, 0)),
        ],
        out_specs=tile,
        out_shape=jax.ShapeDtypeStruct((b, rows, GRID_W, d), F32),
        scratch_shapes=[
            pltpu.VMEM((rows * cols, d), BF16),
        ],
        compiler_params=pltpu.CompilerParams(
            dimension_semantics=("arbitrary", "arbitrary", "arbitrary"),
            vmem_limit_bytes=VMEM_LIMIT),
        name="ffn",
    )(xg, mod, mod, mod, norm_g, final_g, w_up, w_up, ffn_conv_w, ffn_conv_b, w_down)
    return out.reshape(b, n, d)


def _rope_tables(n):
    pos = np.arange(n, dtype=np.int32)
    row = (pos // GRID_W).astype(np.float32)
    col = (pos % GRID_W).astype(np.float32)
    dh = HEAD_DIM // 2
    freqs = np.float32(ROPE_THETA) ** (-np.arange(0, dh, 2, dtype=np.float32) / np.float32(dh))
    parts_c, parts_s = [], []
    for p in (row, col):
        ang = (p[:, None] * freqs[None, :]).astype(np.float32)
        cos = np.cos(ang).astype(np.float32)
        sin = np.sin(ang).astype(np.float32)
        parts_c += [cos, cos]
        parts_s += [-sin, sin]
    return (jnp.asarray(np.concatenate(parts_c, axis=-1)),
            jnp.asarray(np.concatenate(parts_s, axis=-1)))


def kernel(x, c, ctx, c_ctx, w_mod, b_mod, norm1_g, w_in, ret_decay_fwd, ret_decay_bwd,
           conv_w, w_out, norm2_g, w_up, ffn_conv_w, ffn_conv_b, w_down, final_g):
    b, n, d = x.shape
    depth = w_mod.shape[0]
    assert depth == 1 and n % MIX_TILE == 0 and n % GRID_W == 0
    assert MIX_TILE % INPROJ_TILE == 0 and MIX_TILE % RET_CHUNK == 0 and INPROJ_TILE == RET_CHUNK
    cos_t, sin_t = _rope_tables(n)
    craw = jnp.concatenate(
        [c, c_ctx[None, :], jnp.zeros((MOD_ROWS - b - 1, d), F32)], axis=0)

    layer = 0
    mod = _mod_call(craw, w_mod[layer], b_mod[layer][None, :])
    w_in_b = w_in[layer].astype(BF16)
    w_out_b = w_out[layer].astype(BF16)
    w_up_b = w_up[layer].astype(BF16)
    w_down_b = w_down[layer].astype(BF16)
    dec_f = jnp.repeat(ret_decay_fwd[layer], HEAD_DIM)[None, :]
    dec_b = jnp.repeat(ret_decay_bwd[layer], HEAD_DIM)[None, :]
    dec_f_rows = jnp.broadcast_to(ret_decay_fwd[layer][:, None], (RET_HEADS, RET_CHUNK))
    dec_b_rows = jnp.broadcast_to(ret_decay_bwd[layer][:, None], (RET_HEADS, RET_CHUNK))
    g1 = norm1_g[layer][None, :]

    sf0, sb0 = _ctx_call(ctx, mod, g1, w_in_b, dec_f, dec_b)
    q, kt, v, gate, conv, sbound = _inproj_call(
        x, mod, g1, w_in_b, cos_t, sin_t, conv_w[layer], dec_b_rows, sb0)
    x_mid = _mix_call(q, kt, v, gate, conv, x, sbound, sf0, w_out_b, mod,
                      dec_f_rows, dec_b_rows)
    return _ffn_call(x_mid, mod, norm2_g[layer][None, :], final_g[None, :], w_up_b,
                     ffn_conv_w[layer], ffn_conv_b[layer][None, :], w_down_b)
```

```python
import functools

import jax
import jax.numpy as jnp
import numpy as np
from jax import lax
from jax.experimental import pallas as pl
from jax.experimental.pallas import tpu as pltpu

F32 = jnp.float32
BF16 = jnp.bfloat16

GRID_W = 64
RET_HEADS = 8
HEAD_DIM = 128
RET_WIDTH = RET_HEADS * HEAD_DIM
CONV_CH = 1024
N_MOD = 6
EPS = 1e-6
ROPE_THETA = 10000.0

MOD_ROWS = 8
INPROJ_TILE = 512
MIX_TILE = 512
FFN_COLS = 8
RET_CHUNK = 256
FF_TILE = 512
MOD_COL_TILE = 1024
VMEM_LIMIT = 60 * 1024 * 1024


def _sigmoid(x):
    return 1.0 / (1.0 + jnp.exp(-x))


def _log_sigmoid(x):
    return jnp.minimum(x, 0.0) - jnp.log(1.0 + jnp.exp(-jnp.abs(x)))


def _norm_mod(x, gain, shift, scale):
    ms = jnp.mean(x * x, axis=-1, keepdims=True)
    y = x * lax.rsqrt(ms + EPS) * gain
    return y * (1.0 + scale) + shift


def _dot(a, b):
    return jnp.dot(a, b, preferred_element_type=F32)


def _dot_t0(a, b):
    return lax.dot_general(a, b, (((0,), (0,)), ((), ())), preferred_element_type=F32)


def _mod_kernel(c_ref, w_ref, b_ref, o_ref):
    a = c_ref[...]
    a = (a * _sigmoid(a)).astype(BF16)
    o_ref[...] = _dot(a, w_ref[...].astype(BF16)) + b_ref[...]


def _mod_call(craw, w_mod, b_mod):
    d, n = w_mod.shape
    tn = MOD_COL_TILE
    return pl.pallas_call(
        _mod_kernel,
        grid=(n // tn,),
        in_specs=[
            pl.BlockSpec((MOD_ROWS, d), lambda j: (0, 0)),
            pl.BlockSpec((d, tn), lambda j: (0, j)),
            pl.BlockSpec((1, tn), lambda j: (0, j)),
        ],
        out_specs=pl.BlockSpec((MOD_ROWS, tn), lambda j: (0, j)),
        out_shape=jax.ShapeDtypeStruct((MOD_ROWS, n), F32),
        compiler_params=pltpu.CompilerParams(
            dimension_semantics=("arbitrary",), vmem_limit_bytes=VMEM_LIMIT),
        name="mod",
    )(craw, w_mod, b_mod)


def _ctx_kernel(ctx_row, x_ref, sh_ref, sc_ref, g_ref, wk_ref, wv_ref, decf_ref, decb_ref,
                sf_ref, sb_ref):
    x = x_ref[0]
    length = x.shape[0]
    h = _norm_mod(x, g_ref[...], sh_ref[ctx_row:ctx_row + 1, :],
                  sc_ref[ctx_row:ctx_row + 1, :]).astype(BF16)
    k = _dot(h, wk_ref[...]) * (HEAD_DIM ** -0.5)
    v = _dot(h, wv_ref[...]).astype(BF16)
    t = lax.broadcasted_iota(jnp.int32, k.shape, 0).astype(F32)
    wf = jnp.exp(_log_sigmoid(decf_ref[...]) * (length - 1.0 - t))
    wb = jnp.exp(_log_sigmoid(decb_ref[...]) * t)
    kf = (k * wf).astype(BF16)
    kb = (k * wb).astype(BF16)
    for hd in range(RET_HEADS):
        sl = slice(hd * HEAD_DIM, (hd + 1) * HEAD_DIM)
        sf_ref[0, hd] = _dot_t0(kf[:, sl], v[:, sl])
        sb_ref[0, hd] = _dot_t0(kb[:, sl], v[:, sl])


def _ctx_call(ctx, mod, norm_g, w_in, dec_f, dec_b):
    b, length, d = ctx.shape
    r = RET_WIDTH
    state = jax.ShapeDtypeStruct((b, RET_HEADS, HEAD_DIM, HEAD_DIM), F32)
    state_spec = pl.BlockSpec((1, RET_HEADS, HEAD_DIM, HEAD_DIM), lambda i: (i, 0, 0, 0))
    return pl.pallas_call(
        functools.partial(_ctx_kernel, b),
        grid=(b,),
        in_specs=[
            pl.BlockSpec((1, length, d), lambda i: (i, 0, 0)),
            pl.BlockSpec((MOD_ROWS, d), lambda i: (0, 0)),
            pl.BlockSpec((MOD_ROWS, d), lambda i: (0, 1)),
            pl.BlockSpec((1, d), lambda i: (0, 0)),
            pl.BlockSpec((d, r), lambda i: (0, 1)),
            pl.BlockSpec((d, r), lambda i: (0, 2)),
            pl.BlockSpec((1, r), lambda i: (0, 0)),
            pl.BlockSpec((1, r), lambda i: (0, 0)),
        ],
        out_specs=[state_spec, state_spec],
        out_shape=[state, state],
        compiler_params=pltpu.CompilerParams(
            dimension_semantics=("arbitrary",), vmem_limit_bytes=VMEM_LIMIT),
        name="ctx",
    )(ctx, mod, mod, norm_g, w_in, w_in, dec_f, dec_b)


def _rope(acc, cos, sin, scale):
    lane = lax.broadcasted_iota(jnp.int32, (1, HEAD_DIM), 1)
    lo = (lane % 64) < 32
    outs = []
    for hd in range(RET_HEADS):
        xh = acc[:, hd * HEAD_DIM:(hd + 1) * HEAD_DIM]
        partner = jnp.where(lo, pltpu.roll(xh, 96, 1), pltpu.roll(xh, 32, 1))
        y = xh * cos + partner * sin
        outs.append(y if scale == 1.0 else y * scale)
    return outs


def _inproj_kernel(x_ref, sh_ref, sc_ref, g_ref, w_ref, cos_ref, sin_ref, cw_ref, decb_ref,
                   sb0_ref,
                   q_ref, kt_ref, v_ref, gate_ref, conv_ref, sbound_ref,
                   s_scr, dec_scr):
    b = pl.program_id(0)
    i = pl.program_id(1)
    tm = dec_scr.shape[1]
    r = RET_WIDTH

    @pl.when(i == 0)
    def _():
        s_scr[...] = sb0_ref[0]
        t = lax.broadcasted_iota(jnp.int32, dec_scr.shape, 1).astype(F32)
        dec_scr[...] = jnp.exp(_log_sigmoid(decb_ref[...]) * t)

    h = _norm_mod(x_ref[0], g_ref[...], sh_ref[pl.ds(b, 1), :], sc_ref[pl.ds(b, 1), :]).astype(BF16)

    def proj(group):
        return _dot(h, w_ref[:, group * r:(group + 1) * r])

    cos = cos_ref[...]
    sin = sin_ref[...]
    for hd, qh in enumerate(_rope(proj(0), cos, sin, 1.0)):
        q_ref[0, :, hd * HEAD_DIM:(hd + 1) * HEAD_DIM] = qh.astype(q_ref.dtype)
    ks = _rope(proj(1), cos, sin, HEAD_DIM ** -0.5)
    vb = proj(2).astype(BF16)
    v_ref[0] = vb
    tile_dec = jnp.exp(_log_sigmoid(decb_ref[:, :HEAD_DIM]) * float(tm))
    for hd in range(RET_HEADS):
        kt = ks[hd].T
        kt_ref[0, hd] = kt.astype(kt_ref.dtype)
        ktd = (kt * dec_scr[hd:hd + 1, :]).astype(BF16)
        s_old = s_scr[hd]
        sbound_ref[0, 0, hd] = s_old
        s_scr[hd] = tile_dec[hd:hd + 1, :] * s_old + _dot(ktd, vb[:, hd * HEAD_DIM:(hd + 1) * HEAD_DIM])

    g = proj(3)
    gate_ref[0] = (g * _sigmoid(g)).astype(gate_ref.dtype)

    u = proj(5) * proj(6)
    col = lax.broadcasted_iota(jnp.int32, (tm, 1), 0) % GRID_W
    up = jnp.where(col == 0, 0.0, pltpu.roll(u, 1, 0))
    un = jnp.where(col == GRID_W - 1, 0.0, pltpu.roll(u, tm - 1, 0))
    y = proj(4) * (cw_ref[0:1, :] * up + cw_ref[1:2, :] * u + cw_ref[2:3, :] * un)
    conv_ref[0] = y.astype(conv_ref.dtype)


def _inproj_call(x, mod, norm_g, w_in, cos_t, sin_t, conv_w, dec_b_rows, sb0):
    b, n, d = x.shape
    tm = INPROJ_TILE
    nt = n // tm
    r = RET_WIDTH
    assert dec_b_rows.shape == (RET_HEADS, tm)

    def tok(bi, i):
        return (bi, nt - 1 - i, 0)

    seq = jax.ShapeDtypeStruct((b, n, r), BF16)
    seq_spec = pl.BlockSpec((1, tm, r), tok)
    keys_t = jax.ShapeDtypeStruct((b, RET_HEADS, HEAD_DIM, n), BF16)
    keys_t_spec = pl.BlockSpec((1, RET_HEADS, HEAD_DIM, tm), lambda bi, i: (bi, 0, 0, nt - 1 - i))
    return pl.pallas_call(
        _inproj_kernel,
        grid=(b, nt),
        in_specs=[
            pl.BlockSpec((1, tm, d), tok),
            pl.BlockSpec((MOD_ROWS, d), lambda bi, i: (0, 0)),
            pl.BlockSpec((MOD_ROWS, d), lambda bi, i: (0, 1)),
            pl.BlockSpec((1, d), lambda bi, i: (0, 0)),
            pl.BlockSpec(w_in.shape, lambda bi, i: (0, 0), pipeline_mode=pl.Buffered(1)),
            pl.BlockSpec((tm, HEAD_DIM), lambda bi, i: (nt - 1 - i, 0)),
            pl.BlockSpec((tm, HEAD_DIM), lambda bi, i: (nt - 1 - i, 0)),
            pl.BlockSpec((3, CONV_CH), lambda bi, i: (0, 0)),
            pl.BlockSpec((RET_HEADS, tm), lambda bi, i: (0, 0)),
            pl.BlockSpec((1, RET_HEADS, HEAD_DIM, HEAD_DIM), lambda bi, i: (bi, 0, 0, 0)),
        ],
        out_specs=[
            seq_spec, keys_t_spec, seq_spec, seq_spec, seq_spec,
            pl.BlockSpec((1, 1, RET_HEADS, HEAD_DIM, HEAD_DIM),
                         lambda bi, i: (bi, nt - 1 - i, 0, 0, 0)),
        ],
        out_shape=[seq, keys_t, seq, seq, seq,
                   jax.ShapeDtypeStruct((b, nt, RET_HEADS, HEAD_DIM, HEAD_DIM), F32)],
        scratch_shapes=[
            pltpu.VMEM((RET_HEADS, HEAD_DIM, HEAD_DIM), F32),
            pltpu.VMEM((RET_HEADS, tm), F32),
        ],
        compiler_params=pltpu.CompilerParams(
            dimension_semantics=("arbitrary", "arbitrary"),
            vmem_limit_bytes=VMEM_LIMIT),
        name="inproj",
    )(x, mod, mod, norm_g, w_in, cos_t, sin_t, conv_w, dec_b_rows, sb0)


def _mix_kernel(q_ref, kt_ref, v_ref, gate_ref, conv_ref, x_ref, sbound_ref, sf0_ref,
                wo_r_ref, wo_c_ref, g1_ref, decf_ref, decb_ref, wup_ref,
                o_ref, wup_b_ref,
                sf_scr, dmat_scr, qdec_scr, kdec_scr, cdec_scr, ret_scr):
    b = pl.program_id(0)
    i = pl.program_id(1)
    tm = ret_scr.shape[0]
    c = dmat_scr.shape[1]
    nc = tm // c

    wup_b_ref[...] = wup_ref[...].astype(wup_b_ref.dtype)

    @pl.when(i == 0)
    def _():
        sf_scr[...] = sf0_ref[0]

    @pl.when((b == 0) & (i == 0))
    def _():
        lgf = _log_sigmoid(decf_ref[...])
        lgb = _log_sigmoid(decb_ref[...])
        ri = lax.broadcasted_iota(jnp.int32, (c, c), 0)
        ci = lax.broadcasted_iota(jnp.int32, (c, c), 1)
        rel = (ri - ci).astype(F32)
        t = lax.broadcasted_iota(jnp.int32, (c, HEAD_DIM), 0).astype(F32)
        tl = lax.broadcasted_iota(jnp.int32, (1, c), 1).astype(F32)
        for hd in range(RET_HEADS):
            f = lgf[hd:hd + 1, :]
            g = lgb[hd:hd + 1, :]
            dmat_scr[hd] = jnp.where(rel >= 0, jnp.exp(f * jnp.maximum(rel, 0.0)),
                                     jnp.exp(g * jnp.maximum(-rel, 0.0)))
            fh = f[:, :HEAD_DIM]
            gh = g[:, :HEAD_DIM]
            qdec_scr[hd, :, :HEAD_DIM] = jnp.exp(fh * (t + 1.0))
            qdec_scr[hd, :, HEAD_DIM:] = jnp.exp(gh * (float(c) - t))
            kdec_scr[hd, 0:1, :] = jnp.exp(f * (float(c) - 1.0 - tl))
            kdec_scr[hd, 1:2, :] = jnp.exp(g * tl)
            cdec_scr[hd, :, :HEAD_DIM] = jnp.broadcast_to(jnp.exp(fh * float(c)), (8, HEAD_DIM))
            cdec_scr[hd, :, HEAD_DIM:] = jnp.broadcast_to(jnp.exp(gh * float(c)), (8, HEAD_DIM))

    heads = range(RET_HEADS)
    hsl = [slice(hd * HEAD_DIM, (hd + 1) * HEAD_DIM) for hd in heads]
    crows = [slice(ch * c, (ch + 1) * c) for ch in range(nc)]
    kts = [[kt_ref[0, hd, :, crows[ch]] for ch in range(nc)] for hd in heads]
    vs = [[v_ref[0, crows[ch], hsl[hd]] for ch in range(nc)] for hd in heads]

    sfs, sbs = [], []
    for hd in heads:
        cf = cdec_scr[hd, 0:1, :HEAD_DIM]
        cb = cdec_scr[hd, 0:1, HEAD_DIM:]
        sb = [None] * nc
        sb[nc - 1] = sbound_ref[0, 0, hd]
        for ch in range(nc - 1, 0, -1):
            kb = (kts[hd][ch].astype(F32) * kdec_scr[hd, 1:2, :]).astype(BF16)
            sb[ch - 1] = cb * sb[ch] + _dot(kb, vs[hd][ch])
        sf = [sf_scr[hd]]
        for ch in range(nc):
            kf = (kts[hd][ch].astype(F32) * kdec_scr[hd, 0:1, :]).astype(BF16)
            sf.append(cf * sf[ch] + _dot(kf, vs[hd][ch]))
        sf_scr[hd] = sf[nc]
        sfs.append(sf)
        sbs.append(sb)

    for ch in range(nc):
        qs = [q_ref[0, crows[ch], hsl[hd]] for hd in heads]
        scores = [_dot(qs[hd], kts[hd][ch]) for hd in heads]
        cross = []
        for hd in heads:
            q2 = (jnp.concatenate([qs[hd], qs[hd]], axis=1).astype(F32) * qdec_scr[hd]).astype(BF16)
            s2 = jnp.concatenate([sfs[hd][ch], sbs[hd][ch]], axis=0).astype(BF16)
            cross.append(_dot(q2, s2))
        ps = [(scores[hd] * dmat_scr[hd]).astype(BF16) for hd in heads]
        for hd in heads:
            o = _dot(ps[hd], vs[hd][ch]) + cross[hd]
            ms = jnp.mean(o * o, axis=-1, keepdims=True)
            o = o * lax.rsqrt(ms + EPS) * gate_ref[0, crows[ch], hsl[hd]].astype(F32)
            ret_scr[crows[ch], hsl[hd]] = o.astype(BF16)

    mix = _dot(ret_scr[...], wo_r_ref[...]) + _dot(conv_ref[0], wo_c_ref[...])
    o_ref[0] = x_ref[0] + g1_ref[pl.ds(b, 1), :] * mix


def _mix_call(q, kt, v, gate, conv, x, sbound, sf0, w_out, mod, dec_f_rows, dec_b_rows, w_up):
    b, n, d = x.shape
    tm = MIX_TILE
    nt = n // tm
    bound_per_tile = sbound.shape[1] // nt
    r = RET_WIDTH
    c = RET_CHUNK
    slab = w_up.shape[0] // (b * nt)
    assert slab * b * nt == w_up.shape[0] and slab % 16 == 0

    def tok(bi, i):
        return (bi, i, 0)

    seq_spec = pl.BlockSpec((1, tm, r), tok)
    slab_spec = pl.BlockSpec((slab, w_up.shape[1]), lambda bi, i: (bi * nt + i, 0))
    return pl.pallas_call(
        _mix_kernel,
        grid=(b, nt),
        in_specs=[
            seq_spec,
            pl.BlockSpec((1, RET_HEADS, HEAD_DIM, tm), lambda bi, i: (bi, 0, 0, i)),
            seq_spec, seq_spec, seq_spec,
            pl.BlockSpec((1, tm, d), tok),
            pl.BlockSpec((1, 1, RET_HEADS, HEAD_DIM, HEAD_DIM),
                         lambda bi, i: (bi, (i + 1) * bound_per_tile - 1, 0, 0, 0)),
            pl.BlockSpec((1, RET_HEADS, HEAD_DIM, HEAD_DIM), lambda bi, i: (bi, 0, 0, 0)),
            pl.BlockSpec((r, d), lambda bi, i: (0, 0)),
            pl.BlockSpec((CONV_CH, d), lambda bi, i: (1, 0)),
            pl.BlockSpec((MOD_ROWS, d), lambda bi, i: (0, 2)),
            pl.BlockSpec((RET_HEADS, c), lambda bi, i: (0, 0)),
            pl.BlockSpec((RET_HEADS, c), lambda bi, i: (0, 0)),
            slab_spec,
        ],
        out_specs=[pl.BlockSpec((1, tm, d), tok), slab_spec],
        out_shape=[jax.ShapeDtypeStruct((b, n, d), F32),
                   jax.ShapeDtypeStruct(w_up.shape, BF16)],
        scratch_shapes=[
            pltpu.VMEM((RET_HEADS, HEAD_DIM, HEAD_DIM), F32),
            pltpu.VMEM((RET_HEADS, c, c), F32),
            pltpu.VMEM((RET_HEADS, c, 2 * HEAD_DIM), F32),
            pltpu.VMEM((RET_HEADS, 8, c), F32),
            pltpu.VMEM((RET_HEADS, 8, 2 * HEAD_DIM), F32),
            pltpu.VMEM((tm, r), BF16),
        ],
        compiler_params=pltpu.CompilerParams(
            dimension_semantics=("arbitrary", "arbitrary"), vmem_limit_bytes=VMEM_LIMIT),
        name="mix",
    )(q, kt, v, gate, conv, x, sbound, sf0, w_out, w_out, mod, dec_f_rows, dec_b_rows, w_up)


def _ffn_kernel(x_ref, sh_ref, sc_ref, gt_ref, g2_ref, gf_ref,
                wa_ref, wb_ref, cw_ref, cb_ref, wd_ref,
                o_ref,
                h_scr):
    b = pl.program_id(0)
    f = pl.program_id(2)
    nf = pl.num_programs(2)
    tm, d = h_scr.shape
    tf = wa_ref.shape[1]
    blk = o_ref.shape[1:]
    step = blk[1]

    def body(first, last):
        if first:
            h = _norm_mod(x_ref[0].reshape(tm, d), g2_ref[...], sh_ref[pl.ds(b, 1), :],
                          sc_ref[pl.ds(b, 1), :]).astype(BF16)
            h_scr[...] = h
        else:
            h = h_scr[...]
        a = _dot(h, wa_ref[...])
        zero = jnp.zeros((step, tf), F32)
        a_up = jnp.concatenate([zero, a[:tm - step]], axis=0)
        a_dn = jnp.concatenate([a[step:], zero], axis=0)
        a = cw_ref[0:1, :] * a_up + cw_ref[1:2, :] * a + cw_ref[2:3, :] * a_dn + cb_ref[...]
        act = (a * _sigmoid(a) * _dot(h, wb_ref[...])).astype(BF16)
        y = _dot(act, wd_ref[...])
        acc = y if first else o_ref[0].reshape(tm, d) + y
        if last:
            z = x_ref[0].reshape(tm, d) + gt_ref[pl.ds(b, 1), :] * acc
            ms = jnp.mean(z * z, axis=-1, keepdims=True)
            acc = z * lax.rsqrt(ms + EPS) * gf_ref[...]
        o_ref[0] = acc.reshape(blk)

    @pl.when(f == 0)
    def _():
        body(True, False)

    @pl.when((f > 0) & (f < nf - 1))
    def _():
        body(False, False)

    @pl.when(f == nf - 1)
    def _():
        body(False, True)


def _ffn_call(x, mod, norm_g, final_g, w_up, ffn_conv_w, ffn_conv_b, w_down):
    b, n, d = x.shape
    d_ff = w_down.shape[0]
    rows = n // GRID_W
    cols = FFN_COLS
    tf = FF_TILE
    nf = d_ff // tf
    assert nf >= 2
    xg = x.reshape(b, rows, GRID_W, d)
    tile = pl.BlockSpec((1, rows, cols, d), lambda bi, i, f: (bi, 0, i, 0))

    out = pl.pallas_call(
        _ffn_kernel,
        grid=(b, GRID_W // cols, nf),
        in_specs=[
            tile,
            pl.BlockSpec((MOD_ROWS, d), lambda bi, i, f: (0, 3)),
            pl.BlockSpec((MOD_ROWS, d), lambda bi, i, f: (0, 4)),
            pl.BlockSpec((MOD_ROWS, d), lambda bi, i, f: (0, 5)),
            pl.BlockSpec((1, d), lambda bi, i, f: (0, 0)),
            pl.BlockSpec((1, d), lambda bi, i, f: (0, 0)),
            pl.BlockSpec((d, tf), lambda bi, i, f: (0, f)),
            pl.BlockSpec((d, tf), lambda bi, i, f: (0, nf + f)),
            pl.BlockSpec((3, tf), lambda bi, i, f: (0, f)),
            pl.BlockSpec((1, tf), lambda bi, i, f: (0, f)),
            pl.BlockSpec((tf, d), lambda bi, i, f: (f, 0)),
        ],
        out_specs=tile,
        out_shape=jax.ShapeDtypeStruct((b, rows, GRID_W, d), F32),
        scratch_shapes=[
            pltpu.VMEM((rows * cols, d), BF16),
        ],
        compiler_params=pltpu.CompilerParams(
            dimension_semantics=("arbitrary", "arbitrary", "arbitrary"),
            vmem_limit_bytes=VMEM_LIMIT),
        name="ffn",
    )(xg, mod, mod, mod, norm_g, final_g, w_up, w_up, ffn_conv_w, ffn_conv_b, w_down)
    return out.reshape(b, n, d)


def _rope_tables(n):
    pos = np.arange(n, dtype=np.int32)
    row = (pos // GRID_W).astype(np.float32)
    col = (pos % GRID_W).astype(np.float32)
    dh = HEAD_DIM // 2
    freqs = np.float32(ROPE_THETA) ** (-np.arange(0, dh, 2, dtype=np.float32) / np.float32(dh))
    parts_c, parts_s = [], []
    for p in (row, col):
        ang = (p[:, None] * freqs[None, :]).astype(np.float32)
        cos = np.cos(ang).astype(np.float32)
        sin = np.sin(ang).astype(np.float32)
        parts_c += [cos, cos]
        parts_s += [-sin, sin]
    return (jnp.asarray(np.concatenate(parts_c, axis=-1)),
            jnp.asarray(np.concatenate(parts_s, axis=-1)))


def kernel(x, c, ctx, c_ctx, w_mod, b_mod, norm1_g, w_in, ret_decay_fwd, ret_decay_bwd,
           conv_w, w_out, norm2_g, w_up, ffn_conv_w, ffn_conv_b, w_down, final_g):
    b, n, d = x.shape
    depth = w_mod.shape[0]
    assert depth == 1 and n % MIX_TILE == 0 and n % GRID_W == 0
    assert MIX_TILE % INPROJ_TILE == 0 and MIX_TILE % RET_CHUNK == 0
    cos_t, sin_t = _rope_tables(n)
    craw = jnp.concatenate(
        [c, c_ctx[None, :], jnp.zeros((MOD_ROWS - b - 1, d), F32)], axis=0)

    layer = 0
    mod = _mod_call(craw, w_mod[layer], b_mod[layer][None, :])
    w_in_b = w_in[layer].astype(BF16)
    w_out_b = w_out[layer].astype(BF16)
    w_down_b = w_down[layer].astype(BF16)
    dec_f = jnp.repeat(ret_decay_fwd[layer], HEAD_DIM)[None, :]
    dec_b = jnp.repeat(ret_decay_bwd[layer], HEAD_DIM)[None, :]
    dec_f_rows = jnp.broadcast_to(ret_decay_fwd[layer][:, None], (RET_HEADS, RET_CHUNK))
    dec_b_rows = jnp.broadcast_to(ret_decay_bwd[layer][:, None], (RET_HEADS, RET_CHUNK))
    g1 = norm1_g[layer][None, :]

    sf0, sb0 = _ctx_call(ctx, mod, g1, w_in_b, dec_f, dec_b)
    dec_b_tile = jnp.broadcast_to(ret_decay_bwd[layer][:, None], (RET_HEADS, INPROJ_TILE))
    q, kt, v, gate, conv, sbound = _inproj_call(
        x, mod, g1, w_in_b, cos_t, sin_t, conv_w[layer], dec_b_tile, sb0)
    x_mid, w_up_b = _mix_call(q, kt, v, gate, conv, x, sbound, sf0, w_out_b, mod,
                              dec_f_rows, dec_b_rows, w_up[layer])
    return _ffn_call(x_mid, mod, norm2_g[layer][None, :], final_g[None, :], w_up_b,
                     ffn_conv_w[layer], ffn_conv_b[layer][None, :], w_down_b)
```

```python
import functools

import jax
import jax.numpy as jnp
import numpy as np
from jax import lax
from jax.experimental import pallas as pl
from jax.experimental.pallas import tpu as pltpu

F32 = jnp.float32
BF16 = jnp.bfloat16

GRID_W = 64
RET_HEADS = 8
HEAD_DIM = 128
RET_WIDTH = RET_HEADS * HEAD_DIM
CONV_CH = 1024
N_MOD = 6
EPS = 1e-6
ROPE_THETA = 10000.0

MOD_ROWS = 8
INPROJ_TILE = 512
MIX_TILE = 512
FFN_COLS = 8
RET_CHUNK = 256
FF_TILE = 512
MOD_COL_TILE = 1024
LAST_STEP_ROW_GROUPS = 4
VMEM_LIMIT = 60 * 1024 * 1024


def _sigmoid(x):
    return 1.0 / (1.0 + jnp.exp(-x))


def _log_sigmoid(x):
    return jnp.minimum(x, 0.0) - jnp.log(1.0 + jnp.exp(-jnp.abs(x)))


def _norm_mod(x, gain, shift, scale):
    ms = jnp.mean(x * x, axis=-1, keepdims=True)
    y = x * lax.rsqrt(ms + EPS) * gain
    return y * (1.0 + scale) + shift


def _dot(a, b):
    return jnp.dot(a, b, preferred_element_type=F32)


def _dot_t0(a, b):
    return lax.dot_general(a, b, (((0,), (0,)), ((), ())), preferred_element_type=F32)


def _mod_kernel(c_ref, w_ref, b_ref, o_ref):
    a = c_ref[...]
    a = (a * _sigmoid(a)).astype(BF16)
    o_ref[...] = _dot(a, w_ref[...].astype(BF16)) + b_ref[...]


def _mod_call(craw, w_mod, b_mod):
    d, n = w_mod.shape
    tn = MOD_COL_TILE
    return pl.pallas_call(
        _mod_kernel,
        grid=(n // tn,),
        in_specs=[
            pl.BlockSpec((MOD_ROWS, d), lambda j: (0, 0)),
            pl.BlockSpec((d, tn), lambda j: (0, j)),
            pl.BlockSpec((1, tn), lambda j: (0, j)),
        ],
        out_specs=pl.BlockSpec((MOD_ROWS, tn), lambda j: (0, j)),
        out_shape=jax.ShapeDtypeStruct((MOD_ROWS, n), F32),
        compiler_params=pltpu.CompilerParams(
            dimension_semantics=("arbitrary",), vmem_limit_bytes=VMEM_LIMIT),
        name="mod",
    )(craw, w_mod, b_mod)


def _ctx_kernel(ctx_row, x_ref, sh_ref, sc_ref, g_ref, wk_ref, wv_ref, decf_ref, decb_ref,
                sf_ref, sb_ref):
    x = x_ref[0]
    length = x.shape[0]
    h = _norm_mod(x, g_ref[...], sh_ref[ctx_row:ctx_row + 1, :],
                  sc_ref[ctx_row:ctx_row + 1, :]).astype(BF16)
    k = _dot(h, wk_ref[...]) * (HEAD_DIM ** -0.5)
    v = _dot(h, wv_ref[...]).astype(BF16)
    t = lax.broadcasted_iota(jnp.int32, k.shape, 0).astype(F32)
    wf = jnp.exp(_log_sigmoid(decf_ref[...]) * (length - 1.0 - t))
    wb = jnp.exp(_log_sigmoid(decb_ref[...]) * t)
    kf = (k * wf).astype(BF16)
    kb = (k * wb).astype(BF16)
    for hd in range(RET_HEADS):
        sl = slice(hd * HEAD_DIM, (hd + 1) * HEAD_DIM)
        sf_ref[0, hd] = _dot_t0(kf[:, sl], v[:, sl])
        sb_ref[0, hd] = _dot_t0(kb[:, sl], v[:, sl])


def _ctx_call(ctx, mod, norm_g, w_in, dec_f, dec_b):
    b, length, d = ctx.shape
    r = RET_WIDTH
    state = jax.ShapeDtypeStruct((b, RET_HEADS, HEAD_DIM, HEAD_DIM), F32)
    state_spec = pl.BlockSpec((1, RET_HEADS, HEAD_DIM, HEAD_DIM), lambda i: (i, 0, 0, 0))
    return pl.pallas_call(
        functools.partial(_ctx_kernel, b),
        grid=(b,),
        in_specs=[
            pl.BlockSpec((1, length, d), lambda i: (i, 0, 0)),
            pl.BlockSpec((MOD_ROWS, d), lambda i: (0, 0)),
            pl.BlockSpec((MOD_ROWS, d), lambda i: (0, 1)),
            pl.BlockSpec((1, d), lambda i: (0, 0)),
            pl.BlockSpec((d, r), lambda i: (0, 1)),
            pl.BlockSpec((d, r), lambda i: (0, 2)),
            pl.BlockSpec((1, r), lambda i: (0, 0)),
            pl.BlockSpec((1, r), lambda i: (0, 0)),
        ],
        out_specs=[state_spec, state_spec],
        out_shape=[state, state],
        compiler_params=pltpu.CompilerParams(
            dimension_semantics=("arbitrary",), vmem_limit_bytes=VMEM_LIMIT),
        name="ctx",
    )(ctx, mod, mod, norm_g, w_in, w_in, dec_f, dec_b)


def _rope(acc, cos, sin, scale):
    lane = lax.broadcasted_iota(jnp.int32, (1, HEAD_DIM), 1)
    lo = (lane % 64) < 32
    outs = []
    for hd in range(RET_HEADS):
        xh = acc[:, hd * HEAD_DIM:(hd + 1) * HEAD_DIM]
        partner = jnp.where(lo, pltpu.roll(xh, 96, 1), pltpu.roll(xh, 32, 1))
        y = xh * cos + partner * sin
        outs.append(y if scale == 1.0 else y * scale)
    return outs


def _inproj_kernel(x_ref, sh_ref, sc_ref, g_ref, w_ref, cos_ref, sin_ref, cw_ref, decb_ref,
                   sb0_ref, wout_ref,
                   q_ref, kt_ref, v_ref, gate_ref, conv_ref, sbound_ref, wout_b_ref,
                   s_scr, dec_scr):
    b = pl.program_id(0)
    i = pl.program_id(1)
    tm = dec_scr.shape[1]
    r = RET_WIDTH

    wout_b_ref[...] = wout_ref[...].astype(wout_b_ref.dtype)

    @pl.when(i == 0)
    def _():
        s_scr[...] = sb0_ref[0]
        t = lax.broadcasted_iota(jnp.int32, dec_scr.shape, 1).astype(F32)
        dec_scr[...] = jnp.exp(_log_sigmoid(decb_ref[...]) * t)

    h = _norm_mod(x_ref[0], g_ref[...], sh_ref[pl.ds(b, 1), :], sc_ref[pl.ds(b, 1), :]).astype(BF16)

    def proj(group):
        return _dot(h, w_ref[:, group * r:(group + 1) * r])

    cos = cos_ref[...]
    sin = sin_ref[...]
    for hd, qh in enumerate(_rope(proj(0), cos, sin, 1.0)):
        q_ref[0, :, hd * HEAD_DIM:(hd + 1) * HEAD_DIM] = qh.astype(q_ref.dtype)
    ks = _rope(proj(1), cos, sin, HEAD_DIM ** -0.5)
    vb = proj(2).astype(BF16)
    v_ref[0] = vb
    tile_dec = jnp.exp(_log_sigmoid(decb_ref[:, :HEAD_DIM]) * float(tm))
    for hd in range(RET_HEADS):
        kt = ks[hd].T
        kt_ref[0, hd] = kt.astype(kt_ref.dtype)
        ktd = (kt * dec_scr[hd:hd + 1, :]).astype(BF16)
        s_old = s_scr[hd]
        sbound_ref[0, 0, hd] = s_old
        s_scr[hd] = tile_dec[hd:hd + 1, :] * s_old + _dot(ktd, vb[:, hd * HEAD_DIM:(hd + 1) * HEAD_DIM])

    g = proj(3)
    gate_ref[0] = (g * _sigmoid(g)).astype(gate_ref.dtype)

    u = proj(5) * proj(6)
    col = lax.broadcasted_iota(jnp.int32, (tm, 1), 0) % GRID_W
    up = jnp.where(col == 0, 0.0, pltpu.roll(u, 1, 0))
    un = jnp.where(col == GRID_W - 1, 0.0, pltpu.roll(u, tm - 1, 0))
    y = proj(4) * (cw_ref[0:1, :] * up + cw_ref[1:2, :] * u + cw_ref[2:3, :] * un)
    conv_ref[0] = y.astype(conv_ref.dtype)


def _inproj_call(x, mod, norm_g, w_in, cos_t, sin_t, conv_w, dec_b_rows, sb0, w_out):
    b, n, d = x.shape
    tm = INPROJ_TILE
    nt = n // tm
    r = RET_WIDTH
    assert dec_b_rows.shape == (RET_HEADS, tm)
    slab = w_out.shape[0] // (b * nt)
    assert slab * b * nt == w_out.shape[0] and slab % 16 == 0
    slab_spec = pl.BlockSpec((slab, w_out.shape[1]), lambda bi, i: (bi * nt + i, 0))

    def tok(bi, i):
        return (bi, nt - 1 - i, 0)

    seq = jax.ShapeDtypeStruct((b, n, r), BF16)
    seq_spec = pl.BlockSpec((1, tm, r), tok)
    keys_t = jax.ShapeDtypeStruct((b, RET_HEADS, HEAD_DIM, n), BF16)
    keys_t_spec = pl.BlockSpec((1, RET_HEADS, HEAD_DIM, tm), lambda bi, i: (bi, 0, 0, nt - 1 - i))
    return pl.pallas_call(
        _inproj_kernel,
        grid=(b, nt),
        in_specs=[
            pl.BlockSpec((1, tm, d), tok),
            pl.BlockSpec((MOD_ROWS, d), lambda bi, i: (0, 0)),
            pl.BlockSpec((MOD_ROWS, d), lambda bi, i: (0, 1)),
            pl.BlockSpec((1, d), lambda bi, i: (0, 0)),
            pl.BlockSpec(w_in.shape, lambda bi, i: (0, 0), pipeline_mode=pl.Buffered(1)),
            pl.BlockSpec((tm, HEAD_DIM), lambda bi, i: (nt - 1 - i, 0)),
            pl.BlockSpec((tm, HEAD_DIM), lambda bi, i: (nt - 1 - i, 0)),
            pl.BlockSpec((3, CONV_CH), lambda bi, i: (0, 0)),
            pl.BlockSpec((RET_HEADS, tm), lambda bi, i: (0, 0)),
            pl.BlockSpec((1, RET_HEADS, HEAD_DIM, HEAD_DIM), lambda bi, i: (bi, 0, 0, 0)),
            slab_spec,
        ],
        out_specs=[
            seq_spec, keys_t_spec, seq_spec, seq_spec, seq_spec,
            pl.BlockSpec((1, 1, RET_HEADS, HEAD_DIM, HEAD_DIM),
                         lambda bi, i: (bi, nt - 1 - i, 0, 0, 0)),
            slab_spec,
        ],
        out_shape=[seq, keys_t, seq, seq, seq,
                   jax.ShapeDtypeStruct((b, nt, RET_HEADS, HEAD_DIM, HEAD_DIM), F32),
                   jax.ShapeDtypeStruct(w_out.shape, BF16)],
        scratch_shapes=[
            pltpu.VMEM((RET_HEADS, HEAD_DIM, HEAD_DIM), F32),
            pltpu.VMEM((RET_HEADS, tm), F32),
        ],
        compiler_params=pltpu.CompilerParams(
            dimension_semantics=("arbitrary", "arbitrary"),
            vmem_limit_bytes=VMEM_LIMIT),
        name="inproj",
    )(x, mod, mod, norm_g, w_in, cos_t, sin_t, conv_w, dec_b_rows, sb0, w_out)


def _mix_kernel(q_ref, kt_ref, v_ref, gate_ref, conv_ref, x_ref, sbound_ref, sf0_ref,
                wo_r_ref, wo_c_ref, g1_ref, decf_ref, decb_ref, wup_ref, wdn_ref,
                o_ref, wup_b_ref, wdn_b_ref,
                sf_scr, dmat_scr, qdec_scr, kdec_scr, cdec_scr, ret_scr):
    b = pl.program_id(0)
    i = pl.program_id(1)
    tm = ret_scr.shape[0]
    c = dmat_scr.shape[1]
    nc = tm // c

    wup_b_ref[...] = wup_ref[...].astype(wup_b_ref.dtype)
    wdn_b_ref[...] = wdn_ref[...].astype(wdn_b_ref.dtype)

    @pl.when(i == 0)
    def _():
        sf_scr[...] = sf0_ref[0]

    @pl.when((b == 0) & (i == 0))
    def _():
        lgf = _log_sigmoid(decf_ref[...])
        lgb = _log_sigmoid(decb_ref[...])
        ri = lax.broadcasted_iota(jnp.int32, (c, c), 0)
        ci = lax.broadcasted_iota(jnp.int32, (c, c), 1)
        rel = (ri - ci).astype(F32)
        t = lax.broadcasted_iota(jnp.int32, (c, HEAD_DIM), 0).astype(F32)
        tl = lax.broadcasted_iota(jnp.int32, (1, c), 1).astype(F32)
        for hd in range(RET_HEADS):
            f = lgf[hd:hd + 1, :]
            g = lgb[hd:hd + 1, :]
            dmat_scr[hd] = jnp.where(rel >= 0, jnp.exp(f * jnp.maximum(rel, 0.0)),
                                     jnp.exp(g * jnp.maximum(-rel, 0.0)))
            fh = f[:, :HEAD_DIM]
            gh = g[:, :HEAD_DIM]
            qdec_scr[hd, :, :HEAD_DIM] = jnp.exp(fh * (t + 1.0))
            qdec_scr[hd, :, HEAD_DIM:] = jnp.exp(gh * (float(c) - t))
            kdec_scr[hd, 0:1, :] = jnp.exp(f * (float(c) - 1.0 - tl))
            kdec_scr[hd, 1:2, :] = jnp.exp(g * tl)
            cdec_scr[hd, :, :HEAD_DIM] = jnp.broadcast_to(jnp.exp(fh * float(c)), (8, HEAD_DIM))
            cdec_scr[hd, :, HEAD_DIM:] = jnp.broadcast_to(jnp.exp(gh * float(c)), (8, HEAD_DIM))

    heads = range(RET_HEADS)
    hsl = [slice(hd * HEAD_DIM, (hd + 1) * HEAD_DIM) for hd in heads]
    crows = [slice(ch * c, (ch + 1) * c) for ch in range(nc)]
    kts = [[kt_ref[0, hd, :, crows[ch]] for ch in range(nc)] for hd in heads]
    vs = [[v_ref[0, crows[ch], hsl[hd]] for ch in range(nc)] for hd in heads]

    sfs, sbs = [], []
    for hd in heads:
        cf = cdec_scr[hd, 0:1, :HEAD_DIM]
        cb = cdec_scr[hd, 0:1, HEAD_DIM:]
        sb = [None] * nc
        sb[nc - 1] = sbound_ref[0, 0, hd]
        for ch in range(nc - 1, 0, -1):
            kb = (kts[hd][ch].astype(F32) * kdec_scr[hd, 1:2, :]).astype(BF16)
            sb[ch - 1] = cb * sb[ch] + _dot(kb, vs[hd][ch])
        sf = [sf_scr[hd]]
        for ch in range(nc):
            kf = (kts[hd][ch].astype(F32) * kdec_scr[hd, 0:1, :]).astype(BF16)
            sf.append(cf * sf[ch] + _dot(kf, vs[hd][ch]))
        sf_scr[hd] = sf[nc]
        sfs.append(sf)
        sbs.append(sb)

    for ch in range(nc):
        qs = [q_ref[0, crows[ch], hsl[hd]] for hd in heads]
        scores = [_dot(qs[hd], kts[hd][ch]) for hd in heads]
        cross = []
        for hd in heads:
            q2 = (jnp.concatenate([qs[hd], qs[hd]], axis=1).astype(F32) * qdec_scr[hd]).astype(BF16)
            s2 = jnp.concatenate([sfs[hd][ch], sbs[hd][ch]], axis=0).astype(BF16)
            cross.append(_dot(q2, s2))
        ps = [(scores[hd] * dmat_scr[hd]).astype(BF16) for hd in heads]
        for hd in heads:
            o = _dot(ps[hd], vs[hd][ch]) + cross[hd]
            ms = jnp.mean(o * o, axis=-1, keepdims=True)
            o = o * lax.rsqrt(ms + EPS) * gate_ref[0, crows[ch], hsl[hd]].astype(F32)
            ret_scr[crows[ch], hsl[hd]] = o.astype(BF16)

    mix = _dot(ret_scr[...], wo_r_ref[...]) + _dot(conv_ref[0], wo_c_ref[...])
    o_ref[0] = x_ref[0] + g1_ref[pl.ds(b, 1), :] * mix


def _mix_call(q, kt, v, gate, conv, x, sbound, sf0, w_out, mod, dec_f_rows, dec_b_rows, w_up,
              w_down):
    b, n, d = x.shape
    tm = MIX_TILE
    nt = n // tm
    bound_per_tile = sbound.shape[1] // nt
    r = RET_WIDTH
    c = RET_CHUNK

    def slab_spec_of(w):
        slab = w.shape[0] // (b * nt)
        assert slab * b * nt == w.shape[0] and slab % 16 == 0
        return pl.BlockSpec((slab, w.shape[1]), lambda bi, i: (bi * nt + i, 0))

    def tok(bi, i):
        return (bi, i, 0)

    seq_spec = pl.BlockSpec((1, tm, r), tok)
    return pl.pallas_call(
        _mix_kernel,
        grid=(b, nt),
        in_specs=[
            seq_spec,
            pl.BlockSpec((1, RET_HEADS, HEAD_DIM, tm), lambda bi, i: (bi, 0, 0, i)),
            seq_spec, seq_spec, seq_spec,
            pl.BlockSpec((1, tm, d), tok),
            pl.BlockSpec((1, 1, RET_HEADS, HEAD_DIM, HEAD_DIM),
                         lambda bi, i: (bi, (i + 1) * bound_per_tile - 1, 0, 0, 0)),
            pl.BlockSpec((1, RET_HEADS, HEAD_DIM, HEAD_DIM), lambda bi, i: (bi, 0, 0, 0)),
            pl.BlockSpec((r, d), lambda bi, i: (0, 0)),
            pl.BlockSpec((CONV_CH, d), lambda bi, i: (1, 0)),
            pl.BlockSpec((MOD_ROWS, d), lambda bi, i: (0, 2)),
            pl.BlockSpec((RET_HEADS, c), lambda bi, i: (0, 0)),
            pl.BlockSpec((RET_HEADS, c), lambda bi, i: (0, 0)),
            slab_spec_of(w_up),
            slab_spec_of(w_down),
        ],
        out_specs=[pl.BlockSpec((1, tm, d), tok), slab_spec_of(w_up), slab_spec_of(w_down)],
        out_shape=[jax.ShapeDtypeStruct((b, n, d), F32),
                   jax.ShapeDtypeStruct(w_up.shape, BF16),
                   jax.ShapeDtypeStruct(w_down.shape, BF16)],
        scratch_shapes=[
            pltpu.VMEM((RET_HEADS, HEAD_DIM, HEAD_DIM), F32),
            pltpu.VMEM((RET_HEADS, c, c), F32),
            pltpu.VMEM((RET_HEADS, c, 2 * HEAD_DIM), F32),
            pltpu.VMEM((RET_HEADS, 8, c), F32),
            pltpu.VMEM((RET_HEADS, 8, 2 * HEAD_DIM), F32),
            pltpu.VMEM((tm, r), BF16),
        ],
        compiler_params=pltpu.CompilerParams(
            dimension_semantics=("arbitrary", "arbitrary"), vmem_limit_bytes=VMEM_LIMIT),
        name="mix",
    )(q, kt, v, gate, conv, x, sbound, sf0, w_out, w_out, mod, dec_f_rows, dec_b_rows, w_up,
      w_down)


def _ffn_kernel(x_ref, sh_ref, sc_ref, gt_ref, g2_ref, gf_ref,
                wa_ref, wb_ref, cw_ref, cb_ref, wd_ref,
                o_ref,
                h_scr):
    b = pl.program_id(0)
    f = pl.program_id(2)
    nf = pl.num_programs(2)
    tm, d = h_scr.shape
    tf = wa_ref.shape[1]
    blk = o_ref.shape[1:]
    step = blk[1]

    def body(first, last):
        if first:
            h = _norm_mod(x_ref[0].reshape(tm, d), g2_ref[...], sh_ref[pl.ds(b, 1), :],
                          sc_ref[pl.ds(b, 1), :]).astype(BF16)
            h_scr[...] = h
        else:
            h = h_scr[...]
        a = _dot(h, wa_ref[...])
        zero = jnp.zeros((step, tf), F32)
        a_up = jnp.concatenate([zero, a[:tm - step]], axis=0)
        a_dn = jnp.concatenate([a[step:], zero], axis=0)
        a = cw_ref[0:1, :] * a_up + cw_ref[1:2, :] * a + cw_ref[2:3, :] * a_dn + cb_ref[...]
        act = (a * _sigmoid(a) * _dot(h, wb_ref[...])).astype(BF16)
        if not last:
            y = _dot(act, wd_ref[...])
            o_ref[0] = (y if first else o_ref[0].reshape(tm, d) + y).reshape(blk)
            return
        groups = LAST_STEP_ROW_GROUPS
        gr = blk[0] // groups
        ys = [_dot(act[g * gr * step:(g + 1) * gr * step], wd_ref[...]) for g in range(groups)]
        for g in range(groups):
            rs = slice(g * gr, (g + 1) * gr)
            acc = o_ref[0, rs].reshape(gr * step, d) + ys[g]
            z = x_ref[0, rs].reshape(gr * step, d) + gt_ref[pl.ds(b, 1), :] * acc
            ms = jnp.mean(z * z, axis=-1, keepdims=True)
            o_ref[0, rs] = (z * lax.rsqrt(ms + EPS) * gf_ref[...]).reshape(gr, step, d)

    @pl.when(f == 0)
    def _():
        body(True, False)

    @pl.when((f > 0) & (f < nf - 1))
    def _():
        body(False, False)

    @pl.when(f == nf - 1)
    def _():
        body(False, True)


def _ffn_call(x, mod, norm_g, final_g, w_up, ffn_conv_w, ffn_conv_b, w_down):
    b, n, d = x.shape
    d_ff = w_down.shape[0]
    rows = n // GRID_W
    cols = FFN_COLS
    tf = FF_TILE
    nf = d_ff // tf
    assert nf >= 2
    xg = x.reshape(b, rows, GRID_W, d)
    tile = pl.BlockSpec((1, rows, cols, d), lambda bi, i, f: (bi, 0, i, 0))

    out = pl.pallas_call(
        _ffn_kernel,
        grid=(b, GRID_W // cols, nf),
        in_specs=[
            tile,
            pl.BlockSpec((MOD_ROWS, d), lambda bi, i, f: (0, 3)),
            pl.BlockSpec((MOD_ROWS, d), lambda bi, i, f: (0, 4)),
            pl.BlockSpec((MOD_ROWS, d), lambda bi, i, f: (0, 5)),
            pl.BlockSpec((1, d), lambda bi, i, f: (0, 0)),
            pl.BlockSpec((1, d), lambda bi, i, f: (0, 0)),
            pl.BlockSpec((d, tf), lambda bi, i, f: (0, f)),
            pl.BlockSpec((d, tf), lambda bi, i, f: (0, nf + f)),
            pl.BlockSpec((3, tf), lambda bi, i, f: (0, f)),
            pl.BlockSpec((1, tf), lambda bi, i, f: (0, f)),
            pl.BlockSpec((tf, d), lambda bi, i, f: (f, 0)),
        ],
        out_specs=tile,
        out_shape=jax.ShapeDtypeStruct((b, rows, GRID_W, d), F32),
        scratch_shapes=[
            pltpu.VMEM((rows * cols, d), BF16),
        ],
        compiler_params=pltpu.CompilerParams(
            dimension_semantics=("arbitrary", "arbitrary", "arbitrary"),
            vmem_limit_bytes=VMEM_LIMIT),
        name="ffn",
    )(xg, mod, mod, mod, norm_g, final_g, w_up, w_up, ffn_conv_w, ffn_conv_b, w_down)
    return out.reshape(b, n, d)


def _rope_tables(n):
    pos = np.arange(n, dtype=np.int32)
    row = (pos // GRID_W).astype(np.float32)
    col = (pos % GRID_W).astype(np.float32)
    dh = HEAD_DIM // 2
    freqs = np.float32(ROPE_THETA) ** (-np.arange(0, dh, 2, dtype=np.float32) / np.float32(dh))
    parts_c, parts_s = [], []
    for p in (row, col):
        ang = (p[:, None] * freqs[None, :]).astype(np.float32)
        cos = np.cos(ang).astype(np.float32)
        sin = np.sin(ang).astype(np.float32)
        parts_c += [cos, cos]
        parts_s += [-sin, sin]
    return (jnp.asarray(np.concatenate(parts_c, axis=-1)),
            jnp.asarray(np.concatenate(parts_s, axis=-1)))


def kernel(x, c, ctx, c_ctx, w_mod, b_mod, norm1_g, w_in, ret_decay_fwd, ret_decay_bwd,
           conv_w, w_out, norm2_g, w_up, ffn_conv_w, ffn_conv_b, w_down, final_g):
    b, n, d = x.shape
    depth = w_mod.shape[0]
    assert depth == 1 and n % MIX_TILE == 0 and n % GRID_W == 0
    assert MIX_TILE % INPROJ_TILE == 0 and MIX_TILE % RET_CHUNK == 0
    cos_t, sin_t = _rope_tables(n)
    craw = jnp.concatenate(
        [c, c_ctx[None, :], jnp.zeros((MOD_ROWS - b - 1, d), F32)], axis=0)

    layer = 0
    mod = _mod_call(craw, w_mod[layer], b_mod[layer][None, :])
    w_in_b = w_in[layer].astype(BF16)
    dec_f = jnp.repeat(ret_decay_fwd[layer], HEAD_DIM)[None, :]
    dec_b = jnp.repeat(ret_decay_bwd[layer], HEAD_DIM)[None, :]
    dec_f_rows = jnp.broadcast_to(ret_decay_fwd[layer][:, None], (RET_HEADS, RET_CHUNK))
    dec_b_rows = jnp.broadcast_to(ret_decay_bwd[layer][:, None], (RET_HEADS, RET_CHUNK))
    g1 = norm1_g[layer][None, :]

    sf0, sb0 = _ctx_call(ctx, mod, g1, w_in_b, dec_f, dec_b)
    dec_b_tile = jnp.broadcast_to(ret_decay_bwd[layer][:, None], (RET_HEADS, INPROJ_TILE))
    q, kt, v, gate, conv, sbound, w_out_b = _inproj_call(
        x, mod, g1, w_in_b, cos_t, sin_t, conv_w[layer], dec_b_tile, sb0, w_out[layer])
    x_mid, w_up_b, w_down_b = _mix_call(q, kt, v, gate, conv, x, sbound, sf0, w_out_b, mod,
                                        dec_f_rows, dec_b_rows, w_up[layer], w_down[layer])
    return _ffn_call(x_mid, mod, norm2_g[layer][None, :], final_g[None, :], w_up_b,
                     ffn_conv_w[layer], ffn_conv_b[layer][None, :], w_down_b)
```

```python
import functools

import jax
import jax.numpy as jnp
import numpy as np
from jax import lax
from jax.experimental import pallas as pl
from jax.experimental.pallas import tpu as pltpu

F32 = jnp.float32
BF16 = jnp.bfloat16

GRID_W = 64
RET_HEADS = 8
HEAD_DIM = 128
RET_WIDTH = RET_HEADS * HEAD_DIM
CONV_CH = 1024
N_MOD = 6
EPS = 1e-6
ROPE_THETA = 10000.0

MOD_ROWS = 8
INPROJ_TILE = 512
MIX_TILE = 512
FFN_COLS = 8
RET_CHUNK = 256
FF_TILE = 512
MOD_COL_TILE = 768
LAST_STEP_ROW_GROUPS = 4
VMEM_LIMIT = 60 * 1024 * 1024


def _sigmoid(x):
    return 1.0 / (1.0 + jnp.exp(-x))


def _log_sigmoid(x):
    return jnp.minimum(x, 0.0) - jnp.log(1.0 + jnp.exp(-jnp.abs(x)))


def _norm_mod(x, gain, shift, scale):
    ms = jnp.mean(x * x, axis=-1, keepdims=True)
    y = x * lax.rsqrt(ms + EPS) * gain
    return y * (1.0 + scale) + shift


def _dot(a, b):
    return jnp.dot(a, b, preferred_element_type=F32)


def _dot_t0(a, b):
    return lax.dot_general(a, b, (((0,), (0,)), ((), ())), preferred_element_type=F32)


def _mod_kernel(c_ref, w_ref, b_ref, win_ref, o_ref, win_b_ref):
    a = c_ref[...]
    a = (a * _sigmoid(a)).astype(BF16)
    o_ref[...] = _dot(a, w_ref[...].astype(BF16)) + b_ref[...]
    win_b_ref[...] = win_ref[...].astype(win_b_ref.dtype)


def _mod_call(craw, w_mod, b_mod, w_in):
    d, n = w_mod.shape
    tn = MOD_COL_TILE
    steps = n // tn
    slab = w_in.shape[0] // steps
    assert steps * tn == n and slab * steps == w_in.shape[0] and slab % 16 == 0
    slab_spec = pl.BlockSpec((slab, w_in.shape[1]), lambda j: (j, 0))
    return pl.pallas_call(
        _mod_kernel,
        grid=(steps,),
        in_specs=[
            pl.BlockSpec((MOD_ROWS, d), lambda j: (0, 0)),
            pl.BlockSpec((d, tn), lambda j: (0, j)),
            pl.BlockSpec((1, tn), lambda j: (0, j)),
            slab_spec,
        ],
        out_specs=[pl.BlockSpec((MOD_ROWS, tn), lambda j: (0, j)), slab_spec],
        out_shape=[jax.ShapeDtypeStruct((MOD_ROWS, n), F32),
                   jax.ShapeDtypeStruct(w_in.shape, BF16)],
        compiler_params=pltpu.CompilerParams(
            dimension_semantics=("arbitrary",), vmem_limit_bytes=VMEM_LIMIT),
        name="mod",
    )(craw, w_mod, b_mod, w_in)


def _ctx_kernel(ctx_row, x_ref, sh_ref, sc_ref, g_ref, wk_ref, wv_ref, decf_ref, decb_ref,
                sf_ref, sb_ref):
    x = x_ref[0]
    length = x.shape[0]
    h = _norm_mod(x, g_ref[...], sh_ref[ctx_row:ctx_row + 1, :],
                  sc_ref[ctx_row:ctx_row + 1, :]).astype(BF16)
    k = _dot(h, wk_ref[...]) * (HEAD_DIM ** -0.5)
    v = _dot(h, wv_ref[...]).astype(BF16)
    t = lax.broadcasted_iota(jnp.int32, k.shape, 0).astype(F32)
    wf = jnp.exp(_log_sigmoid(decf_ref[...]) * (length - 1.0 - t))
    wb = jnp.exp(_log_sigmoid(decb_ref[...]) * t)
    kf = (k * wf).astype(BF16)
    kb = (k * wb).astype(BF16)
    for hd in range(RET_HEADS):
        sl = slice(hd * HEAD_DIM, (hd + 1) * HEAD_DIM)
        sf_ref[0, hd] = _dot_t0(kf[:, sl], v[:, sl])
        sb_ref[0, hd] = _dot_t0(kb[:, sl], v[:, sl])


def _ctx_call(ctx, mod, norm_g, w_in, dec_f, dec_b):
    b, length, d = ctx.shape
    r = RET_WIDTH
    state = jax.ShapeDtypeStruct((b, RET_HEADS, HEAD_DIM, HEAD_DIM), F32)
    state_spec = pl.BlockSpec((1, RET_HEADS, HEAD_DIM, HEAD_DIM), lambda i: (i, 0, 0, 0))
    return pl.pallas_call(
        functools.partial(_ctx_kernel, b),
        grid=(b,),
        in_specs=[
            pl.BlockSpec((1, length, d), lambda i: (i, 0, 0)),
            pl.BlockSpec((MOD_ROWS, d), lambda i: (0, 0)),
            pl.BlockSpec((MOD_ROWS, d), lambda i: (0, 1)),
            pl.BlockSpec((1, d), lambda i: (0, 0)),
            pl.BlockSpec((d, r), lambda i: (0, 1)),
            pl.BlockSpec((d, r), lambda i: (0, 2)),
            pl.BlockSpec((1, r), lambda i: (0, 0)),
            pl.BlockSpec((1, r), lambda i: (0, 0)),
        ],
        out_specs=[state_spec, state_spec],
        out_shape=[state, state],
        compiler_params=pltpu.CompilerParams(
            dimension_semantics=("arbitrary",), vmem_limit_bytes=VMEM_LIMIT),
        name="ctx",
    )(ctx, mod, mod, norm_g, w_in, w_in, dec_f, dec_b)


def _rope(acc, cos, sin, scale):
    lane = lax.broadcasted_iota(jnp.int32, (1, HEAD_DIM), 1)
    lo = (lane % 64) < 32
    outs = []
    for hd in range(RET_HEADS):
        xh = acc[:, hd * HEAD_DIM:(hd + 1) * HEAD_DIM]
        partner = jnp.where(lo, pltpu.roll(xh, 96, 1), pltpu.roll(xh, 32, 1))
        y = xh * cos + partner * sin
        outs.append(y if scale == 1.0 else y * scale)
    return outs


def _inproj_kernel(x_ref, sh_ref, sc_ref, g_ref, w_ref, cos_ref, sin_ref, cw_ref, decb_ref,
                   sb0_ref, wout_ref,
                   q_ref, kt_ref, v_ref, gate_ref, conv_ref, sbound_ref, wout_b_ref,
                   s_scr, dec_scr):
    b = pl.program_id(0)
    i = pl.program_id(1)
    tm = dec_scr.shape[1]
    r = RET_WIDTH

    wout_b_ref[...] = wout_ref[...].astype(wout_b_ref.dtype)

    @pl.when(i == 0)
    def _():
        s_scr[...] = sb0_ref[0]
        t = lax.broadcasted_iota(jnp.int32, dec_scr.shape, 1).astype(F32)
        dec_scr[...] = jnp.exp(_log_sigmoid(decb_ref[...]) * t)

    h = _norm_mod(x_ref[0], g_ref[...], sh_ref[pl.ds(b, 1), :], sc_ref[pl.ds(b, 1), :]).astype(BF16)

    def proj(group):
        return _dot(h, w_ref[:, group * r:(group + 1) * r])

    cos = cos_ref[...]
    sin = sin_ref[...]
    for hd, qh in enumerate(_rope(proj(0), cos, sin, 1.0)):
        q_ref[0, :, hd * HEAD_DIM:(hd + 1) * HEAD_DIM] = qh.astype(q_ref.dtype)
    ks = _rope(proj(1), cos, sin, HEAD_DIM ** -0.5)
    vb = proj(2).astype(BF16)
    v_ref[0] = vb
    tile_dec = jnp.exp(_log_sigmoid(decb_ref[:, :HEAD_DIM]) * float(tm))
    for hd in range(RET_HEADS):
        kt = ks[hd].T
        kt_ref[0, hd] = kt.astype(kt_ref.dtype)
        ktd = (kt * dec_scr[hd:hd + 1, :]).astype(BF16)
        s_old = s_scr[hd]
        sbound_ref[0, 0, hd] = s_old
        s_scr[hd] = tile_dec[hd:hd + 1, :] * s_old + _dot(ktd, vb[:, hd * HEAD_DIM:(hd + 1) * HEAD_DIM])

    g = proj(3)
    gate_ref[0] = (g * _sigmoid(g)).astype(gate_ref.dtype)

    u = proj(5) * proj(6)
    col = lax.broadcasted_iota(jnp.int32, (tm, 1), 0) % GRID_W
    up = jnp.where(col == 0, 0.0, pltpu.roll(u, 1, 0))
    un = jnp.where(col == GRID_W - 1, 0.0, pltpu.roll(u, tm - 1, 0))
    y = proj(4) * (cw_ref[0:1, :] * up + cw_ref[1:2, :] * u + cw_ref[2:3, :] * un)
    conv_ref[0] = y.astype(conv_ref.dtype)


def _inproj_call(x, mod, norm_g, w_in, cos_t, sin_t, conv_w, dec_b_rows, sb0, w_out):
    b, n, d = x.shape
    tm = INPROJ_TILE
    nt = n // tm
    r = RET_WIDTH
    assert dec_b_rows.shape == (RET_HEADS, tm)
    slab = w_out.shape[0] // (b * nt)
    assert slab * b * nt == w_out.shape[0] and slab % 16 == 0
    slab_spec = pl.BlockSpec((slab, w_out.shape[1]), lambda bi, i: (bi * nt + i, 0))

    def tok(bi, i):
        return (bi, nt - 1 - i, 0)

    seq = jax.ShapeDtypeStruct((b, n, r), BF16)
    seq_spec = pl.BlockSpec((1, tm, r), tok)
    keys_t = jax.ShapeDtypeStruct((b, RET_HEADS, HEAD_DIM, n), BF16)
    keys_t_spec = pl.BlockSpec((1, RET_HEADS, HEAD_DIM, tm), lambda bi, i: (bi, 0, 0, nt - 1 - i))
    return pl.pallas_call(
        _inproj_kernel,
        grid=(b, nt),
        in_specs=[
            pl.BlockSpec((1, tm, d), tok),
            pl.BlockSpec((MOD_ROWS, d), lambda bi, i: (0, 0)),
            pl.BlockSpec((MOD_ROWS, d), lambda bi, i: (0, 1)),
            pl.BlockSpec((1, d), lambda bi, i: (0, 0)),
            pl.BlockSpec(w_in.shape, lambda bi, i: (0, 0), pipeline_mode=pl.Buffered(1)),
            pl.BlockSpec((tm, HEAD_DIM), lambda bi, i: (nt - 1 - i, 0)),
            pl.BlockSpec((tm, HEAD_DIM), lambda bi, i: (nt - 1 - i, 0)),
            pl.BlockSpec((3, CONV_CH), lambda bi, i: (0, 0)),
            pl.BlockSpec((RET_HEADS, tm), lambda bi, i: (0, 0)),
            pl.BlockSpec((1, RET_HEADS, HEAD_DIM, HEAD_DIM), lambda bi, i: (bi, 0, 0, 0)),
            slab_spec,
        ],
        out_specs=[
            seq_spec, keys_t_spec, seq_spec, seq_spec, seq_spec,
            pl.BlockSpec((1, 1, RET_HEADS, HEAD_DIM, HEAD_DIM),
                         lambda bi, i: (bi, nt - 1 - i, 0, 0, 0)),
            slab_spec,
        ],
        out_shape=[seq, keys_t, seq, seq, seq,
                   jax.ShapeDtypeStruct((b, nt, RET_HEADS, HEAD_DIM, HEAD_DIM), F32),
                   jax.ShapeDtypeStruct(w_out.shape, BF16)],
        scratch_shapes=[
            pltpu.VMEM((RET_HEADS, HEAD_DIM, HEAD_DIM), F32),
            pltpu.VMEM((RET_HEADS, tm), F32),
        ],
        compiler_params=pltpu.CompilerParams(
            dimension_semantics=("arbitrary", "arbitrary"),
            vmem_limit_bytes=VMEM_LIMIT),
        name="inproj",
    )(x, mod, mod, norm_g, w_in, cos_t, sin_t, conv_w, dec_b_rows, sb0, w_out)


def _mix_kernel(q_ref, kt_ref, v_ref, gate_ref, conv_ref, x_ref, sbound_ref, sf0_ref,
                wo_r_ref, wo_c_ref, g1_ref, decf_ref, decb_ref, wup_ref, wdn_ref,
                o_ref, wup_b_ref, wdn_b_ref,
                sf_scr, dmat_scr, qdec_scr, kdec_scr, cdec_scr, ret_scr):
    b = pl.program_id(0)
    i = pl.program_id(1)
    tm = ret_scr.shape[0]
    c = dmat_scr.shape[1]
    nc = tm // c

    wup_b_ref[...] = wup_ref[...].astype(wup_b_ref.dtype)
    wdn_b_ref[...] = wdn_ref[...].astype(wdn_b_ref.dtype)

    @pl.when(i == 0)
    def _():
        sf_scr[...] = sf0_ref[0]

    @pl.when((b == 0) & (i == 0))
    def _():
        lgf = _log_sigmoid(decf_ref[...])
        lgb = _log_sigmoid(decb_ref[...])
        ri = lax.broadcasted_iota(jnp.int32, (c, c), 0)
        ci = lax.broadcasted_iota(jnp.int32, (c, c), 1)
        rel = (ri - ci).astype(F32)
        t = lax.broadcasted_iota(jnp.int32, (c, HEAD_DIM), 0).astype(F32)
        tl = lax.broadcasted_iota(jnp.int32, (1, c), 1).astype(F32)
        for hd in range(RET_HEADS):
            f = lgf[hd:hd + 1, :]
            g = lgb[hd:hd + 1, :]
            dmat_scr[hd] = jnp.where(rel >= 0, jnp.exp(f * jnp.maximum(rel, 0.0)),
                                     jnp.exp(g * jnp.maximum(-rel, 0.0)))
            fh = f[:, :HEAD_DIM]
            gh = g[:, :HEAD_DIM]
            qdec_scr[hd, :, :HEAD_DIM] = jnp.exp(fh * (t + 1.0))
            qdec_scr[hd, :, HEAD_DIM:] = jnp.exp(gh * (float(c) - t))
            kdec_scr[hd, 0:1, :] = jnp.exp(f * (float(c) - 1.0 - tl))
            kdec_scr[hd, 1:2, :] = jnp.exp(g * tl)
            cdec_scr[hd, :, :HEAD_DIM] = jnp.broadcast_to(jnp.exp(fh * float(c)), (8, HEAD_DIM))
            cdec_scr[hd, :, HEAD_DIM:] = jnp.broadcast_to(jnp.exp(gh * float(c)), (8, HEAD_DIM))

    heads = range(RET_HEADS)
    hsl = [slice(hd * HEAD_DIM, (hd + 1) * HEAD_DIM) for hd in heads]
    crows = [slice(ch * c, (ch + 1) * c) for ch in range(nc)]
    kts = [[kt_ref[0, hd, :, crows[ch]] for ch in range(nc)] for hd in heads]
    vs = [[v_ref[0, crows[ch], hsl[hd]] for ch in range(nc)] for hd in heads]

    sfs, sbs = [], []
    for hd in heads:
        cf = cdec_scr[hd, 0:1, :HEAD_DIM]
        cb = cdec_scr[hd, 0:1, HEAD_DIM:]
        sb = [None] * nc
        sb[nc - 1] = sbound_ref[0, 0, hd]
        for ch in range(nc - 1, 0, -1):
            kb = (kts[hd][ch].astype(F32) * kdec_scr[hd, 1:2, :]).astype(BF16)
            sb[ch - 1] = cb * sb[ch] + _dot(kb, vs[hd][ch])
        sf = [sf_scr[hd]]
        for ch in range(nc):
            kf = (kts[hd][ch].astype(F32) * kdec_scr[hd, 0:1, :]).astype(BF16)
            sf.append(cf * sf[ch] + _dot(kf, vs[hd][ch]))
        sf_scr[hd] = sf[nc]
        sfs.append(sf)
        sbs.append(sb)

    for ch in range(nc):
        qs = [q_ref[0, crows[ch], hsl[hd]] for hd in heads]
        scores = [_dot(qs[hd], kts[hd][ch]) for hd in heads]
        cross = []
        for hd in heads:
            q2 = (jnp.concatenate([qs[hd], qs[hd]], axis=1).astype(F32) * qdec_scr[hd]).astype(BF16)
            s2 = jnp.concatenate([sfs[hd][ch], sbs[hd][ch]], axis=0).astype(BF16)
            cross.append(_dot(q2, s2))
        ps = [(scores[hd] * dmat_scr[hd]).astype(BF16) for hd in heads]
        for hd in heads:
            o = _dot(ps[hd], vs[hd][ch]) + cross[hd]
            ms = jnp.mean(o * o, axis=-1, keepdims=True)
            o = o * lax.rsqrt(ms + EPS) * gate_ref[0, crows[ch], hsl[hd]].astype(F32)
            ret_scr[crows[ch], hsl[hd]] = o.astype(BF16)
        mix = _dot(ret_scr[crows[ch], :], wo_r_ref[...]) + _dot(conv_ref[0, crows[ch], :], wo_c_ref[...])
        o_ref[0, crows[ch], :] = x_ref[0, crows[ch], :] + g1_ref[pl.ds(b, 1), :] * mix


def _mix_call(q, kt, v, gate, conv, x, sbound, sf0, w_out, mod, dec_f_rows, dec_b_rows, w_up,
              w_down):
    b, n, d = x.shape
    tm = MIX_TILE
    nt = n // tm
    bound_per_tile = sbound.shape[1] // nt
    r = RET_WIDTH
    c = RET_CHUNK

    def slab_spec_of(w):
        slab = w.shape[0] // (b * nt)
        assert slab * b * nt == w.shape[0] and slab % 16 == 0
        return pl.BlockSpec((slab, w.shape[1]), lambda bi, i: (bi * nt + i, 0))

    def tok(bi, i):
        return (bi, i, 0)

    seq_spec = pl.BlockSpec((1, tm, r), tok)
    return pl.pallas_call(
        _mix_kernel,
        grid=(b, nt),
        in_specs=[
            seq_spec,
            pl.BlockSpec((1, RET_HEADS, HEAD_DIM, tm), lambda bi, i: (bi, 0, 0, i)),
            seq_spec, seq_spec, seq_spec,
            pl.BlockSpec((1, tm, d), tok),
            pl.BlockSpec((1, 1, RET_HEADS, HEAD_DIM, HEAD_DIM),
                         lambda bi, i: (bi, (i + 1) * bound_per_tile - 1, 0, 0, 0)),
            pl.BlockSpec((1, RET_HEADS, HEAD_DIM, HEAD_DIM), lambda bi, i: (bi, 0, 0, 0)),
            pl.BlockSpec((r, d), lambda bi, i: (0, 0)),
            pl.BlockSpec((CONV_CH, d), lambda bi, i: (1, 0)),
            pl.BlockSpec((MOD_ROWS, d), lambda bi, i: (0, 2)),
            pl.BlockSpec((RET_HEADS, c), lambda bi, i: (0, 0)),
            pl.BlockSpec((RET_HEADS, c), lambda bi, i: (0, 0)),
            slab_spec_of(w_up),
            slab_spec_of(w_down),
        ],
        out_specs=[pl.BlockSpec((1, tm, d), tok), slab_spec_of(w_up), slab_spec_of(w_down)],
        out_shape=[jax.ShapeDtypeStruct((b, n, d), F32),
                   jax.ShapeDtypeStruct(w_up.shape, BF16),
                   jax.ShapeDtypeStruct(w_down.shape, BF16)],
        scratch_shapes=[
            pltpu.VMEM((RET_HEADS, HEAD_DIM, HEAD_DIM), F32),
            pltpu.VMEM((RET_HEADS, c, c), F32),
            pltpu.VMEM((RET_HEADS, c, 2 * HEAD_DIM), F32),
            pltpu.VMEM((RET_HEADS, 8, c), F32),
            pltpu.VMEM((RET_HEADS, 8, 2 * HEAD_DIM), F32),
            pltpu.VMEM((tm, r), BF16),
        ],
        compiler_params=pltpu.CompilerParams(
            dimension_semantics=("arbitrary", "arbitrary"), vmem_limit_bytes=VMEM_LIMIT),
        name="mix",
    )(q, kt, v, gate, conv, x, sbound, sf0, w_out, w_out, mod, dec_f_rows, dec_b_rows, w_up,
      w_down)


def _ffn_kernel(x_ref, sh_ref, sc_ref, gt_ref, g2_ref, gf_ref,
                wa_ref, wb_ref, cw_ref, cb_ref, wd_ref,
                o_ref,
                h_scr):
    b = pl.program_id(0)
    f = pl.program_id(2)
    nf = pl.num_programs(2)
    tm, d = h_scr.shape
    tf = wa_ref.shape[1]
    blk = o_ref.shape[1:]
    step = blk[1]

    def body(first, last):
        if first:
            h = _norm_mod(x_ref[0].reshape(tm, d), g2_ref[...], sh_ref[pl.ds(b, 1), :],
                          sc_ref[pl.ds(b, 1), :]).astype(BF16)
            h_scr[...] = h
        else:
            h = h_scr[...]
        a = _dot(h, wa_ref[...])
        zero = jnp.zeros((step, tf), F32)
        a_up = jnp.concatenate([zero, a[:tm - step]], axis=0)
        a_dn = jnp.concatenate([a[step:], zero], axis=0)
        a = cw_ref[0:1, :] * a_up + cw_ref[1:2, :] * a + cw_ref[2:3, :] * a_dn + cb_ref[...]
        act = (a * _sigmoid(a) * _dot(h, wb_ref[...])).astype(BF16)
        if not last:
            y = _dot(act, wd_ref[...])
            o_ref[0] = (y if first else o_ref[0].reshape(tm, d) + y).reshape(blk)
            return
        groups = LAST_STEP_ROW_GROUPS
        gr = blk[0] // groups
        ys = [_dot(act[g * gr * step:(g + 1) * gr * step], wd_ref[...]) for g in range(groups)]
        for g in range(groups):
            rs = slice(g * gr, (g + 1) * gr)
            acc = o_ref[0, rs].reshape(gr * step, d) + ys[g]
            z = x_ref[0, rs].reshape(gr * step, d) + gt_ref[pl.ds(b, 1), :] * acc
            ms = jnp.mean(z * z, axis=-1, keepdims=True)
            o_ref[0, rs] = (z * lax.rsqrt(ms + EPS) * gf_ref[...]).reshape(gr, step, d)

    @pl.when(f == 0)
    def _():
        body(True, False)

    @pl.when((f > 0) & (f < nf - 1))
    def _():
        body(False, False)

    @pl.when(f == nf - 1)
    def _():
        body(False, True)


def _ffn_call(x, mod, norm_g, final_g, w_up, ffn_conv_w, ffn_conv_b, w_down):
    b, n, d = x.shape
    d_ff = w_down.shape[0]
    rows = n // GRID_W
    cols = FFN_COLS
    tf = FF_TILE
    nf = d_ff // tf
    assert nf >= 2
    xg = x.reshape(b, rows, GRID_W, d)
    tile = pl.BlockSpec((1, rows, cols, d), lambda bi, i, f: (bi, 0, i, 0))

    out = pl.pallas_call(
        _ffn_kernel,
        grid=(b, GRID_W // cols, nf),
        in_specs=[
            tile,
            pl.BlockSpec((MOD_ROWS, d), lambda bi, i, f: (0, 3)),
            pl.BlockSpec((MOD_ROWS, d), lambda bi, i, f: (0, 4)),
            pl.BlockSpec((MOD_ROWS, d), lambda bi, i, f: (0, 5)),
            pl.BlockSpec((1, d), lambda bi, i, f: (0, 0)),
            pl.BlockSpec((1, d), lambda bi, i, f: (0, 0)),
            pl.BlockSpec((d, tf), lambda bi, i, f: (0, f)),
            pl.BlockSpec((d, tf), lambda bi, i, f: (0, nf + f)),
            pl.BlockSpec((3, tf), lambda bi, i, f: (0, f)),
            pl.BlockSpec((1, tf), lambda bi, i, f: (0, f)),
            pl.BlockSpec((tf, d), lambda bi, i, f: (f, 0)),
        ],
        out_specs=tile,
        out_shape=jax.ShapeDtypeStruct((b, rows, GRID_W, d), F32),
        scratch_shapes=[
            pltpu.VMEM((rows * cols, d), BF16),
        ],
        compiler_params=pltpu.CompilerParams(
            dimension_semantics=("arbitrary", "arbitrary", "arbitrary"),
            vmem_limit_bytes=VMEM_LIMIT),
        name="ffn",
    )(xg, mod, mod, mod, norm_g, final_g, w_up, w_up, ffn_conv_w, ffn_conv_b, w_down)
    return out.reshape(b, n, d)


def _rope_tables(n):
    pos = np.arange(n, dtype=np.int32)
    row = (pos // GRID_W).astype(np.float32)
    col = (pos % GRID_W).astype(np.float32)
    dh = HEAD_DIM // 2
    freqs = np.float32(ROPE_THETA) ** (-np.arange(0, dh, 2, dtype=np.float32) / np.float32(dh))
    parts_c, parts_s = [], []
    for p in (row, col):
        ang = (p[:, None] * freqs[None, :]).astype(np.float32)
        cos = np.cos(ang).astype(np.float32)
        sin = np.sin(ang).astype(np.float32)
        parts_c += [cos, cos]
        parts_s += [-sin, sin]
    return (jnp.asarray(np.concatenate(parts_c, axis=-1)),
            jnp.asarray(np.concatenate(parts_s, axis=-1)))


def kernel(x, c, ctx, c_ctx, w_mod, b_mod, norm1_g, w_in, ret_decay_fwd, ret_decay_bwd,
           conv_w, w_out, norm2_g, w_up, ffn_conv_w, ffn_conv_b, w_down, final_g):
    b, n, d = x.shape
    depth = w_mod.shape[0]
    assert depth == 1 and n % MIX_TILE == 0 and n % GRID_W == 0
    assert MIX_TILE % INPROJ_TILE == 0 and MIX_TILE % RET_CHUNK == 0
    cos_t, sin_t = _rope_tables(n)
    craw = jnp.concatenate(
        [c, c_ctx[None, :], jnp.zeros((MOD_ROWS - b - 1, d), F32)], axis=0)

    layer = 0
    mod, w_in_b = _mod_call(craw, w_mod[layer], b_mod[layer][None, :], w_in[layer])
    dec_f = jnp.repeat(ret_decay_fwd[layer], HEAD_DIM)[None, :]
    dec_b = jnp.repeat(ret_decay_bwd[layer], HEAD_DIM)[None, :]
    dec_f_rows = jnp.broadcast_to(ret_decay_fwd[layer][:, None], (RET_HEADS, RET_CHUNK))
    dec_b_rows = jnp.broadcast_to(ret_decay_bwd[layer][:, None], (RET_HEADS, RET_CHUNK))
    g1 = norm1_g[layer][None, :]

    sf0, sb0 = _ctx_call(ctx, mod, g1, w_in_b, dec_f, dec_b)
    dec_b_tile = jnp.broadcast_to(ret_decay_bwd[layer][:, None], (RET_HEADS, INPROJ_TILE))
    q, kt, v, gate, conv, sbound, w_out_b = _inproj_call(
        x, mod, g1, w_in_b, cos_t, sin_t, conv_w[layer], dec_b_tile, sb0, w_out[layer])
    x_mid, w_up_b, w_down_b = _mix_call(q, kt, v, gate, conv, x, sbound, sf0, w_out_b, mod,
                                        dec_f_rows, dec_b_rows, w_up[layer], w_down[layer])
    return _ffn_call(x_mid, mod, norm2_g[layer][None, :], final_g[None, :], w_up_b,
                     ffn_conv_w[layer], ffn_conv_b[layer][None, :], w_down_b)
```

```python
import functools

import jax
import jax.numpy as jnp
import numpy as np
from jax import lax
from jax.experimental import pallas as pl
from jax.experimental.pallas import tpu as pltpu

F32 = jnp.float32
BF16 = jnp.bfloat16

GRID_W = 64
RET_HEADS = 8
HEAD_DIM = 128
RET_WIDTH = RET_HEADS * HEAD_DIM
CONV_CH = 1024
N_MOD = 6
EPS = 1e-6
ROPE_THETA = 10000.0

MOD_ROWS = 8
INPROJ_TILE = 512
MIX_TILE = 512
FFN_COLS = 8
RET_CHUNK = 256
FF_TILE = 512
MOD_COL_TILE = 768
LAST_STEP_ROW_GROUPS = 4
VMEM_LIMIT = 60 * 1024 * 1024


def _sigmoid(x):
    return 1.0 / (1.0 + jnp.exp(-x))


def _log_sigmoid(x):
    return jnp.minimum(x, 0.0) - jnp.log(1.0 + jnp.exp(-jnp.abs(x)))


def _norm_mod(x, gain, shift, scale):
    ms = jnp.mean(x * x, axis=-1, keepdims=True)
    y = x * lax.rsqrt(ms + EPS) * gain
    return y * (1.0 + scale) + shift


def _dot(a, b):
    return jnp.dot(a, b, preferred_element_type=F32)


def _dot_t0(a, b):
    return lax.dot_general(a, b, (((0,), (0,)), ((), ())), preferred_element_type=F32)


def _dot_nt(a, b):
    return lax.dot_general(a, b, (((1,), (1,)), ((), ())), preferred_element_type=F32)


def _mod_kernel(c_ref, w_ref, b_ref, win_ref, o_ref, win_b_ref):
    a = c_ref[...]
    a = (a * _sigmoid(a)).astype(BF16)
    o_ref[...] = _dot(a, w_ref[...].astype(BF16)) + b_ref[...]
    win_b_ref[...] = win_ref[...].astype(win_b_ref.dtype)


def _mod_call(craw, w_mod, b_mod, w_in):
    d, n = w_mod.shape
    tn = MOD_COL_TILE
    steps = n // tn
    slab = w_in.shape[0] // steps
    assert steps * tn == n and slab * steps == w_in.shape[0] and slab % 16 == 0
    slab_spec = pl.BlockSpec((slab, w_in.shape[1]), lambda j: (j, 0))
    return pl.pallas_call(
        _mod_kernel,
        grid=(steps,),
        in_specs=[
            pl.BlockSpec((MOD_ROWS, d), lambda j: (0, 0)),
            pl.BlockSpec((d, tn), lambda j: (0, j)),
            pl.BlockSpec((1, tn), lambda j: (0, j)),
            slab_spec,
        ],
        out_specs=[pl.BlockSpec((MOD_ROWS, tn), lambda j: (0, j)), slab_spec],
        out_shape=[jax.ShapeDtypeStruct((MOD_ROWS, n), F32),
                   jax.ShapeDtypeStruct(w_in.shape, BF16)],
        compiler_params=pltpu.CompilerParams(
            dimension_semantics=("arbitrary",), vmem_limit_bytes=VMEM_LIMIT),
        name="mod",
    )(craw, w_mod, b_mod, w_in)


def _ctx_kernel(ctx_row, x_ref, sh_ref, sc_ref, g_ref, wk_ref, wv_ref, decf_ref, decb_ref,
                sf_ref, sb_ref):
    x = x_ref[0]
    length = x.shape[0]
    h = _norm_mod(x, g_ref[...], sh_ref[ctx_row:ctx_row + 1, :],
                  sc_ref[ctx_row:ctx_row + 1, :]).astype(BF16)
    k = _dot(h, wk_ref[...]) * (HEAD_DIM ** -0.5)
    v = _dot(h, wv_ref[...]).astype(BF16)
    t = lax.broadcasted_iota(jnp.int32, k.shape, 0).astype(F32)
    wf = jnp.exp(_log_sigmoid(decf_ref[...]) * (length - 1.0 - t))
    wb = jnp.exp(_log_sigmoid(decb_ref[...]) * t)
    kf = (k * wf).astype(BF16)
    kb = (k * wb).astype(BF16)
    for hd in range(RET_HEADS):
        sl = slice(hd * HEAD_DIM, (hd + 1) * HEAD_DIM)
        sf_ref[0, hd] = _dot_t0(v[:, sl], kf[:, sl])
        sb_ref[0, hd] = _dot_t0(v[:, sl], kb[:, sl])


def _ctx_call(ctx, mod, norm_g, w_in, dec_f, dec_b):
    b, length, d = ctx.shape
    r = RET_WIDTH
    state = jax.ShapeDtypeStruct((b, RET_HEADS, HEAD_DIM, HEAD_DIM), F32)
    state_spec = pl.BlockSpec((1, RET_HEADS, HEAD_DIM, HEAD_DIM), lambda i: (i, 0, 0, 0))
    return pl.pallas_call(
        functools.partial(_ctx_kernel, b),
        grid=(b,),
        in_specs=[
            pl.BlockSpec((1, length, d), lambda i: (i, 0, 0)),
            pl.BlockSpec((MOD_ROWS, d), lambda i: (0, 0)),
            pl.BlockSpec((MOD_ROWS, d), lambda i: (0, 1)),
            pl.BlockSpec((1, d), lambda i: (0, 0)),
            pl.BlockSpec((d, r), lambda i: (0, 1)),
            pl.BlockSpec((d, r), lambda i: (0, 2)),
            pl.BlockSpec((1, r), lambda i: (0, 0)),
            pl.BlockSpec((1, r), lambda i: (0, 0)),
        ],
        out_specs=[state_spec, state_spec],
        out_shape=[state, state],
        compiler_params=pltpu.CompilerParams(
            dimension_semantics=("arbitrary",), vmem_limit_bytes=VMEM_LIMIT),
        name="ctx",
    )(ctx, mod, mod, norm_g, w_in, w_in, dec_f, dec_b)


def _rope(acc, cos, sin, scale):
    lane = lax.broadcasted_iota(jnp.int32, (1, HEAD_DIM), 1)
    lo = (lane % 64) < 32
    outs = []
    for hd in range(RET_HEADS):
        xh = acc[:, hd * HEAD_DIM:(hd + 1) * HEAD_DIM]
        partner = jnp.where(lo, pltpu.roll(xh, 96, 1), pltpu.roll(xh, 32, 1))
        y = xh * cos + partner * sin
        outs.append(y if scale == 1.0 else y * scale)
    return outs


def _inproj_kernel(x_ref, sh_ref, sc_ref, g_ref, w_ref, cos_ref, sin_ref, cw_ref, decb_ref,
                   sb0_ref, wout_ref,
                   q_ref, k_ref, vt_ref, gate_ref, conv_ref, sbound_ref, wout_b_ref,
                   s_scr, dec_scr):
    b = pl.program_id(0)
    i = pl.program_id(1)
    tm = dec_scr.shape[1]
    r = RET_WIDTH

    wout_b_ref[...] = wout_ref[...].astype(wout_b_ref.dtype)

    @pl.when(i == 0)
    def _():
        s_scr[...] = sb0_ref[0]
        t = lax.broadcasted_iota(jnp.int32, dec_scr.shape, 1).astype(F32)
        dec_scr[...] = jnp.exp(_log_sigmoid(decb_ref[...]) * t)

    h = _norm_mod(x_ref[0], g_ref[...], sh_ref[pl.ds(b, 1), :], sc_ref[pl.ds(b, 1), :]).astype(BF16)

    def proj(group):
        return _dot(h, w_ref[:, group * r:(group + 1) * r])

    cos = cos_ref[...]
    sin = sin_ref[...]
    for hd, qh in enumerate(_rope(proj(0), cos, sin, 1.0)):
        q_ref[0, :, hd * HEAD_DIM:(hd + 1) * HEAD_DIM] = qh.astype(q_ref.dtype)
    ks = [kh.astype(BF16) for kh in _rope(proj(1), cos, sin, HEAD_DIM ** -0.5)]
    for hd in range(RET_HEADS):
        k_ref[0, :, hd * HEAD_DIM:(hd + 1) * HEAD_DIM] = ks[hd]
    v = proj(2)
    tile_dec = jnp.exp(_log_sigmoid(decb_ref[:, :HEAD_DIM]) * float(tm))
    for hd in range(RET_HEADS):
        vt = v[:, hd * HEAD_DIM:(hd + 1) * HEAD_DIM].T
        vt_ref[0, hd] = vt.astype(vt_ref.dtype)
        vtd = (vt * dec_scr[hd:hd + 1, :]).astype(BF16)
        s_old = s_scr[hd]
        sbound_ref[0, 0, hd] = s_old
        s_scr[hd] = tile_dec[hd:hd + 1, :] * s_old + _dot(vtd, ks[hd])

    g = proj(3)
    gate_ref[0] = (g * _sigmoid(g)).astype(gate_ref.dtype)

    u = proj(5) * proj(6)
    col = lax.broadcasted_iota(jnp.int32, (tm, 1), 0) % GRID_W
    up = jnp.where(col == 0, 0.0, pltpu.roll(u, 1, 0))
    un = jnp.where(col == GRID_W - 1, 0.0, pltpu.roll(u, tm - 1, 0))
    y = proj(4) * (cw_ref[0:1, :] * up + cw_ref[1:2, :] * u + cw_ref[2:3, :] * un)
    conv_ref[0] = y.astype(conv_ref.dtype)


def _inproj_call(x, mod, norm_g, w_in, cos_t, sin_t, conv_w, dec_b_rows, sb0, w_out):
    b, n, d = x.shape
    tm = INPROJ_TILE
    nt = n // tm
    r = RET_WIDTH
    assert dec_b_rows.shape == (RET_HEADS, tm)
    slab = w_out.shape[0] // (b * nt)
    assert slab * b * nt == w_out.shape[0] and slab % 16 == 0
    slab_spec = pl.BlockSpec((slab, w_out.shape[1]), lambda bi, i: (bi * nt + i, 0))

    def tok(bi, i):
        return (bi, nt - 1 - i, 0)

    seq = jax.ShapeDtypeStruct((b, n, r), BF16)
    seq_spec = pl.BlockSpec((1, tm, r), tok)
    heads_t = jax.ShapeDtypeStruct((b, RET_HEADS, HEAD_DIM, n), BF16)
    heads_t_spec = pl.BlockSpec((1, RET_HEADS, HEAD_DIM, tm), lambda bi, i: (bi, 0, 0, nt - 1 - i))
    return pl.pallas_call(
        _inproj_kernel,
        grid=(b, nt),
        in_specs=[
            pl.BlockSpec((1, tm, d), tok),
            pl.BlockSpec((MOD_ROWS, d), lambda bi, i: (0, 0)),
            pl.BlockSpec((MOD_ROWS, d), lambda bi, i: (0, 1)),
            pl.BlockSpec((1, d), lambda bi, i: (0, 0)),
            pl.BlockSpec(w_in.shape, lambda bi, i: (0, 0), pipeline_mode=pl.Buffered(1)),
            pl.BlockSpec((tm, HEAD_DIM), lambda bi, i: (nt - 1 - i, 0)),
            pl.BlockSpec((tm, HEAD_DIM), lambda bi, i: (nt - 1 - i, 0)),
            pl.BlockSpec((3, CONV_CH), lambda bi, i: (0, 0)),
            pl.BlockSpec((RET_HEADS, tm), lambda bi, i: (0, 0)),
            pl.BlockSpec((1, RET_HEADS, HEAD_DIM, HEAD_DIM), lambda bi, i: (bi, 0, 0, 0)),
            slab_spec,
        ],
        out_specs=[
            seq_spec, seq_spec, heads_t_spec, seq_spec, seq_spec,
            pl.BlockSpec((1, 1, RET_HEADS, HEAD_DIM, HEAD_DIM),
                         lambda bi, i: (bi, nt - 1 - i, 0, 0, 0)),
            slab_spec,
        ],
        out_shape=[seq, seq, heads_t, seq, seq,
                   jax.ShapeDtypeStruct((b, nt, RET_HEADS, HEAD_DIM, HEAD_DIM), F32),
                   jax.ShapeDtypeStruct(w_out.shape, BF16)],
        scratch_shapes=[
            pltpu.VMEM((RET_HEADS, HEAD_DIM, HEAD_DIM), F32),
            pltpu.VMEM((RET_HEADS, tm), F32),
        ],
        compiler_params=pltpu.CompilerParams(
            dimension_semantics=("arbitrary", "arbitrary"),
            vmem_limit_bytes=VMEM_LIMIT),
        name="inproj",
    )(x, mod, mod, norm_g, w_in, cos_t, sin_t, conv_w, dec_b_rows, sb0, w_out)


def _mix_kernel(q_ref, k_ref, vt_ref, gate_ref, conv_ref, x_ref, sbound_ref, sf0_ref,
                wo_r_ref, wo_c_ref, g1_ref, decf_ref, decb_ref, wup_ref, wdn_ref,
                o_ref, wup_b_ref, wdn_b_ref,
                sf_scr, dmat_scr, qdec_scr, kdec_scr, cdec_scr, ret_scr):
    b = pl.program_id(0)
    i = pl.program_id(1)
    tm = ret_scr.shape[0]
    c = dmat_scr.shape[1]
    nc = tm // c

    wup_b_ref[...] = wup_ref[...].astype(wup_b_ref.dtype)
    wdn_b_ref[...] = wdn_ref[...].astype(wdn_b_ref.dtype)

    @pl.when(i == 0)
    def _():
        sf_scr[...] = sf0_ref[0]

    @pl.when((b == 0) & (i == 0))
    def _():
        lgf = _log_sigmoid(decf_ref[...])
        lgb = _log_sigmoid(decb_ref[...])
        ri = lax.broadcasted_iota(jnp.int32, (c, c), 0)
        ci = lax.broadcasted_iota(jnp.int32, (c, c), 1)
        rel = (ri - ci).astype(F32)
        t = lax.broadcasted_iota(jnp.int32, (c, HEAD_DIM), 0).astype(F32)
        tl = lax.broadcasted_iota(jnp.int32, (1, c), 1).astype(F32)
        for hd in range(RET_HEADS):
            f = lgf[hd:hd + 1, :]
            g = lgb[hd:hd + 1, :]
            dmat_scr[hd] = jnp.where(rel >= 0, jnp.exp(f * jnp.maximum(rel, 0.0)),
                                     jnp.exp(g * jnp.maximum(-rel, 0.0)))
            fh = f[:, :HEAD_DIM]
            gh = g[:, :HEAD_DIM]
            qdec_scr[hd, :, :HEAD_DIM] = jnp.exp(fh * (t + 1.0))
            qdec_scr[hd, :, HEAD_DIM:] = jnp.exp(gh * (float(c) - t))
            kdec_scr[hd, 0:1, :] = jnp.exp(f * (float(c) - 1.0 - tl))
            kdec_scr[hd, 1:2, :] = jnp.exp(g * tl)
            cdec_scr[hd, :, :HEAD_DIM] = jnp.broadcast_to(jnp.exp(fh * float(c)), (8, HEAD_DIM))
            cdec_scr[hd, :, HEAD_DIM:] = jnp.broadcast_to(jnp.exp(gh * float(c)), (8, HEAD_DIM))

    heads = range(RET_HEADS)
    hsl = [slice(hd * HEAD_DIM, (hd + 1) * HEAD_DIM) for hd in heads]
    crows = [slice(ch * c, (ch + 1) * c) for ch in range(nc)]
    ks = [[k_ref[0, crows[ch], hsl[hd]] for ch in range(nc)] for hd in heads]
    vts = [[vt_ref[0, hd, :, crows[ch]] for ch in range(nc)] for hd in heads]

    qs = [[q_ref[0, crows[ch], hsl[hd]] for ch in range(nc)] for hd in heads]

    def masked_scores(ch):
        return [(_dot_nt(qs[hd][ch], ks[hd][ch]) * dmat_scr[hd]).astype(BF16) for hd in heads]

    ps_first = masked_scores(0)

    s2ts = []
    for hd in heads:
        cf = cdec_scr[hd, 0:1, :HEAD_DIM]
        cb = cdec_scr[hd, 0:1, HEAD_DIM:]
        sb = [None] * nc
        sb[nc - 1] = sbound_ref[0, 0, hd]
        for ch in range(nc - 1, 0, -1):
            vb = (vts[hd][ch].astype(F32) * kdec_scr[hd, 1:2, :]).astype(BF16)
            sb[ch - 1] = cb * sb[ch] + _dot(vb, ks[hd][ch])
        sf = [sf_scr[hd]]
        for ch in range(nc):
            vf = (vts[hd][ch].astype(F32) * kdec_scr[hd, 0:1, :]).astype(BF16)
            sf.append(cf * sf[ch] + _dot(vf, ks[hd][ch]))
        sf_scr[hd] = sf[nc]
        s2ts.append([jnp.concatenate([sf[ch], sb[ch]], axis=1).astype(BF16) for ch in range(nc)])

    for ch in range(nc):
        ps = ps_first if ch == 0 else masked_scores(ch)
        cross = []
        for hd in heads:
            qh = qs[hd][ch]
            q2 = (jnp.concatenate([qh, qh], axis=1).astype(F32) * qdec_scr[hd]).astype(BF16)
            cross.append(_dot_nt(s2ts[hd][ch], q2))
        for hd in heads:
            o = (_dot_nt(vts[hd][ch], ps[hd]) + cross[hd]).T
            ms = jnp.mean(o * o, axis=-1, keepdims=True)
            o = o * lax.rsqrt(ms + EPS) * gate_ref[0, crows[ch], hsl[hd]].astype(F32)
            ret_scr[crows[ch], hsl[hd]] = o.astype(BF16)
        mix = _dot(ret_scr[crows[ch], :], wo_r_ref[...]) + _dot(conv_ref[0, crows[ch], :], wo_c_ref[...])
        o_ref[0, crows[ch], :] = x_ref[0, crows[ch], :] + g1_ref[pl.ds(b, 1), :] * mix


def _mix_call(q, k, vt, gate, conv, x, sbound, sf0, w_out, mod, dec_f_rows, dec_b_rows, w_up,
              w_down):
    b, n, d = x.shape
    tm = MIX_TILE
    nt = n // tm
    bound_per_tile = sbound.shape[1] // nt
    r = RET_WIDTH
    c = RET_CHUNK

    def slab_spec_of(w):
        slab = w.shape[0] // (b * nt)
        assert slab * b * nt == w.shape[0] and slab % 16 == 0
        return pl.BlockSpec((slab, w.shape[1]), lambda bi, i: (bi * nt + i, 0))

    def tok(bi, i):
        return (bi, i, 0)

    seq_spec = pl.BlockSpec((1, tm, r), tok)
    return pl.pallas_call(
        _mix_kernel,
        grid=(b, nt),
        in_specs=[
            seq_spec, seq_spec,
            pl.BlockSpec((1, RET_HEADS, HEAD_DIM, tm), lambda bi, i: (bi, 0, 0, i)),
            seq_spec, seq_spec,
            pl.BlockSpec((1, tm, d), tok),
            pl.BlockSpec((1, 1, RET_HEADS, HEAD_DIM, HEAD_DIM),
                         lambda bi, i: (bi, (i + 1) * bound_per_tile - 1, 0, 0, 0)),
            pl.BlockSpec((1, RET_HEADS, HEAD_DIM, HEAD_DIM), lambda bi, i: (bi, 0, 0, 0)),
            pl.BlockSpec((r, d), lambda bi, i: (0, 0)),
            pl.BlockSpec((CONV_CH, d), lambda bi, i: (1, 0)),
            pl.BlockSpec((MOD_ROWS, d), lambda bi, i: (0, 2)),
            pl.BlockSpec((RET_HEADS, c), lambda bi, i: (0, 0)),
            pl.BlockSpec((RET_HEADS, c), lambda bi, i: (0, 0)),
            slab_spec_of(w_up),
            slab_spec_of(w_down),
        ],
        out_specs=[pl.BlockSpec((1, tm, d), tok), slab_spec_of(w_up), slab_spec_of(w_down)],
        out_shape=[jax.ShapeDtypeStruct((b, n, d), F32),
                   jax.ShapeDtypeStruct(w_up.shape, BF16),
                   jax.ShapeDtypeStruct(w_down.shape, BF16)],
        scratch_shapes=[
            pltpu.VMEM((RET_HEADS, HEAD_DIM, HEAD_DIM), F32),
            pltpu.VMEM((RET_HEADS, c, c), F32),
            pltpu.VMEM((RET_HEADS, c, 2 * HEAD_DIM), F32),
            pltpu.VMEM((RET_HEADS, 8, c), F32),
            pltpu.VMEM((RET_HEADS, 8, 2 * HEAD_DIM), F32),
            pltpu.VMEM((tm, r), BF16),
        ],
        compiler_params=pltpu.CompilerParams(
            dimension_semantics=("arbitrary", "arbitrary"), vmem_limit_bytes=VMEM_LIMIT),
        name="mix",
    )(q, k, vt, gate, conv, x, sbound, sf0, w_out, w_out, mod, dec_f_rows, dec_b_rows, w_up,
      w_down)


def _ffn_kernel(x_ref, sh_ref, sc_ref, gt_ref, g2_ref, gf_ref,
                wa_ref, wb_ref, cw_ref, cb_ref, wd_ref,
                o_ref,
                h_scr):
    b = pl.program_id(0)
    f = pl.program_id(2)
    nf = pl.num_programs(2)
    tm, d = h_scr.shape
    tf = wa_ref.shape[1]
    blk = o_ref.shape[1:]
    step = blk[1]

    def body(first, last):
        if first:
            h = _norm_mod(x_ref[0].reshape(tm, d), g2_ref[...], sh_ref[pl.ds(b, 1), :],
                          sc_ref[pl.ds(b, 1), :]).astype(BF16)
            h_scr[...] = h
        else:
            h = h_scr[...]
        a = _dot(h, wa_ref[...])
        zero = jnp.zeros((step, tf), F32)
        a_up = jnp.concatenate([zero, a[:tm - step]], axis=0)
        a_dn = jnp.concatenate([a[step:], zero], axis=0)
        a = cw_ref[0:1, :] * a_up + cw_ref[1:2, :] * a + cw_ref[2:3, :] * a_dn + cb_ref[...]
        act = (a * _sigmoid(a) * _dot(h, wb_ref[...])).astype(BF16)
        if not last:
            y = _dot(act, wd_ref[...])
            o_ref[0] = (y if first else o_ref[0].reshape(tm, d) + y).reshape(blk)
            return
        groups = LAST_STEP_ROW_GROUPS
        gr = blk[0] // groups
        ys = [_dot(act[g * gr * step:(g + 1) * gr * step], wd_ref[...]) for g in range(groups)]
        for g in range(groups):
            rs = slice(g * gr, (g + 1) * gr)
            acc = o_ref[0, rs].reshape(gr * step, d) + ys[g]
            z = x_ref[0, rs].reshape(gr * step, d) + gt_ref[pl.ds(b, 1), :] * acc
            ms = jnp.mean(z * z, axis=-1, keepdims=True)
            o_ref[0, rs] = (z * lax.rsqrt(ms + EPS) * gf_ref[...]).reshape(gr, step, d)

    @pl.when(f == 0)
    def _():
        body(True, False)

    @pl.when((f > 0) & (f < nf - 1))
    def _():
        body(False, False)

    @pl.when(f == nf - 1)
    def _():
        body(False, True)


def _ffn_call(x, mod, norm_g, final_g, w_up, ffn_conv_w, ffn_conv_b, w_down):
    b, n, d = x.shape
    d_ff = w_down.shape[0]
    rows = n // GRID_W
    cols = FFN_COLS
    tf = FF_TILE
    nf = d_ff // tf
    assert nf >= 2
    xg = x.reshape(b, rows, GRID_W, d)
    tile = pl.BlockSpec((1, rows, cols, d), lambda bi, i, f: (bi, 0, i, 0))

    out = pl.pallas_call(
        _ffn_kernel,
        grid=(b, GRID_W // cols, nf),
        in_specs=[
            tile,
            pl.BlockSpec((MOD_ROWS, d), lambda bi, i, f: (0, 3)),
            pl.BlockSpec((MOD_ROWS, d), lambda bi, i, f: (0, 4)),
            pl.BlockSpec((MOD_ROWS, d), lambda bi, i, f: (0, 5)),
            pl.BlockSpec((1, d), lambda bi, i, f: (0, 0)),
            pl.BlockSpec((1, d), lambda bi, i, f: (0, 0)),
            pl.BlockSpec((d, tf), lambda bi, i, f: (0, f)),
            pl.BlockSpec((d, tf), lambda bi, i, f: (0, nf + f)),
            pl.BlockSpec((3, tf), lambda bi, i, f: (0, f)),
            pl.BlockSpec((1, tf), lambda bi, i, f: (0, f)),
            pl.BlockSpec((tf, d), lambda bi, i, f: (f, 0)),
        ],
        out_specs=tile,
        out_shape=jax.ShapeDtypeStruct((b, rows, GRID_W, d), F32),
        scratch_shapes=[
            pltpu.VMEM((rows * cols, d), BF16),
        ],
        compiler_params=pltpu.CompilerParams(
            dimension_semantics=("arbitrary", "arbitrary", "arbitrary"),
            vmem_limit_bytes=VMEM_LIMIT),
        name="ffn",
    )(xg, mod, mod, mod, norm_g, final_g, w_up, w_up, ffn_conv_w, ffn_conv_b, w_down)
    return out.reshape(b, n, d)


def _rope_tables(n):
    pos = np.arange(n, dtype=np.int32)
    row = (pos // GRID_W).astype(np.float32)
    col = (pos % GRID_W).astype(np.float32)
    dh = HEAD_DIM // 2
    freqs = np.float32(ROPE_THETA) ** (-np.arange(0, dh, 2, dtype=np.float32) / np.float32(dh))
    parts_c, parts_s = [], []
    for p in (row, col):
        ang = (p[:, None] * freqs[None, :]).astype(np.float32)
        cos = np.cos(ang).astype(np.float32)
        sin = np.sin(ang).astype(np.float32)
        parts_c += [cos, cos]
        parts_s += [-sin, sin]
    return (jnp.asarray(np.concatenate(parts_c, axis=-1)),
            jnp.asarray(np.concatenate(parts_s, axis=-1)))


def kernel(x, c, ctx, c_ctx, w_mod, b_mod, norm1_g, w_in, ret_decay_fwd, ret_decay_bwd,
           conv_w, w_out, norm2_g, w_up, ffn_conv_w, ffn_conv_b, w_down, final_g):
    b, n, d = x.shape
    depth = w_mod.shape[0]
    assert depth == 1 and n % MIX_TILE == 0 and n % GRID_W == 0
    assert MIX_TILE % INPROJ_TILE == 0 and MIX_TILE % RET_CHUNK == 0
    cos_t, sin_t = _rope_tables(n)
    craw = jnp.concatenate(
        [c, c_ctx[None, :], jnp.zeros((MOD_ROWS - b - 1, d), F32)], axis=0)

    layer = 0
    mod, w_in_b = _mod_call(craw, w_mod[layer], b_mod[layer][None, :], w_in[layer])
    dec_f = jnp.repeat(ret_decay_fwd[layer], HEAD_DIM)[None, :]
    dec_b = jnp.repeat(ret_decay_bwd[layer], HEAD_DIM)[None, :]
    dec_f_rows = jnp.broadcast_to(ret_decay_fwd[layer][:, None], (RET_HEADS, RET_CHUNK))
    dec_b_rows = jnp.broadcast_to(ret_decay_bwd[layer][:, None], (RET_HEADS, RET_CHUNK))
    g1 = norm1_g[layer][None, :]

    sf0, sb0 = _ctx_call(ctx, mod, g1, w_in_b, dec_f, dec_b)
    dec_b_tile = jnp.broadcast_to(ret_decay_bwd[layer][:, None], (RET_HEADS, INPROJ_TILE))
    q, k, vt, gate, conv, sbound, w_out_b = _inproj_call(
        x, mod, g1, w_in_b, cos_t, sin_t, conv_w[layer], dec_b_tile, sb0, w_out[layer])
    x_mid, w_up_b, w_down_b = _mix_call(q, k, vt, gate, conv, x, sbound, sf0, w_out_b, mod,
                                        dec_f_rows, dec_b_rows, w_up[layer], w_down[layer])
    return _ffn_call(x_mid, mod, norm2_g[layer][None, :], final_g[None, :], w_up_b,
                     ffn_conv_w[layer], ffn_conv_b[layer][None, :], w_down_b)
```

```python
import functools

import jax
import jax.numpy as jnp
import numpy as np
from jax import lax
from jax.experimental import pallas as pl
from jax.experimental.pallas import tpu as pltpu

F32 = jnp.float32
BF16 = jnp.bfloat16

GRID_W = 64
RET_HEADS = 8
HEAD_DIM = 128
RET_WIDTH = RET_HEADS * HEAD_DIM
CONV_CH = 1024
EPS = 1e-6
ROPE_THETA = 10000.0

MOD_ROWS = 8
INPROJ_TILE = 512
MIX_TILE = 512
FFN_COLS = 8
RET_CHUNK = 256
FF_TILE = 512
MOD_COL_TILE = 768
LAST_STEP_ROW_GROUPS = 4
V7X_VMEM_BYTES = 64 * 1024 * 1024
VMEM_LIMIT = V7X_VMEM_BYTES - 4 * 1024 * 1024


def _sigmoid(x):
    return 1.0 / (1.0 + jnp.exp(-x))


def _log_sigmoid(x):
    return jnp.minimum(x, 0.0) - jnp.log(1.0 + jnp.exp(-jnp.abs(x)))


def _norm_mod(x, gain, shift, scale):
    ms = jnp.mean(x * x, axis=-1, keepdims=True)
    y = x * lax.rsqrt(ms + EPS) * gain
    return y * (1.0 + scale) + shift


def _dot(a, b):
    return jnp.dot(a, b, preferred_element_type=F32)


def _dot_t0(a, b):
    return lax.dot_general(a, b, (((0,), (0,)), ((), ())), preferred_element_type=F32)


def _dot_nt(a, b):
    return lax.dot_general(a, b, (((1,), (1,)), ((), ())), preferred_element_type=F32)


def _mod_kernel(c_ref, w_ref, b_ref, win_ref, o_ref, win_b_ref):
    a = c_ref[...]
    a = (a * _sigmoid(a)).astype(BF16)
    o_ref[...] = _dot(a, w_ref[...].astype(BF16)) + b_ref[...]
    win_b_ref[...] = win_ref[...].astype(win_b_ref.dtype)


def _mod_call(craw, w_mod, b_mod, w_in):
    d, n = w_mod.shape
    tn = MOD_COL_TILE
    steps = n // tn
    slab = w_in.shape[0] // steps
    assert steps * tn == n and slab * steps == w_in.shape[0] and slab % 16 == 0
    slab_spec = pl.BlockSpec((slab, w_in.shape[1]), lambda j: (j, 0))
    return pl.pallas_call(
        _mod_kernel,
        grid=(steps,),
        in_specs=[
            pl.BlockSpec((MOD_ROWS, d), lambda j: (0, 0)),
            pl.BlockSpec((d, tn), lambda j: (0, j)),
            pl.BlockSpec((1, tn), lambda j: (0, j)),
            slab_spec,
        ],
        out_specs=[pl.BlockSpec((MOD_ROWS, tn), lambda j: (0, j)), slab_spec],
        out_shape=[jax.ShapeDtypeStruct((MOD_ROWS, n), F32),
                   jax.ShapeDtypeStruct(w_in.shape, BF16)],
        compiler_params=pltpu.CompilerParams(
            dimension_semantics=("arbitrary",), vmem_limit_bytes=VMEM_LIMIT),
        name="mod",
    )(craw, w_mod, b_mod, w_in)


def _head_lanes(rows):
    return jnp.concatenate([rows[hd:hd + 1, :HEAD_DIM] for hd in range(RET_HEADS)], axis=1)


def _ctx_kernel(ctx_row, x_ref, sh_ref, sc_ref, g_ref, wk_ref, wv_ref, dec_ref, sf_ref, sb_ref):
    nb, length, d = x_ref.shape
    x = x_ref[...].reshape(nb * length, d)
    h = _norm_mod(x, g_ref[...], sh_ref[ctx_row:ctx_row + 1, :],
                  sc_ref[ctx_row:ctx_row + 1, :]).astype(BF16)
    k = _dot(h, wk_ref[...]) * (HEAD_DIM ** -0.5)
    v = _dot(h, wv_ref[...]).astype(BF16)
    t = (lax.broadcasted_iota(jnp.int32, k.shape, 0) % length).astype(F32)
    lg = _log_sigmoid(dec_ref[...])
    kf = (k * jnp.exp(_head_lanes(lg[:RET_HEADS]) * (length - 1.0 - t))).astype(BF16)
    kb = (k * jnp.exp(_head_lanes(lg[RET_HEADS:]) * t)).astype(BF16)
    for bi in range(nb):
        rows = slice(bi * length, (bi + 1) * length)
        for hd in range(RET_HEADS):
            sl = slice(hd * HEAD_DIM, (hd + 1) * HEAD_DIM)
            sf_ref[bi, hd] = _dot_t0(v[rows, sl], kf[rows, sl])
            sb_ref[bi, hd] = _dot_t0(v[rows, sl], kb[rows, sl])


def _ctx_call(ctx, mod, norm_g, w_in, dec_rows):
    b, length, d = ctx.shape
    r = RET_WIDTH
    state = jax.ShapeDtypeStruct((b, RET_HEADS, HEAD_DIM, HEAD_DIM), F32)
    state_spec = pl.BlockSpec(state.shape, lambda i: (0, 0, 0, 0))
    return pl.pallas_call(
        functools.partial(_ctx_kernel, b),
        grid=(1,),
        in_specs=[
            pl.BlockSpec((b, length, d), lambda i: (0, 0, 0)),
            pl.BlockSpec((MOD_ROWS, d), lambda i: (0, 0)),
            pl.BlockSpec((MOD_ROWS, d), lambda i: (0, 1)),
            pl.BlockSpec((1, d), lambda i: (0, 0)),
            pl.BlockSpec((d, r), lambda i: (0, 1)),
            pl.BlockSpec((d, r), lambda i: (0, 2)),
            pl.BlockSpec(dec_rows.shape, lambda i: (0, 0)),
        ],
        out_specs=[state_spec, state_spec],
        out_shape=[state, state],
        compiler_params=pltpu.CompilerParams(
            dimension_semantics=("arbitrary",), vmem_limit_bytes=VMEM_LIMIT),
        name="ctx",
    )(ctx, mod, mod, norm_g, w_in, w_in, dec_rows)


def _rope(acc, cos, sin, scale):
    lane = lax.broadcasted_iota(jnp.int32, (1, HEAD_DIM), 1)
    lo = (lane % 64) < 32
    outs = []
    for hd in range(RET_HEADS):
        xh = acc[:, hd * HEAD_DIM:(hd + 1) * HEAD_DIM]
        partner = jnp.where(lo, pltpu.roll(xh, 96, 1), pltpu.roll(xh, 32, 1))
        y = xh * cos + partner * sin
        outs.append(y if scale == 1.0 else y * scale)
    return outs


def _inproj_kernel(x_ref, sh_ref, sc_ref, g_ref, w_ref, cos_ref, sin_ref, cw_ref, dec_ref,
                   sb0_ref, wout_ref,
                   q_ref, k_ref, vt_ref, gate_ref, conv_ref, sbound_ref, wout_b_ref,
                   s_scr, dec_scr):
    b = pl.program_id(0)
    i = pl.program_id(1)
    tm = dec_scr.shape[1]
    r = RET_WIDTH

    wout_b_ref[...] = wout_ref[...].astype(wout_b_ref.dtype)

    @pl.when(i == 0)
    def _():
        s_scr[...] = sb0_ref[0]
        t = lax.broadcasted_iota(jnp.int32, dec_scr.shape, 1).astype(F32)
        lgb = _log_sigmoid(dec_ref[RET_HEADS:, 0:1])
        dec_scr[...] = jnp.exp(lgb * t)

    h = _norm_mod(x_ref[0], g_ref[...], sh_ref[pl.ds(b, 1), :], sc_ref[pl.ds(b, 1), :]).astype(BF16)

    def proj(group):
        return _dot(h, w_ref[:, group * r:(group + 1) * r])

    cos = cos_ref[...]
    sin = sin_ref[...]
    for hd, qh in enumerate(_rope(proj(0), cos, sin, 1.0)):
        q_ref[0, :, hd * HEAD_DIM:(hd + 1) * HEAD_DIM] = qh.astype(q_ref.dtype)
    ks = [kh.astype(BF16) for kh in _rope(proj(1), cos, sin, HEAD_DIM ** -0.5)]
    for hd in range(RET_HEADS):
        k_ref[0, :, hd * HEAD_DIM:(hd + 1) * HEAD_DIM] = ks[hd]
    v = proj(2)
    tile_dec = jnp.exp(_log_sigmoid(dec_ref[RET_HEADS:, :HEAD_DIM]) * float(tm))
    for hd in range(RET_HEADS):
        vt = v[:, hd * HEAD_DIM:(hd + 1) * HEAD_DIM].T
        vt_ref[0, hd] = vt.astype(vt_ref.dtype)
        vtd = (vt * dec_scr[hd:hd + 1, :]).astype(BF16)
        s_old = s_scr[hd]
        sbound_ref[0, 0, hd] = s_old
        s_scr[hd] = tile_dec[hd:hd + 1, :] * s_old + _dot(vtd, ks[hd])

    g = proj(3)
    gate_ref[0] = (g * _sigmoid(g)).astype(gate_ref.dtype)

    u = proj(5) * proj(6)
    col = lax.broadcasted_iota(jnp.int32, (tm, 1), 0) % GRID_W
    up = jnp.where(col == 0, 0.0, pltpu.roll(u, 1, 0))
    un = jnp.where(col == GRID_W - 1, 0.0, pltpu.roll(u, tm - 1, 0))
    y = proj(4) * (cw_ref[0:1, :] * up + cw_ref[1:2, :] * u + cw_ref[2:3, :] * un)
    conv_ref[0] = y.astype(conv_ref.dtype)


def _inproj_call(x, mod, norm_g, w_in, cos_t, sin_t, conv_w, dec_rows, sb0, w_out):
    b, n, d = x.shape
    tm = INPROJ_TILE
    nt = n // tm
    r = RET_WIDTH
    slab = w_out.shape[0] // (b * nt)
    assert slab * b * nt == w_out.shape[0] and slab % 16 == 0
    slab_spec = pl.BlockSpec((slab, w_out.shape[1]), lambda bi, i: (bi * nt + i, 0))

    def tok(bi, i):
        return (bi, nt - 1 - i, 0)

    seq = jax.ShapeDtypeStruct((b, n, r), BF16)
    seq_spec = pl.BlockSpec((1, tm, r), tok)
    heads_t = jax.ShapeDtypeStruct((b, RET_HEADS, HEAD_DIM, n), BF16)
    heads_t_spec = pl.BlockSpec((1, RET_HEADS, HEAD_DIM, tm), lambda bi, i: (bi, 0, 0, nt - 1 - i))
    return pl.pallas_call(
        _inproj_kernel,
        grid=(b, nt),
        in_specs=[
            pl.BlockSpec((1, tm, d), tok),
            pl.BlockSpec((MOD_ROWS, d), lambda bi, i: (0, 0)),
            pl.BlockSpec((MOD_ROWS, d), lambda bi, i: (0, 1)),
            pl.BlockSpec((1, d), lambda bi, i: (0, 0)),
            pl.BlockSpec(w_in.shape, lambda bi, i: (0, 0), pipeline_mode=pl.Buffered(1)),
            pl.BlockSpec((tm, HEAD_DIM), lambda bi, i: (nt - 1 - i, 0)),
            pl.BlockSpec((tm, HEAD_DIM), lambda bi, i: (nt - 1 - i, 0)),
            pl.BlockSpec((3, CONV_CH), lambda bi, i: (0, 0)),
            pl.BlockSpec(dec_rows.shape, lambda bi, i: (0, 0)),
            pl.BlockSpec((1, RET_HEADS, HEAD_DIM, HEAD_DIM), lambda bi, i: (bi, 0, 0, 0)),
            slab_spec,
        ],
        out_specs=[
            seq_spec, seq_spec, heads_t_spec, seq_spec, seq_spec,
            pl.BlockSpec((1, 1, RET_HEADS, HEAD_DIM, HEAD_DIM),
                         lambda bi, i: (bi, nt - 1 - i, 0, 0, 0)),
            slab_spec,
        ],
        out_shape=[seq, seq, heads_t, seq, seq,
                   jax.ShapeDtypeStruct((b, nt, RET_HEADS, HEAD_DIM, HEAD_DIM), F32),
                   jax.ShapeDtypeStruct(w_out.shape, BF16)],
        scratch_shapes=[
            pltpu.VMEM((RET_HEADS, HEAD_DIM, HEAD_DIM), F32),
            pltpu.VMEM((RET_HEADS, tm), F32),
        ],
        compiler_params=pltpu.CompilerParams(
            dimension_semantics=("arbitrary", "arbitrary"),
            vmem_limit_bytes=VMEM_LIMIT),
        name="inproj",
    )(x, mod, mod, norm_g, w_in, cos_t, sin_t, conv_w, dec_rows, sb0, w_out)


def _mix_kernel(q_ref, k_ref, vt_ref, gate_ref, conv_ref, x_ref, sbound_ref, sf0_ref,
                wo_r_ref, wo_c_ref, g1_ref, dec_ref, wup_ref, wdn_ref,
                o_ref, wup_b_ref, wdn_b_ref,
                sf_scr, dmat_scr, qdec_scr, kdec_scr, cdec_scr, ret_scr):
    b = pl.program_id(0)
    i = pl.program_id(1)
    tm = ret_scr.shape[0]
    c = dmat_scr.shape[1]
    nc = tm // c

    wup_b_ref[...] = wup_ref[...].astype(wup_b_ref.dtype)
    wdn_b_ref[...] = wdn_ref[...].astype(wdn_b_ref.dtype)

    @pl.when(i == 0)
    def _():
        sf_scr[...] = sf0_ref[0]

    @pl.when((b == 0) & (i == 0))
    def _():
        lgf = _log_sigmoid(dec_ref[:RET_HEADS])
        lgb = _log_sigmoid(dec_ref[RET_HEADS:])
        ri = lax.broadcasted_iota(jnp.int32, (c, c), 0)
        ci = lax.broadcasted_iota(jnp.int32, (c, c), 1)
        rel = (ri - ci).astype(F32)
        t = lax.broadcasted_iota(jnp.int32, (c, HEAD_DIM), 0).astype(F32)
        tl = lax.broadcasted_iota(jnp.int32, (1, c), 1).astype(F32)
        for hd in range(RET_HEADS):
            f = lgf[hd:hd + 1, :]
            g = lgb[hd:hd + 1, :]
            dmat_scr[hd] = jnp.where(rel >= 0, jnp.exp(f * jnp.maximum(rel, 0.0)),
                                     jnp.exp(g * jnp.maximum(-rel, 0.0)))
            fh = f[:, :HEAD_DIM]
            gh = g[:, :HEAD_DIM]
            qdec_scr[hd, :, :HEAD_DIM] = jnp.exp(fh * (t + 1.0))
            qdec_scr[hd, :, HEAD_DIM:] = jnp.exp(gh * (float(c) - t))
            kdec_scr[hd, 0:1, :] = jnp.exp(f * (float(c) - 1.0 - tl))
            kdec_scr[hd, 1:2, :] = jnp.exp(g * tl)
            cdec_scr[hd, :, :HEAD_DIM] = jnp.broadcast_to(jnp.exp(fh * float(c)), (8, HEAD_DIM))
            cdec_scr[hd, :, HEAD_DIM:] = jnp.broadcast_to(jnp.exp(gh * float(c)), (8, HEAD_DIM))

    heads = range(RET_HEADS)
    hsl = [slice(hd * HEAD_DIM, (hd + 1) * HEAD_DIM) for hd in heads]
    crows = [slice(ch * c, (ch + 1) * c) for ch in range(nc)]
    ks = [[k_ref[0, crows[ch], hsl[hd]] for ch in range(nc)] for hd in heads]
    vts = [[vt_ref[0, hd, :, crows[ch]] for ch in range(nc)] for hd in heads]

    qs = [[q_ref[0, crows[ch], hsl[hd]] for ch in range(nc)] for hd in heads]

    def masked_scores(ch):
        return [(_dot_nt(qs[hd][ch], ks[hd][ch]) * dmat_scr[hd]).astype(BF16) for hd in heads]

    ps_first = masked_scores(0)

    s2ts = []
    for hd in heads:
        cf = cdec_scr[hd, 0:1, :HEAD_DIM]
        cb = cdec_scr[hd, 0:1, HEAD_DIM:]
        sb = [None] * nc
        sb[nc - 1] = sbound_ref[0, 0, hd]
        for ch in range(nc - 1, 0, -1):
            vb = (vts[hd][ch].astype(F32) * kdec_scr[hd, 1:2, :]).astype(BF16)
            sb[ch - 1] = cb * sb[ch] + _dot(vb, ks[hd][ch])
        sf = [sf_scr[hd]]
        for ch in range(nc):
            vf = (vts[hd][ch].astype(F32) * kdec_scr[hd, 0:1, :]).astype(BF16)
            sf.append(cf * sf[ch] + _dot(vf, ks[hd][ch]))
        sf_scr[hd] = sf[nc]
        s2ts.append([jnp.concatenate([sf[ch], sb[ch]], axis=1).astype(BF16) for ch in range(nc)])

    for ch in range(nc):
        ps = ps_first if ch == 0 else masked_scores(ch)
        cross = []
        for hd in heads:
            qh = qs[hd][ch]
            q2 = (jnp.concatenate([qh, qh], axis=1).astype(F32) * qdec_scr[hd]).astype(BF16)
            cross.append(_dot_nt(s2ts[hd][ch], q2))
        for hd in heads:
            o = (_dot_nt(vts[hd][ch], ps[hd]) + cross[hd]).T
            ms = jnp.mean(o * o, axis=-1, keepdims=True)
            o = o * lax.rsqrt(ms + EPS) * gate_ref[0, crows[ch], hsl[hd]].astype(F32)
            ret_scr[crows[ch], hsl[hd]] = o.astype(BF16)
        mix = _dot(ret_scr[crows[ch], :], wo_r_ref[...]) + _dot(conv_ref[0, crows[ch], :], wo_c_ref[...])
        o_ref[0, crows[ch], :] = x_ref[0, crows[ch], :] + g1_ref[pl.ds(b, 1), :] * mix


def _mix_call(q, k, vt, gate, conv, x, sbound, sf0, w_out, mod, dec_rows, w_up, w_down):
    b, n, d = x.shape
    tm = MIX_TILE
    nt = n // tm
    bound_per_tile = sbound.shape[1] // nt
    r = RET_WIDTH
    c = RET_CHUNK

    def slab_spec_of(w):
        slab = w.shape[0] // (b * nt)
        assert slab * b * nt == w.shape[0] and slab % 16 == 0
        return pl.BlockSpec((slab, w.shape[1]), lambda bi, i: (bi * nt + i, 0))

    def tok(bi, i):
        return (bi, i, 0)

    seq_spec = pl.BlockSpec((1, tm, r), tok)
    return pl.pallas_call(
        _mix_kernel,
        grid=(b, nt),
        in_specs=[
            seq_spec, seq_spec,
            pl.BlockSpec((1, RET_HEADS, HEAD_DIM, tm), lambda bi, i: (bi, 0, 0, i)),
            seq_spec, seq_spec,
            pl.BlockSpec((1, tm, d), tok),
            pl.BlockSpec((1, 1, RET_HEADS, HEAD_DIM, HEAD_DIM),
                         lambda bi, i: (bi, (i + 1) * bound_per_tile - 1, 0, 0, 0)),
            pl.BlockSpec((1, RET_HEADS, HEAD_DIM, HEAD_DIM), lambda bi, i: (bi, 0, 0, 0)),
            pl.BlockSpec((r, d), lambda bi, i: (0, 0)),
            pl.BlockSpec((CONV_CH, d), lambda bi, i: (1, 0)),
            pl.BlockSpec((MOD_ROWS, d), lambda bi, i: (0, 2)),
            pl.BlockSpec(dec_rows.shape, lambda bi, i: (0, 0)),
            slab_spec_of(w_up),
            slab_spec_of(w_down),
        ],
        out_specs=[pl.BlockSpec((1, tm, d), tok), slab_spec_of(w_up), slab_spec_of(w_down)],
        out_shape=[jax.ShapeDtypeStruct((b, n, d), F32),
                   jax.ShapeDtypeStruct(w_up.shape, BF16),
                   jax.ShapeDtypeStruct(w_down.shape, BF16)],
        scratch_shapes=[
            pltpu.VMEM((RET_HEADS, HEAD_DIM, HEAD_DIM), F32),
            pltpu.VMEM((RET_HEADS, c, c), F32),
            pltpu.VMEM((RET_HEADS, c, 2 * HEAD_DIM), F32),
            pltpu.VMEM((RET_HEADS, 8, c), F32),
            pltpu.VMEM((RET_HEADS, 8, 2 * HEAD_DIM), F32),
            pltpu.VMEM((tm, r), BF16),
        ],
        compiler_params=pltpu.CompilerParams(
            dimension_semantics=("arbitrary", "arbitrary"), vmem_limit_bytes=VMEM_LIMIT),
        name="mix",
    )(q, k, vt, gate, conv, x, sbound, sf0, w_out, w_out, mod, dec_rows, w_up, w_down)


def _ffn_kernel(x_ref, sh_ref, sc_ref, gt_ref, g2_ref, gf_ref,
                wa_ref, wb_ref, cw_ref, cb_ref, wd_ref,
                o_ref,
                h_scr):
    b = pl.program_id(0)
    f = pl.program_id(2)
    nf = pl.num_programs(2)
    tm, d = h_scr.shape
    tf = wa_ref.shape[1]
    blk = o_ref.shape[1:]
    step = blk[1]

    def body(first, last):
        if first:
            h = _norm_mod(x_ref[0].reshape(tm, d), g2_ref[...], sh_ref[pl.ds(b, 1), :],
                          sc_ref[pl.ds(b, 1), :]).astype(BF16)
            h_scr[...] = h
        else:
            h = h_scr[...]
        a = _dot(h, wa_ref[...])
        zero = jnp.zeros((step, tf), F32)
        a_up = jnp.concatenate([zero, a[:tm - step]], axis=0)
        a_dn = jnp.concatenate([a[step:], zero], axis=0)
        a = cw_ref[0:1, :] * a_up + cw_ref[1:2, :] * a + cw_ref[2:3, :] * a_dn + cb_ref[...]
        act = (a * _sigmoid(a) * _dot(h, wb_ref[...])).astype(BF16)
        if not last:
            y = _dot(act, wd_ref[...])
            o_ref[0] = (y if first else o_ref[0].reshape(tm, d) + y).reshape(blk)
            return
        groups = LAST_STEP_ROW_GROUPS
        gr = blk[0] // groups
        ys = [_dot(act[g * gr * step:(g + 1) * gr * step], wd_ref[...]) for g in range(groups)]
        for g in range(groups):
            rs = slice(g * gr, (g + 1) * gr)
            acc = o_ref[0, rs].reshape(gr * step, d) + ys[g]
            z = x_ref[0, rs].reshape(gr * step, d) + gt_ref[pl.ds(b, 1), :] * acc
            ms = jnp.mean(z * z, axis=-1, keepdims=True)
            o_ref[0, rs] = (z * lax.rsqrt(ms + EPS) * gf_ref[...]).reshape(gr, step, d)

    @pl.when(f == 0)
    def _():
        body(True, False)

    @pl.when((f > 0) & (f < nf - 1))
    def _():
        body(False, False)

    @pl.when(f == nf - 1)
    def _():
        body(False, True)


def _ffn_call(x, mod, norm_g, final_g, w_up, ffn_conv_w, ffn_conv_b, w_down):
    b, n, d = x.shape
    d_ff = w_down.shape[0]
    rows = n // GRID_W
    cols = FFN_COLS
    tf = FF_TILE
    nf = d_ff // tf
    assert nf >= 2
    xg = x.reshape(b, rows, GRID_W, d)
    tile = pl.BlockSpec((1, rows, cols, d), lambda bi, i, f: (bi, 0, i, 0))

    out = pl.pallas_call(
        _ffn_kernel,
        grid=(b, GRID_W // cols, nf),
        in_specs=[
            tile,
            pl.BlockSpec((MOD_ROWS, d), lambda bi, i, f: (0, 3)),
            pl.BlockSpec((MOD_ROWS, d), lambda bi, i, f: (0, 4)),
            pl.BlockSpec((MOD_ROWS, d), lambda bi, i, f: (0, 5)),
            pl.BlockSpec((1, d), lambda bi, i, f: (0, 0)),
            pl.BlockSpec((1, d), lambda bi, i, f: (0, 0)),
            pl.BlockSpec((d, tf), lambda bi, i, f: (0, f)),
            pl.BlockSpec((d, tf), lambda bi, i, f: (0, nf + f)),
            pl.BlockSpec((3, tf), lambda bi, i, f: (0, f)),
            pl.BlockSpec((1, tf), lambda bi, i, f: (0, f)),
            pl.BlockSpec((tf, d), lambda bi, i, f: (f, 0)),
        ],
        out_specs=tile,
        out_shape=jax.ShapeDtypeStruct((b, rows, GRID_W, d), F32),
        scratch_shapes=[
            pltpu.VMEM((rows * cols, d), BF16),
        ],
        compiler_params=pltpu.CompilerParams(
            dimension_semantics=("arbitrary", "arbitrary", "arbitrary"),
            vmem_limit_bytes=VMEM_LIMIT),
        name="ffn",
    )(xg, mod, mod, mod, norm_g, final_g, w_up, w_up, ffn_conv_w, ffn_conv_b, w_down)
    return out.reshape(b, n, d)


def _rope_tables(n):
    pos = np.arange(n, dtype=np.int32)
    row = (pos // GRID_W).astype(np.float32)
    col = (pos % GRID_W).astype(np.float32)
    dh = HEAD_DIM // 2
    freqs = np.float32(ROPE_THETA) ** (-np.arange(0, dh, 2, dtype=np.float32) / np.float32(dh))
    parts_c, parts_s = [], []
    for p in (row, col):
        ang = (p[:, None] * freqs[None, :]).astype(np.float32)
        cos = np.cos(ang).astype(np.float32)
        sin = np.sin(ang).astype(np.float32)
        parts_c += [cos, cos]
        parts_s += [-sin, sin]
    return (jnp.asarray(np.concatenate(parts_c, axis=-1)),
            jnp.asarray(np.concatenate(parts_s, axis=-1)))


def kernel(x, c, ctx, c_ctx, w_mod, b_mod, norm1_g, w_in, ret_decay_fwd, ret_decay_bwd,
           conv_w, w_out, norm2_g, w_up, ffn_conv_w, ffn_conv_b, w_down, final_g):
    b, n, d = x.shape
    depth = w_mod.shape[0]
    assert depth == 1 and n % MIX_TILE == 0 and n % GRID_W == 0
    assert MIX_TILE % INPROJ_TILE == 0 and MIX_TILE % RET_CHUNK == 0
    cos_t, sin_t = _rope_tables(n)
    craw = jnp.concatenate(
        [c, c_ctx[None, :], jnp.zeros((MOD_ROWS - b - 1, d), F32)], axis=0)

    layer = 0
    mod, w_in_b = _mod_call(craw, w_mod[layer], b_mod[layer][None, :], w_in[layer])
    dec_rows = jnp.broadcast_to(
        jnp.concatenate([ret_decay_fwd[layer], ret_decay_bwd[layer]])[:, None],
        (2 * RET_HEADS, RET_CHUNK))
    g1 = norm1_g[layer][None, :]

    sf0, sb0 = _ctx_call(ctx, mod, g1, w_in_b, dec_rows)
    q, k, vt, gate, conv, sbound, w_out_b = _inproj_call(
        x, mod, g1, w_in_b, cos_t, sin_t, conv_w[layer], dec_rows, sb0, w_out[layer])
    x_mid, w_up_b, w_down_b = _mix_call(q, k, vt, gate, conv, x, sbound, sf0, w_out_b, mod,
                                        dec_rows, w_up[layer], w_down[layer])
    return _ffn_call(x_mid, mod, norm2_g[layer][None, :], final_g[None, :], w_up_b,
                     ffn_conv_w[layer], ffn_conv_b[layer][None, :], w_down_b)
```

```python
import functools

import jax
import jax.numpy as jnp
import numpy as np
from jax import lax
from jax.experimental import pallas as pl
from jax.experimental.pallas import tpu as pltpu

F32 = jnp.float32
BF16 = jnp.bfloat16

GRID_W = 64
RET_HEADS = 8
HEAD_DIM = 128
RET_WIDTH = RET_HEADS * HEAD_DIM
CONV_CH = 1024
EPS = 1e-6
ROPE_THETA = 10000.0

MOD_ROWS = 8
INPROJ_TILE = 512
MIX_TILE = 512
FFN_COLS = 8
RET_CHUNK = 256
FF_TILE = 512
MOD_COL_TILE = 1536
LAST_STEP_ROW_GROUPS = 4
V7X_VMEM_BYTES = 64 * 1024 * 1024
VMEM_LIMIT = V7X_VMEM_BYTES - 4 * 1024 * 1024


def _sigmoid(x):
    return 1.0 / (1.0 + jnp.exp(-x))


def _log_sigmoid(x):
    return jnp.minimum(x, 0.0) - jnp.log(1.0 + jnp.exp(-jnp.abs(x)))


def _norm_mod(x, gain, shift, scale):
    ms = jnp.mean(x * x, axis=-1, keepdims=True)
    y = x * lax.rsqrt(ms + EPS) * gain
    return y * (1.0 + scale) + shift


def _dot(a, b):
    return jnp.dot(a, b, preferred_element_type=F32)


def _dot_t0(a, b):
    return lax.dot_general(a, b, (((0,), (0,)), ((), ())), preferred_element_type=F32)


def _dot_nt(a, b):
    return lax.dot_general(a, b, (((1,), (1,)), ((), ())), preferred_element_type=F32)


def _mod_kernel(c_ref, w_ref, b_ref, win_ref, o_ref, win_b_ref):
    a = c_ref[...]
    a = (a * _sigmoid(a)).astype(BF16)
    o_ref[...] = _dot(a, w_ref[...].astype(BF16)) + b_ref[...]
    win_b_ref[...] = win_ref[...].astype(win_b_ref.dtype)


def _mod_call(craw, w_mod, b_mod, w_in):
    d, n = w_mod.shape
    tn = MOD_COL_TILE
    steps = n // tn
    slab = w_in.shape[0] // steps
    assert steps * tn == n and slab * steps == w_in.shape[0] and slab % 16 == 0
    slab_spec = pl.BlockSpec((slab, w_in.shape[1]), lambda j: (j, 0))
    return pl.pallas_call(
        _mod_kernel,
        grid=(steps,),
        in_specs=[
            pl.BlockSpec((MOD_ROWS, d), lambda j: (0, 0)),
            pl.BlockSpec((d, tn), lambda j: (0, j)),
            pl.BlockSpec((1, tn), lambda j: (0, j)),
            slab_spec,
        ],
        out_specs=[pl.BlockSpec((MOD_ROWS, tn), lambda j: (0, j)), slab_spec],
        out_shape=[jax.ShapeDtypeStruct((MOD_ROWS, n), F32),
                   jax.ShapeDtypeStruct(w_in.shape, BF16)],
        compiler_params=pltpu.CompilerParams(
            dimension_semantics=("arbitrary",), vmem_limit_bytes=VMEM_LIMIT),
        name="mod",
    )(craw, w_mod, b_mod, w_in)


def _head_lanes(rows):
    return jnp.concatenate([rows[hd:hd + 1, :HEAD_DIM] for hd in range(RET_HEADS)], axis=1)


def _ctx_kernel(ctx_row, x_ref, sh_ref, sc_ref, g_ref, wk_ref, wv_ref, dec_ref, sf_ref, sb_ref):
    nb, length, d = x_ref.shape
    x = x_ref[...].reshape(nb * length, d)
    h = _norm_mod(x, g_ref[...], sh_ref[ctx_row:ctx_row + 1, :],
                  sc_ref[ctx_row:ctx_row + 1, :]).astype(BF16)
    k = _dot(h, wk_ref[...]) * (HEAD_DIM ** -0.5)
    v = _dot(h, wv_ref[...]).astype(BF16)
    t = (lax.broadcasted_iota(jnp.int32, k.shape, 0) % length).astype(F32)
    lg = _log_sigmoid(dec_ref[...])
    kf = (k * jnp.exp(_head_lanes(lg[:RET_HEADS]) * (length - 1.0 - t))).astype(BF16)
    kb = (k * jnp.exp(_head_lanes(lg[RET_HEADS:]) * t)).astype(BF16)
    for bi in range(nb):
        rows = slice(bi * length, (bi + 1) * length)
        for hd in range(RET_HEADS):
            sl = slice(hd * HEAD_DIM, (hd + 1) * HEAD_DIM)
            sf_ref[bi, hd] = _dot_t0(v[rows, sl], kf[rows, sl])
            sb_ref[bi, hd] = _dot_t0(v[rows, sl], kb[rows, sl])


def _ctx_call(ctx, mod, norm_g, w_in, dec_rows):
    b, length, d = ctx.shape
    r = RET_WIDTH
    state = jax.ShapeDtypeStruct((b, RET_HEADS, HEAD_DIM, HEAD_DIM), F32)
    state_spec = pl.BlockSpec(state.shape, lambda i: (0, 0, 0, 0))
    return pl.pallas_call(
        functools.partial(_ctx_kernel, b),
        grid=(1,),
        in_specs=[
            pl.BlockSpec((b, length, d), lambda i: (0, 0, 0)),
            pl.BlockSpec((MOD_ROWS, d), lambda i: (0, 0)),
            pl.BlockSpec((MOD_ROWS, d), lambda i: (0, 1)),
            pl.BlockSpec((1, d), lambda i: (0, 0)),
            pl.BlockSpec((d, r), lambda i: (0, 1)),
            pl.BlockSpec((d, r), lambda i: (0, 2)),
            pl.BlockSpec(dec_rows.shape, lambda i: (0, 0)),
        ],
        out_specs=[state_spec, state_spec],
        out_shape=[state, state],
        compiler_params=pltpu.CompilerParams(
            dimension_semantics=("arbitrary",), vmem_limit_bytes=VMEM_LIMIT),
        name="ctx",
    )(ctx, mod, mod, norm_g, w_in, w_in, dec_rows)


def _rope(acc, cos, sin, scale):
    lane = lax.broadcasted_iota(jnp.int32, (1, HEAD_DIM), 1)
    lo = (lane % 64) < 32
    outs = []
    for hd in range(RET_HEADS):
        xh = acc[:, hd * HEAD_DIM:(hd + 1) * HEAD_DIM]
        partner = jnp.where(lo, pltpu.roll(xh, 96, 1), pltpu.roll(xh, 32, 1))
        y = xh * cos + partner * sin
        outs.append(y if scale == 1.0 else y * scale)
    return outs


def _inproj_kernel(x_ref, sh_ref, sc_ref, g_ref, w_ref, cos_ref, sin_ref, cw_ref, dec_ref,
                   sb0_ref, wout_ref,
                   q_ref, k_ref, vt_ref, gate_ref, conv_ref, sbound_ref, wout_b_ref,
                   s_scr, dec_scr):
    b = pl.program_id(0)
    i = pl.program_id(1)
    tm = dec_scr.shape[1]
    r = RET_WIDTH

    wout_b_ref[...] = wout_ref[...].astype(wout_b_ref.dtype)

    @pl.when(i == 0)
    def _():
        s_scr[...] = sb0_ref[0]
        t = lax.broadcasted_iota(jnp.int32, dec_scr.shape, 1).astype(F32)
        lgb = _log_sigmoid(dec_ref[RET_HEADS:, 0:1])
        dec_scr[...] = jnp.exp(lgb * t)

    h = _norm_mod(x_ref[0], g_ref[...], sh_ref[pl.ds(b, 1), :], sc_ref[pl.ds(b, 1), :]).astype(BF16)

    def proj(group):
        return _dot(h, w_ref[:, group * r:(group + 1) * r])

    cos = cos_ref[...]
    sin = sin_ref[...]
    for hd, qh in enumerate(_rope(proj(0), cos, sin, 1.0)):
        q_ref[0, :, hd * HEAD_DIM:(hd + 1) * HEAD_DIM] = qh.astype(q_ref.dtype)
    ks = [kh.astype(BF16) for kh in _rope(proj(1), cos, sin, HEAD_DIM ** -0.5)]
    for hd in range(RET_HEADS):
        k_ref[0, :, hd * HEAD_DIM:(hd + 1) * HEAD_DIM] = ks[hd]
    v = proj(2)
    tile_dec = jnp.exp(_log_sigmoid(dec_ref[RET_HEADS:, :HEAD_DIM]) * float(tm))
    for hd in range(RET_HEADS):
        vt = v[:, hd * HEAD_DIM:(hd + 1) * HEAD_DIM].T
        vt_ref[0, hd] = vt.astype(vt_ref.dtype)
        vtd = (vt * dec_scr[hd:hd + 1, :]).astype(BF16)
        s_old = s_scr[hd]
        sbound_ref[0, 0, hd] = s_old
        s_scr[hd] = tile_dec[hd:hd + 1, :] * s_old + _dot(vtd, ks[hd])

    g = proj(3)
    gate_ref[0] = (g * _sigmoid(g)).astype(gate_ref.dtype)

    u = proj(5) * proj(6)
    col = lax.broadcasted_iota(jnp.int32, (tm, 1), 0) % GRID_W
    up = jnp.where(col == 0, 0.0, pltpu.roll(u, 1, 0))
    un = jnp.where(col == GRID_W - 1, 0.0, pltpu.roll(u, tm - 1, 0))
    y = proj(4) * (cw_ref[0:1, :] * up + cw_ref[1:2, :] * u + cw_ref[2:3, :] * un)
    conv_ref[0] = y.astype(conv_ref.dtype)


def _inproj_call(x, mod, norm_g, w_in, cos_t, sin_t, conv_w, dec_rows, sb0, w_out):
    b, n, d = x.shape
    tm = INPROJ_TILE
    nt = n // tm
    r = RET_WIDTH
    slab = w_out.shape[0] // (b * nt)
    assert slab * b * nt == w_out.shape[0] and slab % 16 == 0
    slab_spec = pl.BlockSpec((slab, w_out.shape[1]), lambda bi, i: (bi * nt + i, 0))

    def tok(bi, i):
        return (bi, nt - 1 - i, 0)

    seq = jax.ShapeDtypeStruct((b, n, r), BF16)
    seq_spec = pl.BlockSpec((1, tm, r), tok)
    heads_t = jax.ShapeDtypeStruct((b, RET_HEADS, HEAD_DIM, n), BF16)
    heads_t_spec = pl.BlockSpec((1, RET_HEADS, HEAD_DIM, tm), lambda bi, i: (bi, 0, 0, nt - 1 - i))
    return pl.pallas_call(
        _inproj_kernel,
        grid=(b, nt),
        in_specs=[
            pl.BlockSpec((1, tm, d), tok),
            pl.BlockSpec((MOD_ROWS, d), lambda bi, i: (0, 0)),
            pl.BlockSpec((MOD_ROWS, d), lambda bi, i: (0, 1)),
            pl.BlockSpec((1, d), lambda bi, i: (0, 0)),
            pl.BlockSpec(w_in.shape, lambda bi, i: (0, 0), pipeline_mode=pl.Buffered(1)),
            pl.BlockSpec((tm, HEAD_DIM), lambda bi, i: (nt - 1 - i, 0)),
            pl.BlockSpec((tm, HEAD_DIM), lambda bi, i: (nt - 1 - i, 0)),
            pl.BlockSpec((3, CONV_CH), lambda bi, i: (0, 0)),
            pl.BlockSpec(dec_rows.shape, lambda bi, i: (0, 0)),
            pl.BlockSpec((1, RET_HEADS, HEAD_DIM, HEAD_DIM), lambda bi, i: (bi, 0, 0, 0)),
            slab_spec,
        ],
        out_specs=[
            seq_spec, seq_spec, heads_t_spec, seq_spec, seq_spec,
            pl.BlockSpec((1, 1, RET_HEADS, HEAD_DIM, HEAD_DIM),
                         lambda bi, i: (bi, nt - 1 - i, 0, 0, 0)),
            slab_spec,
        ],
        out_shape=[seq, seq, heads_t, seq, seq,
                   jax.ShapeDtypeStruct((b, nt, RET_HEADS, HEAD_DIM, HEAD_DIM), F32),
                   jax.ShapeDtypeStruct(w_out.shape, BF16)],
        scratch_shapes=[
            pltpu.VMEM((RET_HEADS, HEAD_DIM, HEAD_DIM), F32),
            pltpu.VMEM((RET_HEADS, tm), F32),
        ],
        compiler_params=pltpu.CompilerParams(
            dimension_semantics=("arbitrary", "arbitrary"),
            vmem_limit_bytes=VMEM_LIMIT),
        name="inproj",
    )(x, mod, mod, norm_g, w_in, cos_t, sin_t, conv_w, dec_rows, sb0, w_out)


def _mix_kernel(q_ref, k_ref, vt_ref, gate_ref, conv_ref, x_ref, sbound_ref, sf0_ref,
                wo_r_ref, wo_c_ref, g1_ref, dec_ref, wup_ref, wdn_ref,
                o_ref, wup_b_ref, wdn_b_ref,
                sf_scr, dmat_scr, qdec_scr, kdec_scr, cdec_scr, ret_scr):
    b = pl.program_id(0)
    i = pl.program_id(1)
    tm = ret_scr.shape[0]
    c = dmat_scr.shape[1]
    nc = tm // c

    wup_b_ref[...] = wup_ref[...].astype(wup_b_ref.dtype)
    wdn_b_ref[...] = wdn_ref[...].astype(wdn_b_ref.dtype)

    @pl.when(i == 0)
    def _():
        sf_scr[...] = sf0_ref[0]

    @pl.when((b == 0) & (i == 0))
    def _():
        lgf = _log_sigmoid(dec_ref[:RET_HEADS])
        lgb = _log_sigmoid(dec_ref[RET_HEADS:])
        ri = lax.broadcasted_iota(jnp.int32, (c, c), 0)
        ci = lax.broadcasted_iota(jnp.int32, (c, c), 1)
        rel = (ri - ci).astype(F32)
        t = lax.broadcasted_iota(jnp.int32, (c, HEAD_DIM), 0).astype(F32)
        tl = lax.broadcasted_iota(jnp.int32, (1, c), 1).astype(F32)
        for hd in range(RET_HEADS):
            f = lgf[hd:hd + 1, :]
            g = lgb[hd:hd + 1, :]
            dmat_scr[hd] = jnp.where(rel >= 0, jnp.exp(f * jnp.maximum(rel, 0.0)),
                                     jnp.exp(g * jnp.maximum(-rel, 0.0)))
            fh = f[:, :HEAD_DIM]
            gh = g[:, :HEAD_DIM]
            qdec_scr[hd, :, :HEAD_DIM] = jnp.exp(fh * (t + 1.0))
            qdec_scr[hd, :, HEAD_DIM:] = jnp.exp(gh * (float(c) - t))
            kdec_scr[hd, 0:1, :] = jnp.exp(f * (float(c) - 1.0 - tl))
            kdec_scr[hd, 1:2, :] = jnp.exp(g * tl)
            cdec_scr[hd, :, :HEAD_DIM] = jnp.broadcast_to(jnp.exp(fh * float(c)), (8, HEAD_DIM))
            cdec_scr[hd, :, HEAD_DIM:] = jnp.broadcast_to(jnp.exp(gh * float(c)), (8, HEAD_DIM))

    heads = range(RET_HEADS)
    hsl = [slice(hd * HEAD_DIM, (hd + 1) * HEAD_DIM) for hd in heads]
    crows = [slice(ch * c, (ch + 1) * c) for ch in range(nc)]
    ks = [[k_ref[0, crows[ch], hsl[hd]] for ch in range(nc)] for hd in heads]
    vts = [[vt_ref[0, hd, :, crows[ch]] for ch in range(nc)] for hd in heads]

    qs = [[q_ref[0, crows[ch], hsl[hd]] for ch in range(nc)] for hd in heads]

    def masked_scores(ch):
        return [(_dot_nt(qs[hd][ch], ks[hd][ch]) * dmat_scr[hd]).astype(BF16) for hd in heads]

    ps_first = masked_scores(0)

    s2ts = []
    for hd in heads:
        cf = cdec_scr[hd, 0:1, :HEAD_DIM]
        cb = cdec_scr[hd, 0:1, HEAD_DIM:]
        sb = [None] * nc
        sb[nc - 1] = sbound_ref[0, 0, hd]
        for ch in range(nc - 1, 0, -1):
            vb = (vts[hd][ch].astype(F32) * kdec_scr[hd, 1:2, :]).astype(BF16)
            sb[ch - 1] = cb * sb[ch] + _dot(vb, ks[hd][ch])
        sf = [sf_scr[hd]]
        for ch in range(nc):
            vf = (vts[hd][ch].astype(F32) * kdec_scr[hd, 0:1, :]).astype(BF16)
            sf.append(cf * sf[ch] + _dot(vf, ks[hd][ch]))
        sf_scr[hd] = sf[nc]
        s2ts.append([jnp.concatenate([sf[ch], sb[ch]], axis=1).astype(BF16) for ch in range(nc)])

    for ch in range(nc):
        ps = ps_first if ch == 0 else masked_scores(ch)
        cross = []
        for hd in heads:
            qh = qs[hd][ch]
            q2 = (jnp.concatenate([qh, qh], axis=1).astype(F32) * qdec_scr[hd]).astype(BF16)
            cross.append(_dot_nt(s2ts[hd][ch], q2))
        for hd in heads:
            o = (_dot_nt(vts[hd][ch], ps[hd]) + cross[hd]).T
            ms = jnp.mean(o * o, axis=-1, keepdims=True)
            o = o * lax.rsqrt(ms + EPS) * gate_ref[0, crows[ch], hsl[hd]].astype(F32)
            ret_scr[crows[ch], hsl[hd]] = o.astype(BF16)
        mix = _dot(ret_scr[crows[ch], :], wo_r_ref[...]) + _dot(conv_ref[0, crows[ch], :], wo_c_ref[...])
        o_ref[0, crows[ch], :] = x_ref[0, crows[ch], :] + g1_ref[pl.ds(b, 1), :] * mix


def _mix_call(q, k, vt, gate, conv, x, sbound, sf0, w_out, mod, dec_rows, w_up, w_down):
    b, n, d = x.shape
    tm = MIX_TILE
    nt = n // tm
    bound_per_tile = sbound.shape[1] // nt
    r = RET_WIDTH
    c = RET_CHUNK

    def slab_spec_of(w):
        slab = w.shape[0] // (b * nt)
        assert slab * b * nt == w.shape[0] and slab % 16 == 0
        return pl.BlockSpec((slab, w.shape[1]), lambda bi, i: (bi * nt + i, 0))

    def tok(bi, i):
        return (bi, i, 0)

    seq_spec = pl.BlockSpec((1, tm, r), tok)
    return pl.pallas_call(
        _mix_kernel,
        grid=(b, nt),
        in_specs=[
            seq_spec, seq_spec,
            pl.BlockSpec((1, RET_HEADS, HEAD_DIM, tm), lambda bi, i: (bi, 0, 0, i)),
            seq_spec, seq_spec,
            pl.BlockSpec((1, tm, d), tok),
            pl.BlockSpec((1, 1, RET_HEADS, HEAD_DIM, HEAD_DIM),
                         lambda bi, i: (bi, (i + 1) * bound_per_tile - 1, 0, 0, 0)),
            pl.BlockSpec((1, RET_HEADS, HEAD_DIM, HEAD_DIM), lambda bi, i: (bi, 0, 0, 0)),
            pl.BlockSpec((r, d), lambda bi, i: (0, 0)),
            pl.BlockSpec((CONV_CH, d), lambda bi, i: (1, 0)),
            pl.BlockSpec((MOD_ROWS, d), lambda bi, i: (0, 2)),
            pl.BlockSpec(dec_rows.shape, lambda bi, i: (0, 0)),
            slab_spec_of(w_up),
            slab_spec_of(w_down),
        ],
        out_specs=[pl.BlockSpec((1, tm, d), tok), slab_spec_of(w_up), slab_spec_of(w_down)],
        out_shape=[jax.ShapeDtypeStruct((b, n, d), F32),
                   jax.ShapeDtypeStruct(w_up.shape, BF16),
                   jax.ShapeDtypeStruct(w_down.shape, BF16)],
        scratch_shapes=[
            pltpu.VMEM((RET_HEADS, HEAD_DIM, HEAD_DIM), F32),
            pltpu.VMEM((RET_HEADS, c, c), F32),
            pltpu.VMEM((RET_HEADS, c, 2 * HEAD_DIM), F32),
            pltpu.VMEM((RET_HEADS, 8, c), F32),
            pltpu.VMEM((RET_HEADS, 8, 2 * HEAD_DIM), F32),
            pltpu.VMEM((tm, r), BF16),
        ],
        compiler_params=pltpu.CompilerParams(
            dimension_semantics=("arbitrary", "arbitrary"), vmem_limit_bytes=VMEM_LIMIT),
        name="mix",
    )(q, k, vt, gate, conv, x, sbound, sf0, w_out, w_out, mod, dec_rows, w_up, w_down)


def _ffn_kernel(x_ref, sh_ref, sc_ref, gt_ref, g2_ref, gf_ref,
                wa_ref, wb_ref, cw_ref, cb_ref, wd_ref,
                o_ref,
                h_scr):
    b = pl.program_id(0)
    f = pl.program_id(2)
    nf = pl.num_programs(2)
    tm, d = h_scr.shape
    tf = wa_ref.shape[1]
    blk = o_ref.shape[1:]
    step = blk[1]

    def body(first, last):
        if first:
            h = _norm_mod(x_ref[0].reshape(tm, d), g2_ref[...], sh_ref[pl.ds(b, 1), :],
                          sc_ref[pl.ds(b, 1), :]).astype(BF16)
            h_scr[...] = h
        else:
            h = h_scr[...]
        a = _dot(h, wa_ref[...])
        zero = jnp.zeros((step, tf), F32)
        a_up = jnp.concatenate([zero, a[:tm - step]], axis=0)
        a_dn = jnp.concatenate([a[step:], zero], axis=0)
        a = cw_ref[0:1, :] * a_up + cw_ref[1:2, :] * a + cw_ref[2:3, :] * a_dn + cb_ref[...]
        act = (a * _sigmoid(a) * _dot(h, wb_ref[...])).astype(BF16)
        if not last:
            y = _dot(act, wd_ref[...])
            o_ref[0] = (y if first else o_ref[0].reshape(tm, d) + y).reshape(blk)
            return
        groups = LAST_STEP_ROW_GROUPS
        gr = blk[0] // groups
        ys = [_dot(act[g * gr * step:(g + 1) * gr * step], wd_ref[...]) for g in range(groups)]
        for g in range(groups):
            rs = slice(g * gr, (g + 1) * gr)
            acc = o_ref[0, rs].reshape(gr * step, d) + ys[g]
            z = x_ref[0, rs].reshape(gr * step, d) + gt_ref[pl.ds(b, 1), :] * acc
            ms = jnp.mean(z * z, axis=-1, keepdims=True)
            o_ref[0, rs] = (z * lax.rsqrt(ms + EPS) * gf_ref[...]).reshape(gr, step, d)

    @pl.when(f == 0)
    def _():
        body(True, False)

    @pl.when((f > 0) & (f < nf - 1))
    def _():
        body(False, False)

    @pl.when(f == nf - 1)
    def _():
        body(False, True)


def _ffn_call(x, mod, norm_g, final_g, w_up, ffn_conv_w, ffn_conv_b, w_down):
    b, n, d = x.shape
    d_ff = w_down.shape[0]
    rows = n // GRID_W
    cols = FFN_COLS
    tf = FF_TILE
    nf = d_ff // tf
    assert nf >= 2
    xg = x.reshape(b, rows, GRID_W, d)
    tile = pl.BlockSpec((1, rows, cols, d), lambda bi, i, f: (bi, 0, i, 0))

    out = pl.pallas_call(
        _ffn_kernel,
        grid=(b, GRID_W // cols, nf),
        in_specs=[
            tile,
            pl.BlockSpec((MOD_ROWS, d), lambda bi, i, f: (0, 3)),
            pl.BlockSpec((MOD_ROWS, d), lambda bi, i, f: (0, 4)),
            pl.BlockSpec((MOD_ROWS, d), lambda bi, i, f: (0, 5)),
            pl.BlockSpec((1, d), lambda bi, i, f: (0, 0)),
            pl.BlockSpec((1, d), lambda bi, i, f: (0, 0)),
            pl.BlockSpec((d, tf), lambda bi, i, f: (0, f)),
            pl.BlockSpec((d, tf), lambda bi, i, f: (0, nf + f)),
            pl.BlockSpec((3, tf), lambda bi, i, f: (0, f)),
            pl.BlockSpec((1, tf), lambda bi, i, f: (0, f)),
            pl.BlockSpec((tf, d), lambda bi, i, f: (f, 0)),
        ],
        out_specs=tile,
        out_shape=jax.ShapeDtypeStruct((b, rows, GRID_W, d), F32),
        scratch_shapes=[
            pltpu.VMEM((rows * cols, d), BF16),
        ],
        compiler_params=pltpu.CompilerParams(
            dimension_semantics=("arbitrary", "arbitrary", "arbitrary"),
            vmem_limit_bytes=VMEM_LIMIT),
        name="ffn",
    )(xg, mod, mod, mod, norm_g, final_g, w_up, w_up, ffn_conv_w, ffn_conv_b, w_down)
    return out.reshape(b, n, d)


def _rope_tables(n):
    pos = np.arange(n, dtype=np.int32)
    row = (pos // GRID_W).astype(np.float32)
    col = (pos % GRID_W).astype(np.float32)
    dh = HEAD_DIM // 2
    freqs = np.float32(ROPE_THETA) ** (-np.arange(0, dh, 2, dtype=np.float32) / np.float32(dh))
    parts_c, parts_s = [], []
    for p in (row, col):
        ang = (p[:, None] * freqs[None, :]).astype(np.float32)
        cos = np.cos(ang).astype(np.float32)
        sin = np.sin(ang).astype(np.float32)
        parts_c += [cos, cos]
        parts_s += [-sin, sin]
    return (jnp.asarray(np.concatenate(parts_c, axis=-1)),
            jnp.asarray(np.concatenate(parts_s, axis=-1)))


def kernel(x, c, ctx, c_ctx, w_mod, b_mod, norm1_g, w_in, ret_decay_fwd, ret_decay_bwd,
           conv_w, w_out, norm2_g, w_up, ffn_conv_w, ffn_conv_b, w_down, final_g):
    b, n, d = x.shape
    depth = w_mod.shape[0]
    assert depth == 1 and n % MIX_TILE == 0 and n % GRID_W == 0
    assert MIX_TILE % INPROJ_TILE == 0 and MIX_TILE % RET_CHUNK == 0
    cos_t, sin_t = _rope_tables(n)
    craw = jnp.concatenate(
        [c, c_ctx[None, :], jnp.zeros((MOD_ROWS - b - 1, d), F32)], axis=0)

    layer = 0
    mod, w_in_b = _mod_call(craw, w_mod[layer], b_mod[layer][None, :], w_in[layer])
    dec_rows = jnp.broadcast_to(
        jnp.concatenate([ret_decay_fwd[layer], ret_decay_bwd[layer]])[:, None],
        (2 * RET_HEADS, RET_CHUNK))
    g1 = norm1_g[layer][None, :]

    sf0, sb0 = _ctx_call(ctx, mod, g1, w_in_b, dec_rows)
    q, k, vt, gate, conv, sbound, w_out_b = _inproj_call(
        x, mod, g1, w_in_b, cos_t, sin_t, conv_w[layer], dec_rows, sb0, w_out[layer])
    x_mid, w_up_b, w_down_b = _mix_call(q, k, vt, gate, conv, x, sbound, sf0, w_out_b, mod,
                                        dec_rows, w_up[layer], w_down[layer])
    return _ffn_call(x_mid, mod, norm2_g[layer][None, :], final_g[None, :], w_up_b,
                     ffn_conv_w[layer], ffn_conv_b[layer][None, :], w_down_b)
```

```python
import functools

import jax
import jax.numpy as jnp
import numpy as np
from jax import lax
from jax.experimental import pallas as pl
from jax.experimental.pallas import tpu as pltpu

F32 = jnp.float32
BF16 = jnp.bfloat16

GRID_W = 64
RET_HEADS = 8
HEAD_DIM = 128
RET_WIDTH = RET_HEADS * HEAD_DIM
CONV_CH = 1024
EPS = 1e-6
ROPE_THETA = 10000.0

MOD_ROWS = 8
INPROJ_TILE = 512
MIX_TILE = 512
FFN_COLS = 8
RET_CHUNK = 256
FF_TILE = 512
MOD_COL_TILE = 768
LAST_STEP_ROW_GROUPS = 4
V7X_VMEM_BYTES = 64 * 1024 * 1024
VMEM_LIMIT = V7X_VMEM_BYTES - 4 * 1024 * 1024


def _sigmoid(x):
    return 1.0 / (1.0 + jnp.exp(-x))


def _log_sigmoid(x):
    return jnp.minimum(x, 0.0) - jnp.log(1.0 + jnp.exp(-jnp.abs(x)))


def _norm_mod(x, gain, shift, scale):
    ms = jnp.mean(x * x, axis=-1, keepdims=True)
    y = x * lax.rsqrt(ms + EPS) * gain
    return y * (1.0 + scale) + shift


def _dot(a, b):
    return jnp.dot(a, b, preferred_element_type=F32)


def _dot_t0(a, b):
    return lax.dot_general(a, b, (((0,), (0,)), ((), ())), preferred_element_type=F32)


def _dot_nt(a, b):
    return lax.dot_general(a, b, (((1,), (1,)), ((), ())), preferred_element_type=F32)


def _mod_kernel(c_ref, w_ref, b_ref, win_ref, o_ref, win_b_ref):
    a = c_ref[...]
    a = (a * _sigmoid(a)).astype(BF16)
    o_ref[...] = _dot(a, w_ref[...].astype(BF16)) + b_ref[...]
    win_b_ref[...] = win_ref[...].astype(win_b_ref.dtype)


def _mod_call(craw, w_mod, b_mod, w_in):
    d, n = w_mod.shape
    tn = MOD_COL_TILE
    steps = n // tn
    slab = w_in.shape[0] // steps
    assert steps * tn == n and slab * steps == w_in.shape[0] and slab % 16 == 0
    slab_spec = pl.BlockSpec((slab, w_in.shape[1]), lambda j: (j, 0))
    return pl.pallas_call(
        _mod_kernel,
        grid=(steps,),
        in_specs=[
            pl.BlockSpec((MOD_ROWS, d), lambda j: (0, 0)),
            pl.BlockSpec((d, tn), lambda j: (0, j)),
            pl.BlockSpec((1, tn), lambda j: (0, j)),
            slab_spec,
        ],
        out_specs=[pl.BlockSpec((MOD_ROWS, tn), lambda j: (0, j)), slab_spec],
        out_shape=[jax.ShapeDtypeStruct((MOD_ROWS, n), F32),
                   jax.ShapeDtypeStruct(w_in.shape, BF16)],
        compiler_params=pltpu.CompilerParams(
            dimension_semantics=("arbitrary",), vmem_limit_bytes=VMEM_LIMIT),
        name="mod",
    )(craw, w_mod, b_mod, w_in)


def _head_lanes(rows):
    return jnp.concatenate([rows[hd:hd + 1, :HEAD_DIM] for hd in range(RET_HEADS)], axis=1)


def _ctx_kernel(ctx_row, x_ref, sh_ref, sc_ref, g_ref, wk_ref, wv_ref, dec_ref, sf_ref, sb_ref):
    nb, length, d = x_ref.shape
    x = x_ref[...].reshape(nb * length, d)
    h = _norm_mod(x, g_ref[...], sh_ref[ctx_row:ctx_row + 1, :],
                  sc_ref[ctx_row:ctx_row + 1, :]).astype(BF16)
    k = _dot(h, wk_ref[...]) * (HEAD_DIM ** -0.5)
    v = _dot(h, wv_ref[...]).astype(BF16)
    t = (lax.broadcasted_iota(jnp.int32, k.shape, 0) % length).astype(F32)
    lg = _log_sigmoid(dec_ref[...])
    kf = (k * jnp.exp(_head_lanes(lg[:RET_HEADS]) * (length - 1.0 - t))).astype(BF16)
    kb = (k * jnp.exp(_head_lanes(lg[RET_HEADS:]) * t)).astype(BF16)
    for bi in range(nb):
        rows = slice(bi * length, (bi + 1) * length)
        for hd in range(RET_HEADS):
            sl = slice(hd * HEAD_DIM, (hd + 1) * HEAD_DIM)
            sf_ref[bi, hd] = _dot_t0(v[rows, sl], kf[rows, sl])
            sb_ref[bi, hd] = _dot_t0(v[rows, sl], kb[rows, sl])


def _ctx_call(ctx, mod, norm_g, w_in, dec_rows):
    b, length, d = ctx.shape
    r = RET_WIDTH
    state = jax.ShapeDtypeStruct((b, RET_HEADS, HEAD_DIM, HEAD_DIM), F32)
    state_spec = pl.BlockSpec(state.shape, lambda i: (0, 0, 0, 0))
    return pl.pallas_call(
        functools.partial(_ctx_kernel, b),
        grid=(1,),
        in_specs=[
            pl.BlockSpec((b, length, d), lambda i: (0, 0, 0)),
            pl.BlockSpec((MOD_ROWS, d), lambda i: (0, 0)),
            pl.BlockSpec((MOD_ROWS, d), lambda i: (0, 1)),
            pl.BlockSpec((1, d), lambda i: (0, 0)),
            pl.BlockSpec((d, r), lambda i: (0, 1)),
            pl.BlockSpec((d, r), lambda i: (0, 2)),
            pl.BlockSpec(dec_rows.shape, lambda i: (0, 0)),
        ],
        out_specs=[state_spec, state_spec],
        out_shape=[state, state],
        compiler_params=pltpu.CompilerParams(
            dimension_semantics=("arbitrary",), vmem_limit_bytes=VMEM_LIMIT),
        name="ctx",
    )(ctx, mod, mod, norm_g, w_in, w_in, dec_rows)


def _rope(acc, cos, sin, scale):
    lane = lax.broadcasted_iota(jnp.int32, (1, HEAD_DIM), 1)
    lo = (lane % 64) < 32
    outs = []
    for hd in range(RET_HEADS):
        xh = acc[:, hd * HEAD_DIM:(hd + 1) * HEAD_DIM]
        partner = jnp.where(lo, pltpu.roll(xh, 96, 1), pltpu.roll(xh, 32, 1))
        y = xh * cos + partner * sin
        outs.append(y if scale == 1.0 else y * scale)
    return outs


def _inproj_kernel(x_ref, sh_ref, sc_ref, g_ref, w_ref, cos_ref, sin_ref, cw_ref, dec_ref,
                   sb0_ref, wout_ref,
                   q_ref, k_ref, vt_ref, gate_ref, conv_ref, sbound_ref, wout_b_ref,
                   s_scr, dec_scr):
    b = pl.program_id(0)
    i = pl.program_id(1)
    tm = dec_scr.shape[1]
    r = RET_WIDTH

    wout_b_ref[...] = wout_ref[...].astype(wout_b_ref.dtype)

    @pl.when(i == 0)
    def _():
        s_scr[...] = sb0_ref[0]
        t = lax.broadcasted_iota(jnp.int32, dec_scr.shape, 1).astype(F32)
        lgb = _log_sigmoid(dec_ref[RET_HEADS:, 0:1])
        dec_scr[...] = jnp.exp(lgb * t)

    h = _norm_mod(x_ref[0], g_ref[...], sh_ref[pl.ds(b, 1), :], sc_ref[pl.ds(b, 1), :]).astype(BF16)

    def proj(group):
        return _dot(h, w_ref[:, group * r:(group + 1) * r])

    cos = cos_ref[...]
    sin = sin_ref[...]
    for hd, qh in enumerate(_rope(proj(0), cos, sin, 1.0)):
        q_ref[0, :, hd * HEAD_DIM:(hd + 1) * HEAD_DIM] = qh.astype(q_ref.dtype)
    ks = [kh.astype(BF16) for kh in _rope(proj(1), cos, sin, HEAD_DIM ** -0.5)]
    for hd in range(RET_HEADS):
        k_ref[0, :, hd * HEAD_DIM:(hd + 1) * HEAD_DIM] = ks[hd]
    v = proj(2)
    tile_dec = jnp.exp(_log_sigmoid(dec_ref[RET_HEADS:, :HEAD_DIM]) * float(tm))
    for hd in range(RET_HEADS):
        vt = v[:, hd * HEAD_DIM:(hd + 1) * HEAD_DIM].T
        vt_ref[0, hd] = vt.astype(vt_ref.dtype)
        vtd = (vt * dec_scr[hd:hd + 1, :]).astype(BF16)
        s_old = s_scr[hd]
        sbound_ref[0, 0, hd] = s_old
        s_scr[hd] = tile_dec[hd:hd + 1, :] * s_old + _dot(vtd, ks[hd])

    g = proj(3)
    gate_ref[0] = (g * _sigmoid(g)).astype(gate_ref.dtype)

    u = proj(5) * proj(6)
    col = lax.broadcasted_iota(jnp.int32, (tm, 1), 0) % GRID_W
    up = jnp.where(col == 0, 0.0, pltpu.roll(u, 1, 0))
    un = jnp.where(col == GRID_W - 1, 0.0, pltpu.roll(u, tm - 1, 0))
    y = proj(4) * (cw_ref[0:1, :] * up + cw_ref[1:2, :] * u + cw_ref[2:3, :] * un)
    conv_ref[0] = y.astype(conv_ref.dtype)


def _inproj_call(x, mod, norm_g, w_in, cos_t, sin_t, conv_w, dec_rows, sb0, w_out):
    b, n, d = x.shape
    tm = INPROJ_TILE
    nt = n // tm
    r = RET_WIDTH
    slab = w_out.shape[0] // (b * nt)
    assert slab * b * nt == w_out.shape[0] and slab % 16 == 0
    slab_spec = pl.BlockSpec((slab, w_out.shape[1]), lambda bi, i: (bi * nt + i, 0))

    def tok(bi, i):
        return (bi, nt - 1 - i, 0)

    seq = jax.ShapeDtypeStruct((b, n, r), BF16)
    seq_spec = pl.BlockSpec((1, tm, r), tok)
    heads_t = jax.ShapeDtypeStruct((b, RET_HEADS, HEAD_DIM, n), BF16)
    heads_t_spec = pl.BlockSpec((1, RET_HEADS, HEAD_DIM, tm), lambda bi, i: (bi, 0, 0, nt - 1 - i))
    return pl.pallas_call(
        _inproj_kernel,
        grid=(b, nt),
        in_specs=[
            pl.BlockSpec((1, tm, d), tok),
            pl.BlockSpec((MOD_ROWS, d), lambda bi, i: (0, 0)),
            pl.BlockSpec((MOD_ROWS, d), lambda bi, i: (0, 1)),
            pl.BlockSpec((1, d), lambda bi, i: (0, 0)),
            pl.BlockSpec(w_in.shape, lambda bi, i: (0, 0), pipeline_mode=pl.Buffered(1)),
            pl.BlockSpec((tm, HEAD_DIM), lambda bi, i: (nt - 1 - i, 0)),
            pl.BlockSpec((tm, HEAD_DIM), lambda bi, i: (nt - 1 - i, 0)),
            pl.BlockSpec((3, CONV_CH), lambda bi, i: (0, 0)),
            pl.BlockSpec(dec_rows.shape, lambda bi, i: (0, 0)),
            pl.BlockSpec((1, RET_HEADS, HEAD_DIM, HEAD_DIM), lambda bi, i: (bi, 0, 0, 0)),
            slab_spec,
        ],
        out_specs=[
            seq_spec, seq_spec, heads_t_spec, seq_spec, seq_spec,
            pl.BlockSpec((1, 1, RET_HEADS, HEAD_DIM, HEAD_DIM),
                         lambda bi, i: (bi, nt - 1 - i, 0, 0, 0)),
            slab_spec,
        ],
        out_shape=[seq, seq, heads_t, seq, seq,
                   jax.ShapeDtypeStruct((b, nt, RET_HEADS, HEAD_DIM, HEAD_DIM), F32),
                   jax.ShapeDtypeStruct(w_out.shape, BF16)],
        scratch_shapes=[
            pltpu.VMEM((RET_HEADS, HEAD_DIM, HEAD_DIM), F32),
            pltpu.VMEM((RET_HEADS, tm), F32),
        ],
        compiler_params=pltpu.CompilerParams(
            dimension_semantics=("arbitrary", "arbitrary"),
            vmem_limit_bytes=VMEM_LIMIT),
        name="inproj",
    )(x, mod, mod, norm_g, w_in, cos_t, sin_t, conv_w, dec_rows, sb0, w_out)


def _mix_kernel(q_ref, k_ref, vt_ref, gate_ref, conv_ref, x_ref, sbound_ref, sf0_ref,
                wo_r_ref, wo_c_ref, g1_ref, dec_ref, wup_ref, wdn_ref,
                o_ref, wup_b_ref, wdn_b_ref,
                sf_scr, dmat_scr, qdec_scr, kdec_scr, cdec_scr, ret_scr):
    b = pl.program_id(0)
    i = pl.program_id(1)
    tm = ret_scr.shape[0]
    c = dmat_scr.shape[1]
    nc = tm // c

    wup_b_ref[...] = wup_ref[...].astype(wup_b_ref.dtype)
    wdn_b_ref[...] = wdn_ref[...].astype(wdn_b_ref.dtype)

    @pl.when(i == 0)
    def _():
        sf_scr[...] = sf0_ref[0]

    @pl.when((b == 0) & (i == 0))
    def _():
        lgf = _log_sigmoid(dec_ref[:RET_HEADS])
        lgb = _log_sigmoid(dec_ref[RET_HEADS:])
        ri = lax.broadcasted_iota(jnp.int32, (c, c), 0)
        ci = lax.broadcasted_iota(jnp.int32, (c, c), 1)
        rel = (ri - ci).astype(F32)
        t = lax.broadcasted_iota(jnp.int32, (c, HEAD_DIM), 0).astype(F32)
        tl = lax.broadcasted_iota(jnp.int32, (1, c), 1).astype(F32)
        for hd in range(RET_HEADS):
            f = lgf[hd:hd + 1, :]
            g = lgb[hd:hd + 1, :]
            dmat_scr[hd] = jnp.where(rel >= 0, jnp.exp(f * jnp.maximum(rel, 0.0)),
                                     jnp.exp(g * jnp.maximum(-rel, 0.0))).astype(BF16)
            fh = f[:, :HEAD_DIM]
            gh = g[:, :HEAD_DIM]
            qdec_scr[hd, :, :HEAD_DIM] = jnp.exp(fh * (t + 1.0)).astype(BF16)
            qdec_scr[hd, :, HEAD_DIM:] = jnp.exp(gh * (float(c) - t)).astype(BF16)
            kdec_scr[hd, 0:1, :] = jnp.exp(f * (float(c) - 1.0 - tl))
            kdec_scr[hd, 1:2, :] = jnp.exp(g * tl)
            cdec_scr[hd, :, :HEAD_DIM] = jnp.broadcast_to(jnp.exp(fh * float(c)), (8, HEAD_DIM))
            cdec_scr[hd, :, HEAD_DIM:] = jnp.broadcast_to(jnp.exp(gh * float(c)), (8, HEAD_DIM))

    heads = range(RET_HEADS)
    hsl = [slice(hd * HEAD_DIM, (hd + 1) * HEAD_DIM) for hd in heads]
    crows = [slice(ch * c, (ch + 1) * c) for ch in range(nc)]
    ks = [[k_ref[0, crows[ch], hsl[hd]] for ch in range(nc)] for hd in heads]
    vts = [[vt_ref[0, hd, :, crows[ch]] for ch in range(nc)] for hd in heads]

    qs = [[q_ref[0, crows[ch], hsl[hd]] for ch in range(nc)] for hd in heads]

    def masked_scores(ch):
        return [_dot_nt(qs[hd][ch], ks[hd][ch]).astype(BF16) * dmat_scr[hd] for hd in heads]

    ps_first = masked_scores(0)

    s2ts = []
    for hd in heads:
        cf = cdec_scr[hd, 0:1, :HEAD_DIM]
        cb = cdec_scr[hd, 0:1, HEAD_DIM:]
        sb = [None] * nc
        sb[nc - 1] = sbound_ref[0, 0, hd]
        for ch in range(nc - 1, 0, -1):
            vb = vts[hd][ch] * kdec_scr[hd, 1:2, :].astype(BF16)
            sb[ch - 1] = cb * sb[ch] + _dot(vb, ks[hd][ch])
        sf = [sf_scr[hd]]
        for ch in range(nc):
            vf = vts[hd][ch] * kdec_scr[hd, 0:1, :].astype(BF16)
            sf.append(cf * sf[ch] + _dot(vf, ks[hd][ch]))
        sf_scr[hd] = sf[nc]
        s2ts.append([jnp.concatenate([sf[ch], sb[ch]], axis=1).astype(BF16) for ch in range(nc)])

    def cross_terms(ch):
        out = []
        for hd in heads:
            qh = qs[hd][ch]
            q2 = jnp.concatenate([qh, qh], axis=1) * qdec_scr[hd]
            out.append(_dot_nt(s2ts[hd][ch], q2))
        return out

    crosses = [cross_terms(ch) for ch in range(nc)]
    ps = ps_first
    for ch in range(nc):
        inner = [_dot_nt(vts[hd][ch], ps[hd]) for hd in heads]
        cross = crosses[ch]
        for hd in heads:
            o = (inner[hd] + cross[hd]).T
            ms = jnp.mean(o * o, axis=-1, keepdims=True)
            o = o * lax.rsqrt(ms + EPS) * gate_ref[0, crows[ch], hsl[hd]].astype(F32)
            ret_scr[crows[ch], hsl[hd]] = o.astype(BF16)
        if ch + 1 < nc:
            ps = masked_scores(ch + 1)
        mix = _dot(ret_scr[crows[ch], :], wo_r_ref[...]) + _dot(conv_ref[0, crows[ch], :], wo_c_ref[...])
        o_ref[0, crows[ch], :] = x_ref[0, crows[ch], :] + g1_ref[pl.ds(b, 1), :] * mix


def _mix_call(q, k, vt, gate, conv, x, sbound, sf0, w_out, mod, dec_rows, w_up, w_down):
    b, n, d = x.shape
    tm = MIX_TILE
    nt = n // tm
    bound_per_tile = sbound.shape[1] // nt
    r = RET_WIDTH
    c = RET_CHUNK

    def slab_spec_of(w):
        slab = w.shape[0] // (b * nt)
        assert slab * b * nt == w.shape[0] and slab % 16 == 0
        return pl.BlockSpec((slab, w.shape[1]), lambda bi, i: (bi * nt + i, 0))

    def tok(bi, i):
        return (bi, i, 0)

    seq_spec = pl.BlockSpec((1, tm, r), tok)
    return pl.pallas_call(
        _mix_kernel,
        grid=(b, nt),
        in_specs=[
            seq_spec, seq_spec,
            pl.BlockSpec((1, RET_HEADS, HEAD_DIM, tm), lambda bi, i: (bi, 0, 0, i)),
            seq_spec, seq_spec,
            pl.BlockSpec((1, tm, d), tok),
            pl.BlockSpec((1, 1, RET_HEADS, HEAD_DIM, HEAD_DIM),
                         lambda bi, i: (bi, (i + 1) * bound_per_tile - 1, 0, 0, 0)),
            pl.BlockSpec((1, RET_HEADS, HEAD_DIM, HEAD_DIM), lambda bi, i: (bi, 0, 0, 0)),
            pl.BlockSpec((r, d), lambda bi, i: (0, 0)),
            pl.BlockSpec((CONV_CH, d), lambda bi, i: (1, 0)),
            pl.BlockSpec((MOD_ROWS, d), lambda bi, i: (0, 2)),
            pl.BlockSpec(dec_rows.shape, lambda bi, i: (0, 0)),
            slab_spec_of(w_up),
            slab_spec_of(w_down),
        ],
        out_specs=[pl.BlockSpec((1, tm, d), tok), slab_spec_of(w_up), slab_spec_of(w_down)],
        out_shape=[jax.ShapeDtypeStruct((b, n, d), F32),
                   jax.ShapeDtypeStruct(w_up.shape, BF16),
                   jax.ShapeDtypeStruct(w_down.shape, BF16)],
        scratch_shapes=[
            pltpu.VMEM((RET_HEADS, HEAD_DIM, HEAD_DIM), F32),
            pltpu.VMEM((RET_HEADS, c, c), BF16),
            pltpu.VMEM((RET_HEADS, c, 2 * HEAD_DIM), BF16),
            pltpu.VMEM((RET_HEADS, 8, c), F32),
            pltpu.VMEM((RET_HEADS, 8, 2 * HEAD_DIM), F32),
            pltpu.VMEM((tm, r), BF16),
        ],
        compiler_params=pltpu.CompilerParams(
            dimension_semantics=("arbitrary", "arbitrary"), vmem_limit_bytes=VMEM_LIMIT),
        name="mix",
    )(q, k, vt, gate, conv, x, sbound, sf0, w_out, w_out, mod, dec_rows, w_up, w_down)


def _ffn_kernel(x_ref, sh_ref, sc_ref, gt_ref, g2_ref, gf_ref,
                wa_ref, wb_ref, cw_ref, cb_ref, wd_ref,
                o_ref,
                h_scr):
    b = pl.program_id(0)
    f = pl.program_id(2)
    nf = pl.num_programs(2)
    tm, d = h_scr.shape
    tf = wa_ref.shape[1]
    blk = o_ref.shape[1:]
    step = blk[1]

    def body(first, last):
        if first:
            h = _norm_mod(x_ref[0].reshape(tm, d), g2_ref[...], sh_ref[pl.ds(b, 1), :],
                          sc_ref[pl.ds(b, 1), :]).astype(BF16)
            h_scr[...] = h
        else:
            h = h_scr[...]
        a = _dot(h, wa_ref[...])
        zero = jnp.zeros((step, tf), F32)
        a_up = jnp.concatenate([zero, a[:tm - step]], axis=0)
        a_dn = jnp.concatenate([a[step:], zero], axis=0)
        a = cw_ref[0:1, :] * a_up + cw_ref[1:2, :] * a + cw_ref[2:3, :] * a_dn + cb_ref[...]
        act = (a * _sigmoid(a) * _dot(h, wb_ref[...])).astype(BF16)
        if not last:
            y = _dot(act, wd_ref[...])
            o_ref[0] = (y if first else o_ref[0].reshape(tm, d) + y).reshape(blk)
            return
        groups = LAST_STEP_ROW_GROUPS
        gr = blk[0] // groups
        ys = [_dot(act[g * gr * step:(g + 1) * gr * step], wd_ref[...]) for g in range(groups)]
        for g in range(groups):
            rs = slice(g * gr, (g + 1) * gr)
            acc = o_ref[0, rs].reshape(gr * step, d) + ys[g]
            z = x_ref[0, rs].reshape(gr * step, d) + gt_ref[pl.ds(b, 1), :] * acc
            ms = jnp.mean(z * z, axis=-1, keepdims=True)
            o_ref[0, rs] = (z * lax.rsqrt(ms + EPS) * gf_ref[...]).reshape(gr, step, d)

    @pl.when(f == 0)
    def _():
        body(True, False)

    @pl.when((f > 0) & (f < nf - 1))
    def _():
        body(False, False)

    @pl.when(f == nf - 1)
    def _():
        body(False, True)


def _ffn_call(x, mod, norm_g, final_g, w_up, ffn_conv_w, ffn_conv_b, w_down):
    b, n, d = x.shape
    d_ff = w_down.shape[0]
    rows = n // GRID_W
    cols = FFN_COLS
    tf = FF_TILE
    nf = d_ff // tf
    assert nf >= 2
    xg = x.reshape(b, rows, GRID_W, d)
    tile = pl.BlockSpec((1, rows, cols, d), lambda bi, i, f: (bi, 0, i, 0))

    out = pl.pallas_call(
        _ffn_kernel,
        grid=(b, GRID_W // cols, nf),
        in_specs=[
            tile,
            pl.BlockSpec((MOD_ROWS, d), lambda bi, i, f: (0, 3)),
            pl.BlockSpec((MOD_ROWS, d), lambda bi, i, f: (0, 4)),
            pl.BlockSpec((MOD_ROWS, d), lambda bi, i, f: (0, 5)),
            pl.BlockSpec((1, d), lambda bi, i, f: (0, 0)),
            pl.BlockSpec((1, d), lambda bi, i, f: (0, 0)),
            pl.BlockSpec((d, tf), lambda bi, i, f: (0, f)),
            pl.BlockSpec((d, tf), lambda bi, i, f: (0, nf + f)),
            pl.BlockSpec((3, tf), lambda bi, i, f: (0, f)),
            pl.BlockSpec((1, tf), lambda bi, i, f: (0, f)),
            pl.BlockSpec((tf, d), lambda bi, i, f: (f, 0)),
        ],
        out_specs=tile,
        out_shape=jax.ShapeDtypeStruct((b, rows, GRID_W, d), F32),
        scratch_shapes=[
            pltpu.VMEM((rows * cols, d), BF16),
        ],
        compiler_params=pltpu.CompilerParams(
            dimension_semantics=("arbitrary", "arbitrary", "arbitrary"),
            vmem_limit_bytes=VMEM_LIMIT),
        name="ffn",
    )(xg, mod, mod, mod, norm_g, final_g, w_up, w_up, ffn_conv_w, ffn_conv_b, w_down)
    return out.reshape(b, n, d)


def _rope_tables(n):
    pos = np.arange(n, dtype=np.int32)
    row = (pos // GRID_W).astype(np.float32)
    col = (pos % GRID_W).astype(np.float32)
    dh = HEAD_DIM // 2
    freqs = np.float32(ROPE_THETA) ** (-np.arange(0, dh, 2, dtype=np.float32) / np.float32(dh))
    parts_c, parts_s = [], []
    for p in (row, col):
        ang = (p[:, None] * freqs[None, :]).astype(np.float32)
        cos = np.cos(ang).astype(np.float32)
        sin = np.sin(ang).astype(np.float32)
        parts_c += [cos, cos]
        parts_s += [-sin, sin]
    return (jnp.asarray(np.concatenate(parts_c, axis=-1)),
            jnp.asarray(np.concatenate(parts_s, axis=-1)))


def kernel(x, c, ctx, c_ctx, w_mod, b_mod, norm1_g, w_in, ret_decay_fwd, ret_decay_bwd,
           conv_w, w_out, norm2_g, w_up, ffn_conv_w, ffn_conv_b, w_down, final_g):
    b, n, d = x.shape
    depth = w_mod.shape[0]
    assert depth == 1 and n % MIX_TILE == 0 and n % GRID_W == 0
    assert MIX_TILE % INPROJ_TILE == 0 and MIX_TILE % RET_CHUNK == 0
    cos_t, sin_t = _rope_tables(n)
    craw = jnp.concatenate(
        [c, c_ctx[None, :], jnp.zeros((MOD_ROWS - b - 1, d), F32)], axis=0)

    layer = 0
    mod, w_in_b = _mod_call(craw, w_mod[layer], b_mod[layer][None, :], w_in[layer])
    dec_rows = jnp.broadcast_to(
        jnp.concatenate([ret_decay_fwd[layer], ret_decay_bwd[layer]])[:, None],
        (2 * RET_HEADS, RET_CHUNK))
    g1 = norm1_g[layer][None, :]

    sf0, sb0 = _ctx_call(ctx, mod, g1, w_in_b, dec_rows)
    q, k, vt, gate, conv, sbound, w_out_b = _inproj_call(
        x, mod, g1, w_in_b, cos_t, sin_t, conv_w[layer], dec_rows, sb0, w_out[layer])
    x_mid, w_up_b, w_down_b = _mix_call(q, k, vt, gate, conv, x, sbound, sf0, w_out_b, mod,
                                        dec_rows, w_up[layer], w_down[layer])
    return _ffn_call(x_mid, mod, norm2_g[layer][None, :], final_g[None, :], w_up_b,
                     ffn_conv_w[layer], ffn_conv_b[layer][None, :], w_down_b)
```

```python
import functools

import jax
import jax.numpy as jnp
import numpy as np
from jax import lax
from jax.experimental import pallas as pl
from jax.experimental.pallas import tpu as pltpu

F32 = jnp.float32
BF16 = jnp.bfloat16

GRID_W = 64
RET_HEADS = 8
HEAD_DIM = 128
RET_WIDTH = RET_HEADS * HEAD_DIM
CONV_CH = 1024
EPS = 1e-6
ROPE_THETA = 10000.0

MOD_ROWS = 8
INPROJ_TILE = 512
MIX_TILE = 512
FFN_COLS = 8
RET_CHUNK = 256
FF_TILE = 512
MOD_COL_TILE = 768
LAST_STEP_ROW_GROUPS = 4
V7X_VMEM_BYTES = 64 * 1024 * 1024
VMEM_LIMIT = V7X_VMEM_BYTES - 4 * 1024 * 1024


def _sigmoid(x):
    return 1.0 / (1.0 + jnp.exp(-x))


def _log_sigmoid(x):
    return jnp.minimum(x, 0.0) - jnp.log(1.0 + jnp.exp(-jnp.abs(x)))


def _norm_mod(x, gain, shift, scale):
    ms = jnp.mean(x * x, axis=-1, keepdims=True)
    y = x * lax.rsqrt(ms + EPS) * gain
    return y * (1.0 + scale) + shift


def _dot(a, b):
    return jnp.dot(a, b, preferred_element_type=F32)


def _dot_t0(a, b):
    return lax.dot_general(a, b, (((0,), (0,)), ((), ())), preferred_element_type=F32)


def _dot_nt(a, b):
    return lax.dot_general(a, b, (((1,), (1,)), ((), ())), preferred_element_type=F32)


def _head_lanes(rows):
    return jnp.concatenate([rows[hd:hd + 1, :HEAD_DIM] for hd in range(RET_HEADS)], axis=1)


def _mod_ctx_kernel(ctx_row, c_ref, w_ref, b_ref, win_ref, ctx_ref, g_ref, dec_ref,
                    o_ref, win_b_ref, sf_ref, sb_ref,
                    modc_scr, hc_scr, kv_scr):
    j = pl.program_id(0)
    steps = pl.num_programs(0)
    first_slab_step = steps - hc_scr.shape[0]
    n_keep, _, tn = modc_scr.shape
    nb, length, d = ctx_ref.shape
    slab = win_ref.shape[0]
    r = RET_WIDTH

    a = c_ref[...]
    a = (a * _sigmoid(a)).astype(BF16)
    out = _dot(a, w_ref[...].astype(BF16)) + b_ref[...]
    o_ref[...] = out

    @pl.when(j < n_keep)
    def _():
        modc_scr[j] = out

    @pl.when(j == first_slab_step)
    def _():
        row = jnp.concatenate([modc_scr[t][ctx_row:ctx_row + 1, :] for t in range(n_keep)], axis=1)
        x = ctx_ref[...].reshape(nb * length, d)
        hc = _norm_mod(x, g_ref[...], row[:, 0:d], row[:, d:2 * d]).astype(BF16)
        for t in range(hc_scr.shape[0]):
            hc_scr[t] = hc[:, t * slab:(t + 1) * slab]
        kv_scr[...] = jnp.zeros_like(kv_scr)

    @pl.when(j >= first_slab_step)
    def _():
        wb = win_ref[...].astype(BF16)
        win_b_ref[...] = wb
        kv_scr[...] += _dot(hc_scr[j - first_slab_step], wb[:, r:3 * r])

    @pl.when(j == steps - 1)
    def _():
        k = kv_scr[:, 0:r] * (HEAD_DIM ** -0.5)
        v = kv_scr[:, r:2 * r].astype(BF16)
        t = (lax.broadcasted_iota(jnp.int32, k.shape, 0) % length).astype(F32)
        lg = _log_sigmoid(dec_ref[...])
        kf = (k * jnp.exp(_head_lanes(lg[:RET_HEADS]) * (length - 1.0 - t))).astype(BF16)
        kb = (k * jnp.exp(_head_lanes(lg[RET_HEADS:]) * t)).astype(BF16)
        for bi in range(nb):
            rows = slice(bi * length, (bi + 1) * length)
            for hd in range(RET_HEADS):
                sl = slice(hd * HEAD_DIM, (hd + 1) * HEAD_DIM)
                sf_ref[bi, hd] = _dot_t0(v[rows, sl], kf[rows, sl])
                sb_ref[bi, hd] = _dot_t0(v[rows, sl], kb[rows, sl])


def _mod_ctx_call(craw, w_mod, b_mod, w_in, ctx, norm_g, dec_rows, ctx_row):
    d, n = w_mod.shape
    b, length, _ = ctx.shape
    tn = MOD_COL_TILE
    steps = n // tn
    n_slabs = steps // 2
    slab = w_in.shape[0] // n_slabs
    n_keep = -(-2 * d // tn)
    assert steps * tn == n and slab * n_slabs == w_in.shape[0] and slab % 128 == 0
    assert n_keep <= steps - n_slabs

    def slab_idx(j):
        return (jnp.maximum(j - (steps - n_slabs), 0), 0)

    slab_spec = pl.BlockSpec((slab, w_in.shape[1]), slab_idx)
    state = jax.ShapeDtypeStruct((b, RET_HEADS, HEAD_DIM, HEAD_DIM), F32)
    state_spec = pl.BlockSpec(state.shape, lambda j: (0, 0, 0, 0))
    return pl.pallas_call(
        functools.partial(_mod_ctx_kernel, ctx_row),
        grid=(steps,),
        in_specs=[
            pl.BlockSpec((MOD_ROWS, d), lambda j: (0, 0)),
            pl.BlockSpec((d, tn), lambda j: (0, j)),
            pl.BlockSpec((1, tn), lambda j: (0, j)),
            slab_spec,
            pl.BlockSpec((b, length, d), lambda j: (0, 0, 0)),
            pl.BlockSpec((1, d), lambda j: (0, 0)),
            pl.BlockSpec(dec_rows.shape, lambda j: (0, 0)),
        ],
        out_specs=[pl.BlockSpec((MOD_ROWS, tn), lambda j: (0, j)), slab_spec, state_spec, state_spec],
        out_shape=[jax.ShapeDtypeStruct((MOD_ROWS, n), F32),
                   jax.ShapeDtypeStruct(w_in.shape, BF16), state, state],
        scratch_shapes=[
            pltpu.VMEM((n_keep, MOD_ROWS, tn), F32),
            pltpu.VMEM((n_slabs, b * length, slab), BF16),
            pltpu.VMEM((b * length, 2 * RET_WIDTH), F32),
        ],
        compiler_params=pltpu.CompilerParams(
            dimension_semantics=("arbitrary",), vmem_limit_bytes=VMEM_LIMIT),
        name="mod_ctx",
    )(craw, w_mod, b_mod, w_in, ctx, norm_g, dec_rows)


def _rope(acc, cos, sin, scale):
    lane = lax.broadcasted_iota(jnp.int32, (1, HEAD_DIM), 1)
    lo = (lane % 64) < 32
    outs = []
    for hd in range(RET_HEADS):
        xh = acc[:, hd * HEAD_DIM:(hd + 1) * HEAD_DIM]
        partner = jnp.where(lo, pltpu.roll(xh, 96, 1), pltpu.roll(xh, 32, 1))
        y = xh * cos + partner * sin
        outs.append(y if scale == 1.0 else y * scale)
    return outs


def _inproj_kernel(x_ref, sh_ref, sc_ref, g_ref, w_ref, cos_ref, sin_ref, cw_ref, dec_ref,
                   sb0_ref, wout_ref,
                   q_ref, k_ref, vt_ref, gate_ref, conv_ref, sbound_ref, wout_b_ref,
                   s_scr, dec_scr):
    b = pl.program_id(0)
    i = pl.program_id(1)
    tm = dec_scr.shape[1]
    r = RET_WIDTH

    wout_b_ref[...] = wout_ref[...].astype(wout_b_ref.dtype)

    @pl.when(i == 0)
    def _():
        s_scr[...] = sb0_ref[0]
        t = lax.broadcasted_iota(jnp.int32, dec_scr.shape, 1).astype(F32)
        lgb = _log_sigmoid(dec_ref[RET_HEADS:, 0:1])
        dec_scr[...] = jnp.exp(lgb * t)

    h = _norm_mod(x_ref[0], g_ref[...], sh_ref[pl.ds(b, 1), :], sc_ref[pl.ds(b, 1), :]).astype(BF16)

    def proj(group):
        return _dot(h, w_ref[:, group * r:(group + 1) * r])

    cos = cos_ref[...]
    sin = sin_ref[...]
    for hd, qh in enumerate(_rope(proj(0), cos, sin, 1.0)):
        q_ref[0, :, hd * HEAD_DIM:(hd + 1) * HEAD_DIM] = qh.astype(q_ref.dtype)
    ks = [kh.astype(BF16) for kh in _rope(proj(1), cos, sin, HEAD_DIM ** -0.5)]
    for hd in range(RET_HEADS):
        k_ref[0, :, hd * HEAD_DIM:(hd + 1) * HEAD_DIM] = ks[hd]
    v = proj(2)
    tile_dec = jnp.exp(_log_sigmoid(dec_ref[RET_HEADS:, :HEAD_DIM]) * float(tm))
    for hd in range(RET_HEADS):
        vt = v[:, hd * HEAD_DIM:(hd + 1) * HEAD_DIM].T
        vt_ref[0, hd] = vt.astype(vt_ref.dtype)
        vtd = (vt * dec_scr[hd:hd + 1, :]).astype(BF16)
        s_old = s_scr[hd]
        sbound_ref[0, 0, hd] = s_old
        s_scr[hd] = tile_dec[hd:hd + 1, :] * s_old + _dot(vtd, ks[hd])

    g = proj(3)
    gate_ref[0] = (g * _sigmoid(g)).astype(gate_ref.dtype)

    u = proj(5) * proj(6)
    col = lax.broadcasted_iota(jnp.int32, (tm, 1), 0) % GRID_W
    up = jnp.where(col == 0, 0.0, pltpu.roll(u, 1, 0))
    un = jnp.where(col == GRID_W - 1, 0.0, pltpu.roll(u, tm - 1, 0))
    y = proj(4) * (cw_ref[0:1, :] * up + cw_ref[1:2, :] * u + cw_ref[2:3, :] * un)
    conv_ref[0] = y.astype(conv_ref.dtype)


def _inproj_call(x, mod, norm_g, w_in, cos_t, sin_t, conv_w, dec_rows, sb0, w_out):
    b, n, d = x.shape
    tm = INPROJ_TILE
    nt = n // tm
    r = RET_WIDTH
    slab = w_out.shape[0] // (b * nt)
    assert slab * b * nt == w_out.shape[0] and slab % 16 == 0
    slab_spec = pl.BlockSpec((slab, w_out.shape[1]), lambda bi, i: (bi * nt + i, 0))

    def tok(bi, i):
        return (bi, nt - 1 - i, 0)

    seq = jax.ShapeDtypeStruct((b, n, r), BF16)
    seq_spec = pl.BlockSpec((1, tm, r), tok)
    heads_t = jax.ShapeDtypeStruct((b, RET_HEADS, HEAD_DIM, n), BF16)
    heads_t_spec = pl.BlockSpec((1, RET_HEADS, HEAD_DIM, tm), lambda bi, i: (bi, 0, 0, nt - 1 - i))
    return pl.pallas_call(
        _inproj_kernel,
        grid=(b, nt),
        in_specs=[
            pl.BlockSpec((1, tm, d), tok),
            pl.BlockSpec((MOD_ROWS, d), lambda bi, i: (0, 0)),
            pl.BlockSpec((MOD_ROWS, d), lambda bi, i: (0, 1)),
            pl.BlockSpec((1, d), lambda bi, i: (0, 0)),
            pl.BlockSpec(w_in.shape, lambda bi, i: (0, 0), pipeline_mode=pl.Buffered(1)),
            pl.BlockSpec((tm, HEAD_DIM), lambda bi, i: (nt - 1 - i, 0)),
            pl.BlockSpec((tm, HEAD_DIM), lambda bi, i: (nt - 1 - i, 0)),
            pl.BlockSpec((3, CONV_CH), lambda bi, i: (0, 0)),
            pl.BlockSpec(dec_rows.shape, lambda bi, i: (0, 0)),
            pl.BlockSpec((1, RET_HEADS, HEAD_DIM, HEAD_DIM), lambda bi, i: (bi, 0, 0, 0)),
            slab_spec,
        ],
        out_specs=[
            seq_spec, seq_spec, heads_t_spec, seq_spec, seq_spec,
            pl.BlockSpec((1, 1, RET_HEADS, HEAD_DIM, HEAD_DIM),
                         lambda bi, i: (bi, nt - 1 - i, 0, 0, 0)),
            slab_spec,
        ],
        out_shape=[seq, seq, heads_t, seq, seq,
                   jax.ShapeDtypeStruct((b, nt, RET_HEADS, HEAD_DIM, HEAD_DIM), F32),
                   jax.ShapeDtypeStruct(w_out.shape, BF16)],
        scratch_shapes=[
            pltpu.VMEM((RET_HEADS, HEAD_DIM, HEAD_DIM), F32),
            pltpu.VMEM((RET_HEADS, tm), F32),
        ],
        compiler_params=pltpu.CompilerParams(
            dimension_semantics=("arbitrary", "arbitrary"),
            vmem_limit_bytes=VMEM_LIMIT),
        name="inproj",
    )(x, mod, mod, norm_g, w_in, cos_t, sin_t, conv_w, dec_rows, sb0, w_out)


def _mix_kernel(q_ref, k_ref, vt_ref, gate_ref, conv_ref, x_ref, sbound_ref, sf0_ref,
                wo_r_ref, wo_c_ref, g1_ref, dec_ref, wup_ref, wdn_ref,
                o_ref, wup_b_ref, wdn_b_ref,
                sf_scr, dmat_scr, qdec_scr, kdec_scr, cdec_scr, ret_scr):
    b = pl.program_id(0)
    i = pl.program_id(1)
    tm = ret_scr.shape[0]
    c = dmat_scr.shape[1]
    nc = tm // c

    wup_b_ref[...] = wup_ref[...].astype(wup_b_ref.dtype)
    wdn_b_ref[...] = wdn_ref[...].astype(wdn_b_ref.dtype)

    @pl.when(i == 0)
    def _():
        sf_scr[...] = sf0_ref[0]

    @pl.when((b == 0) & (i == 0))
    def _():
        lgf = _log_sigmoid(dec_ref[:RET_HEADS])
        lgb = _log_sigmoid(dec_ref[RET_HEADS:])
        ri = lax.broadcasted_iota(jnp.int32, (c, c), 0)
        ci = lax.broadcasted_iota(jnp.int32, (c, c), 1)
        rel = (ri - ci).astype(F32)
        t = lax.broadcasted_iota(jnp.int32, (c, HEAD_DIM), 0).astype(F32)
        tl = lax.broadcasted_iota(jnp.int32, (1, c), 1).astype(F32)
        for hd in range(RET_HEADS):
            f = lgf[hd:hd + 1, :]
            g = lgb[hd:hd + 1, :]
            dmat_scr[hd] = jnp.where(rel >= 0, jnp.exp(f * jnp.maximum(rel, 0.0)),
                                     jnp.exp(g * jnp.maximum(-rel, 0.0))).astype(BF16)
            fh = f[:, :HEAD_DIM]
            gh = g[:, :HEAD_DIM]
            qdec_scr[hd, :, :HEAD_DIM] = jnp.exp(fh * (t + 1.0)).astype(BF16)
            qdec_scr[hd, :, HEAD_DIM:] = jnp.exp(gh * (float(c) - t)).astype(BF16)
            kdec_scr[hd, 0:1, :] = jnp.exp(f * (float(c) - 1.0 - tl))
            kdec_scr[hd, 1:2, :] = jnp.exp(g * tl)
            cdec_scr[hd, :, :HEAD_DIM] = jnp.broadcast_to(jnp.exp(fh * float(c)), (8, HEAD_DIM))
            cdec_scr[hd, :, HEAD_DIM:] = jnp.broadcast_to(jnp.exp(gh * float(c)), (8, HEAD_DIM))

    heads = range(RET_HEADS)
    hsl = [slice(hd * HEAD_DIM, (hd + 1) * HEAD_DIM) for hd in heads]
    crows = [slice(ch * c, (ch + 1) * c) for ch in range(nc)]
    ks = [[k_ref[0, crows[ch], hsl[hd]] for ch in range(nc)] for hd in heads]
    vts = [[vt_ref[0, hd, :, crows[ch]] for ch in range(nc)] for hd in heads]

    qs = [[q_ref[0, crows[ch], hsl[hd]] for ch in range(nc)] for hd in heads]

    def masked_scores(ch):
        return [_dot_nt(qs[hd][ch], ks[hd][ch]).astype(BF16) * dmat_scr[hd] for hd in heads]

    ps_first = masked_scores(0)

    s2ts = []
    for hd in heads:
        cf = cdec_scr[hd, 0:1, :HEAD_DIM]
        cb = cdec_scr[hd, 0:1, HEAD_DIM:]
        sb = [None] * nc
        sb[nc - 1] = sbound_ref[0, 0, hd]
        for ch in range(nc - 1, 0, -1):
            vb = vts[hd][ch] * kdec_scr[hd, 1:2, :].astype(BF16)
            sb[ch - 1] = cb * sb[ch] + _dot(vb, ks[hd][ch])
        sf = [sf_scr[hd]]
        for ch in range(nc):
            vf = vts[hd][ch] * kdec_scr[hd, 0:1, :].astype(BF16)
            sf.append(cf * sf[ch] + _dot(vf, ks[hd][ch]))
        sf_scr[hd] = sf[nc]
        s2ts.append([jnp.concatenate([sf[ch], sb[ch]], axis=1).astype(BF16) for ch in range(nc)])

    def cross_terms(ch):
        out = []
        for hd in heads:
            qh = qs[hd][ch]
            q2 = jnp.concatenate([qh, qh], axis=1) * qdec_scr[hd]
            out.append(_dot_nt(s2ts[hd][ch], q2))
        return out

    crosses = [cross_terms(ch) for ch in range(nc)]
    ps = ps_first
    for ch in range(nc):
        inner = [_dot_nt(vts[hd][ch], ps[hd]) for hd in heads]
        cross = crosses[ch]
        for hd in heads:
            o = (inner[hd] + cross[hd]).T
            ms = jnp.mean(o * o, axis=-1, keepdims=True)
            o = o * lax.rsqrt(ms + EPS) * gate_ref[0, crows[ch], hsl[hd]].astype(F32)
            ret_scr[crows[ch], hsl[hd]] = o.astype(BF16)
        if ch + 1 < nc:
            ps = masked_scores(ch + 1)
        mix = _dot(ret_scr[crows[ch], :], wo_r_ref[...]) + _dot(conv_ref[0, crows[ch], :], wo_c_ref[...])
        o_ref[0, crows[ch], :] = x_ref[0, crows[ch], :] + g1_ref[pl.ds(b, 1), :] * mix


def _mix_call(q, k, vt, gate, conv, x, sbound, sf0, w_out, mod, dec_rows, w_up, w_down):
    b, n, d = x.shape
    tm = MIX_TILE
    nt = n // tm
    bound_per_tile = sbound.shape[1] // nt
    r = RET_WIDTH
    c = RET_CHUNK

    def slab_spec_of(w):
        slab = w.shape[0] // (b * nt)
        assert slab * b * nt == w.shape[0] and slab % 16 == 0
        return pl.BlockSpec((slab, w.shape[1]), lambda bi, i: (bi * nt + i, 0))

    def tok(bi, i):
        return (bi, i, 0)

    seq_spec = pl.BlockSpec((1, tm, r), tok)
    return pl.pallas_call(
        _mix_kernel,
        grid=(b, nt),
        in_specs=[
            seq_spec, seq_spec,
            pl.BlockSpec((1, RET_HEADS, HEAD_DIM, tm), lambda bi, i: (bi, 0, 0, i)),
            seq_spec, seq_spec,
            pl.BlockSpec((1, tm, d), tok),
            pl.BlockSpec((1, 1, RET_HEADS, HEAD_DIM, HEAD_DIM),
                         lambda bi, i: (bi, (i + 1) * bound_per_tile - 1, 0, 0, 0)),
            pl.BlockSpec((1, RET_HEADS, HEAD_DIM, HEAD_DIM), lambda bi, i: (bi, 0, 0, 0)),
            pl.BlockSpec((r, d), lambda bi, i: (0, 0)),
            pl.BlockSpec((CONV_CH, d), lambda bi, i: (1, 0)),
            pl.BlockSpec((MOD_ROWS, d), lambda bi, i: (0, 2)),
            pl.BlockSpec(dec_rows.shape, lambda bi, i: (0, 0)),
            slab_spec_of(w_up),
            slab_spec_of(w_down),
        ],
        out_specs=[pl.BlockSpec((1, tm, d), tok), slab_spec_of(w_up), slab_spec_of(w_down)],
        out_shape=[jax.ShapeDtypeStruct((b, n, d), F32),
                   jax.ShapeDtypeStruct(w_up.shape, BF16),
                   jax.ShapeDtypeStruct(w_down.shape, BF16)],
        scratch_shapes=[
            pltpu.VMEM((RET_HEADS, HEAD_DIM, HEAD_DIM), F32),
            pltpu.VMEM((RET_HEADS, c, c), BF16),
            pltpu.VMEM((RET_HEADS, c, 2 * HEAD_DIM), BF16),
            pltpu.VMEM((RET_HEADS, 8, c), F32),
            pltpu.VMEM((RET_HEADS, 8, 2 * HEAD_DIM), F32),
            pltpu.VMEM((tm, r), BF16),
        ],
        compiler_params=pltpu.CompilerParams(
            dimension_semantics=("arbitrary", "arbitrary"), vmem_limit_bytes=VMEM_LIMIT),
        name="mix",
    )(q, k, vt, gate, conv, x, sbound, sf0, w_out, w_out, mod, dec_rows, w_up, w_down)


def _ffn_kernel(x_ref, sh_ref, sc_ref, gt_ref, g2_ref, gf_ref,
                wa_ref, wb_ref, cw_ref, cb_ref, wd_ref,
                o_ref,
                h_scr):
    b = pl.program_id(0)
    f = pl.program_id(2)
    nf = pl.num_programs(2)
    tm, d = h_scr.shape
    tf = wa_ref.shape[1]
    blk = o_ref.shape[1:]
    step = blk[1]

    def body(first, last):
        if first:
            h = _norm_mod(x_ref[0].reshape(tm, d), g2_ref[...], sh_ref[pl.ds(b, 1), :],
                          sc_ref[pl.ds(b, 1), :]).astype(BF16)
            h_scr[...] = h
        else:
            h = h_scr[...]
        a = _dot(h, wa_ref[...])
        zero = jnp.zeros((step, tf), F32)
        a_up = jnp.concatenate([zero, a[:tm - step]], axis=0)
        a_dn = jnp.concatenate([a[step:], zero], axis=0)
        a = cw_ref[0:1, :] * a_up + cw_ref[1:2, :] * a + cw_ref[2:3, :] * a_dn + cb_ref[...]
        act = (a * _sigmoid(a) * _dot(h, wb_ref[...])).astype(BF16)
        if not last:
            y = _dot(act, wd_ref[...])
            o_ref[0] = (y if first else o_ref[0].reshape(tm, d) + y).reshape(blk)
            return
        groups = LAST_STEP_ROW_GROUPS
        gr = blk[0] // groups
        ys = [_dot(act[g * gr * step:(g + 1) * gr * step], wd_ref[...]) for g in range(groups)]
        for g in range(groups):
            rs = slice(g * gr, (g + 1) * gr)
            acc = o_ref[0, rs].reshape(gr * step, d) + ys[g]
            z = x_ref[0, rs].reshape(gr * step, d) + gt_ref[pl.ds(b, 1), :] * acc
            ms = jnp.mean(z * z, axis=-1, keepdims=True)
            o_ref[0, rs] = (z * lax.rsqrt(ms + EPS) * gf_ref[...]).reshape(gr, step, d)

    @pl.when(f == 0)
    def _():
        body(True, False)

    @pl.when((f > 0) & (f < nf - 1))
    def _():
        body(False, False)

    @pl.when(f == nf - 1)
    def _():
        body(False, True)


def _ffn_call(x, mod, norm_g, final_g, w_up, ffn_conv_w, ffn_conv_b, w_down):
    b, n, d = x.shape
    d_ff = w_down.shape[0]
    rows = n // GRID_W
    cols = FFN_COLS
    tf = FF_TILE
    nf = d_ff // tf
    assert nf >= 2
    xg = x.reshape(b, rows, GRID_W, d)
    tile = pl.BlockSpec((1, rows, cols, d), lambda bi, i, f: (bi, 0, i, 0))

    out = pl.pallas_call(
        _ffn_kernel,
        grid=(b, GRID_W // cols, nf),
        in_specs=[
            tile,
            pl.BlockSpec((MOD_ROWS, d), lambda bi, i, f: (0, 3)),
            pl.BlockSpec((MOD_ROWS, d), lambda bi, i, f: (0, 4)),
            pl.BlockSpec((MOD_ROWS, d), lambda bi, i, f: (0, 5)),
            pl.BlockSpec((1, d), lambda bi, i, f: (0, 0)),
            pl.BlockSpec((1, d), lambda bi, i, f: (0, 0)),
            pl.BlockSpec((d, tf), lambda bi, i, f: (0, f)),
            pl.BlockSpec((d, tf), lambda bi, i, f: (0, nf + f)),
            pl.BlockSpec((3, tf), lambda bi, i, f: (0, f)),
            pl.BlockSpec((1, tf), lambda bi, i, f: (0, f)),
            pl.BlockSpec((tf, d), lambda bi, i, f: (f, 0)),
        ],
        out_specs=tile,
        out_shape=jax.ShapeDtypeStruct((b, rows, GRID_W, d), F32),
        scratch_shapes=[
            pltpu.VMEM((rows * cols, d), BF16),
        ],
        compiler_params=pltpu.CompilerParams(
            dimension_semantics=("arbitrary", "arbitrary", "arbitrary"),
            vmem_limit_bytes=VMEM_LIMIT),
        name="ffn",
    )(xg, mod, mod, mod, norm_g, final_g, w_up, w_up, ffn_conv_w, ffn_conv_b, w_down)
    return out.reshape(b, n, d)


def _rope_tables(n):
    pos = np.arange(n, dtype=np.int32)
    row = (pos // GRID_W).astype(np.float32)
    col = (pos % GRID_W).astype(np.float32)
    dh = HEAD_DIM // 2
    freqs = np.float32(ROPE_THETA) ** (-np.arange(0, dh, 2, dtype=np.float32) / np.float32(dh))
    parts_c, parts_s = [], []
    for p in (row, col):
        ang = (p[:, None] * freqs[None, :]).astype(np.float32)
        cos = np.cos(ang).astype(np.float32)
        sin = np.sin(ang).astype(np.float32)
        parts_c += [cos, cos]
        parts_s += [-sin, sin]
    return (jnp.asarray(np.concatenate(parts_c, axis=-1)),
            jnp.asarray(np.concatenate(parts_s, axis=-1)))


def kernel(x, c, ctx, c_ctx, w_mod, b_mod, norm1_g, w_in, ret_decay_fwd, ret_decay_bwd,
           conv_w, w_out, norm2_g, w_up, ffn_conv_w, ffn_conv_b, w_down, final_g):
    b, n, d = x.shape
    depth = w_mod.shape[0]
    assert depth == 1 and n % MIX_TILE == 0 and n % GRID_W == 0
    assert MIX_TILE % INPROJ_TILE == 0 and MIX_TILE % RET_CHUNK == 0
    cos_t, sin_t = _rope_tables(n)
    craw = jnp.concatenate(
        [c, c_ctx[None, :], jnp.zeros((MOD_ROWS - b - 1, d), F32)], axis=0)

    layer = 0
    dec_rows = jnp.broadcast_to(
        jnp.concatenate([ret_decay_fwd[layer], ret_decay_bwd[layer]])[:, None],
        (2 * RET_HEADS, RET_CHUNK))
    g1 = norm1_g[layer][None, :]

    mod, w_in_b, sf0, sb0 = _mod_ctx_call(craw, w_mod[layer], b_mod[layer][None, :], w_in[layer],
                                          ctx, g1, dec_rows, b)
    q, k, vt, gate, conv, sbound, w_out_b = _inproj_call(
        x, mod, g1, w_in_b, cos_t, sin_t, conv_w[layer], dec_rows, sb0, w_out[layer])
    x_mid, w_up_b, w_down_b = _mix_call(q, k, vt, gate, conv, x, sbound, sf0, w_out_b, mod,
                                        dec_rows, w_up[layer], w_down[layer])
    return _ffn_call(x_mid, mod, norm2_g[layer][None, :], final_g[None, :], w_up_b,
                     ffn_conv_w[layer], ffn_conv_b[layer][None, :], w_down_b)
```

```python
import functools

import jax
import jax.numpy as jnp
import numpy as np
from jax import lax
from jax.experimental import pallas as pl
from jax.experimental.pallas import tpu as pltpu

F32 = jnp.float32
BF16 = jnp.bfloat16

GRID_W = 64
RET_HEADS = 8
HEAD_DIM = 128
RET_WIDTH = RET_HEADS * HEAD_DIM
CONV_CH = 1024
EPS = 1e-6
ROPE_THETA = 10000.0

MOD_ROWS = 8
INPROJ_TILE = 512
MIX_TILE = 512
FFN_COLS = 8
RET_CHUNK = 256
FF_TILE = 512
MOD_COL_TILE = 768
INPROJ_GROUP_ORDER = (0, 1, 2, 3, 5, 6, 4)
LAST_STEP_ROW_GROUPS = 4
V7X_VMEM_BYTES = 64 * 1024 * 1024
VMEM_LIMIT = V7X_VMEM_BYTES - 4 * 1024 * 1024


def _sigmoid(x):
    return 1.0 / (1.0 + jnp.exp(-x))


def _log_sigmoid(x):
    return jnp.minimum(x, 0.0) - jnp.log(1.0 + jnp.exp(-jnp.abs(x)))


def _norm_mod(x, gain, shift, scale):
    ms = jnp.mean(x * x, axis=-1, keepdims=True)
    y = x * lax.rsqrt(ms + EPS) * gain
    return y * (1.0 + scale) + shift


def _dot(a, b):
    return jnp.dot(a, b, preferred_element_type=F32)


def _dot_t0(a, b):
    return lax.dot_general(a, b, (((0,), (0,)), ((), ())), preferred_element_type=F32)


def _dot_nt(a, b):
    return lax.dot_general(a, b, (((1,), (1,)), ((), ())), preferred_element_type=F32)


def _head_lanes(rows):
    return jnp.concatenate([rows[hd:hd + 1, :HEAD_DIM] for hd in range(RET_HEADS)], axis=1)


def _mod_ctx_kernel(ctx_row, c_ref, w_ref, b_ref, win_ref, ctx_ref, g_ref, dec_ref,
                    o_ref, win_b_ref, sf_ref, sb_ref,
                    modc_scr, hc_scr, kv_scr):
    j = pl.program_id(0)
    steps = pl.num_programs(0)
    first_slab_step = steps - hc_scr.shape[0]
    n_keep, _, tn = modc_scr.shape
    nb, length, d = ctx_ref.shape
    slab = win_ref.shape[0]
    r = RET_WIDTH

    a = c_ref[...]
    a = (a * _sigmoid(a)).astype(BF16)
    out = _dot(a, w_ref[...].astype(BF16)) + b_ref[...]
    o_ref[...] = out

    @pl.when(j < n_keep)
    def _():
        modc_scr[j] = out

    @pl.when(j == first_slab_step)
    def _():
        row = jnp.concatenate([modc_scr[t][ctx_row:ctx_row + 1, :] for t in range(n_keep)], axis=1)
        x = ctx_ref[...].reshape(nb * length, d)
        hc = _norm_mod(x, g_ref[...], row[:, 0:d], row[:, d:2 * d]).astype(BF16)
        for t in range(hc_scr.shape[0]):
            hc_scr[t] = hc[:, t * slab:(t + 1) * slab]
        kv_scr[...] = jnp.zeros_like(kv_scr)

    @pl.when(j >= first_slab_step)
    def _():
        wb = win_ref[...].astype(BF16)
        win_b_ref[...] = wb
        kv_scr[...] += _dot(hc_scr[j - first_slab_step], wb[:, r:3 * r])

    @pl.when(j == steps - 1)
    def _():
        k = kv_scr[:, 0:r] * (HEAD_DIM ** -0.5)
        v = kv_scr[:, r:2 * r].astype(BF16)
        t = (lax.broadcasted_iota(jnp.int32, k.shape, 0) % length).astype(F32)
        lg = _log_sigmoid(dec_ref[...])
        kf = (k * jnp.exp(_head_lanes(lg[:RET_HEADS]) * (length - 1.0 - t))).astype(BF16)
        kb = (k * jnp.exp(_head_lanes(lg[RET_HEADS:]) * t)).astype(BF16)
        for bi in range(nb):
            rows = slice(bi * length, (bi + 1) * length)
            for hd in range(RET_HEADS):
                sl = slice(hd * HEAD_DIM, (hd + 1) * HEAD_DIM)
                sf_ref[bi, hd] = _dot_t0(v[rows, sl], kf[rows, sl])
                sb_ref[bi, hd] = _dot_t0(v[rows, sl], kb[rows, sl])


def _mod_ctx_call(craw, w_mod, b_mod, w_in, ctx, norm_g, dec_rows, ctx_row):
    d, n = w_mod.shape
    b, length, _ = ctx.shape
    tn = MOD_COL_TILE
    steps = n // tn
    n_slabs = steps // 2
    slab = w_in.shape[0] // n_slabs
    n_keep = -(-2 * d // tn)
    assert steps * tn == n and slab * n_slabs == w_in.shape[0] and slab % 128 == 0
    assert n_keep <= steps - n_slabs

    def slab_idx(j):
        return (jnp.maximum(j - (steps - n_slabs), 0), 0)

    slab_spec = pl.BlockSpec((slab, w_in.shape[1]), slab_idx)
    state = jax.ShapeDtypeStruct((b, RET_HEADS, HEAD_DIM, HEAD_DIM), F32)
    state_spec = pl.BlockSpec(state.shape, lambda j: (0, 0, 0, 0))
    return pl.pallas_call(
        functools.partial(_mod_ctx_kernel, ctx_row),
        grid=(steps,),
        in_specs=[
            pl.BlockSpec((MOD_ROWS, d), lambda j: (0, 0)),
            pl.BlockSpec((d, tn), lambda j: (0, j)),
            pl.BlockSpec((1, tn), lambda j: (0, j)),
            slab_spec,
            pl.BlockSpec((b, length, d), lambda j: (0, 0, 0)),
            pl.BlockSpec((1, d), lambda j: (0, 0)),
            pl.BlockSpec(dec_rows.shape, lambda j: (0, 0)),
        ],
        out_specs=[pl.BlockSpec((MOD_ROWS, tn), lambda j: (0, j)), slab_spec, state_spec, state_spec],
        out_shape=[jax.ShapeDtypeStruct((MOD_ROWS, n), F32),
                   jax.ShapeDtypeStruct(w_in.shape, BF16), state, state],
        scratch_shapes=[
            pltpu.VMEM((n_keep, MOD_ROWS, tn), F32),
            pltpu.VMEM((n_slabs, b * length, slab), BF16),
            pltpu.VMEM((b * length, 2 * RET_WIDTH), F32),
        ],
        compiler_params=pltpu.CompilerParams(
            dimension_semantics=("arbitrary",), vmem_limit_bytes=VMEM_LIMIT),
        name="mod_ctx",
    )(craw, w_mod, b_mod, w_in, ctx, norm_g, dec_rows)


def _rope(acc, cos, sin, scale):
    lane = lax.broadcasted_iota(jnp.int32, (1, HEAD_DIM), 1)
    lo = (lane % 64) < 32
    outs = []
    for hd in range(RET_HEADS):
        xh = acc[:, hd * HEAD_DIM:(hd + 1) * HEAD_DIM]
        partner = jnp.where(lo, pltpu.roll(xh, 96, 1), pltpu.roll(xh, 32, 1))
        y = xh * cos + partner * sin
        outs.append(y if scale == 1.0 else y * scale)
    return outs


def _inproj_kernel(x_ref, sh_ref, sc_ref, g_ref, w_hbm, cos_ref, sin_ref, cw_ref, dec_ref,
                   sb0_ref, wout_ref,
                   q_ref, k_ref, vt_ref, gate_ref, conv_ref, sbound_ref, wout_b_ref,
                   s_scr, dec_scr, w_scr, w_sem):
    b = pl.program_id(0)
    i = pl.program_id(1)
    tm = dec_scr.shape[1]
    r = RET_WIDTH
    first_step = (b == 0) & (i == 0)

    wout_b_ref[...] = wout_ref[...].astype(wout_b_ref.dtype)

    @pl.when(i == 0)
    def _():
        s_scr[...] = sb0_ref[0]
        t = lax.broadcasted_iota(jnp.int32, dec_scr.shape, 1).astype(F32)
        lgb = _log_sigmoid(dec_ref[RET_HEADS:, 0:1])
        dec_scr[...] = jnp.exp(lgb * t)

    def group_copy(group):
        return pltpu.make_async_copy(w_hbm.at[:, pl.ds(group * r, r)], w_scr.at[group],
                                     w_sem.at[group])

    def body(wait_for_weights):
        h = _norm_mod(x_ref[0], g_ref[...], sh_ref[pl.ds(b, 1), :],
                      sc_ref[pl.ds(b, 1), :]).astype(BF16)

        def proj(group):
            if wait_for_weights:
                group_copy(group).wait()
            return _dot(h, w_scr[group])

        cos = cos_ref[...]
        sin = sin_ref[...]
        for hd, qh in enumerate(_rope(proj(0), cos, sin, 1.0)):
            q_ref[0, :, hd * HEAD_DIM:(hd + 1) * HEAD_DIM] = qh.astype(q_ref.dtype)
        ks = [kh.astype(BF16) for kh in _rope(proj(1), cos, sin, HEAD_DIM ** -0.5)]
        for hd in range(RET_HEADS):
            k_ref[0, :, hd * HEAD_DIM:(hd + 1) * HEAD_DIM] = ks[hd]
        v = proj(2)
        tile_dec = jnp.exp(_log_sigmoid(dec_ref[RET_HEADS:, :HEAD_DIM]) * float(tm))
        for hd in range(RET_HEADS):
            vt = v[:, hd * HEAD_DIM:(hd + 1) * HEAD_DIM].T
            vt_ref[0, hd] = vt.astype(vt_ref.dtype)
            vtd = (vt * dec_scr[hd:hd + 1, :]).astype(BF16)
            s_old = s_scr[hd]
            sbound_ref[0, 0, hd] = s_old
            s_scr[hd] = tile_dec[hd:hd + 1, :] * s_old + _dot(vtd, ks[hd])

        g = proj(3)
        gate_ref[0] = (g * _sigmoid(g)).astype(gate_ref.dtype)

        u = proj(5) * proj(6)
        col = lax.broadcasted_iota(jnp.int32, (tm, 1), 0) % GRID_W
        up = jnp.where(col == 0, 0.0, pltpu.roll(u, 1, 0))
        un = jnp.where(col == GRID_W - 1, 0.0, pltpu.roll(u, tm - 1, 0))
        y = proj(4) * (cw_ref[0:1, :] * up + cw_ref[1:2, :] * u + cw_ref[2:3, :] * un)
        conv_ref[0] = y.astype(conv_ref.dtype)

    @pl.when(first_step)
    def _():
        for group in INPROJ_GROUP_ORDER:
            group_copy(group).start()
        body(True)

    @pl.when(jnp.logical_not(first_step))
    def _():
        body(False)


def _inproj_call(x, mod, norm_g, w_in, cos_t, sin_t, conv_w, dec_rows, sb0, w_out):
    b, n, d = x.shape
    tm = INPROJ_TILE
    nt = n // tm
    r = RET_WIDTH
    n_groups = w_in.shape[1] // r
    assert sorted(INPROJ_GROUP_ORDER) == list(range(n_groups))
    slab = w_out.shape[0] // (b * nt)
    assert slab * b * nt == w_out.shape[0] and slab % 16 == 0
    slab_spec = pl.BlockSpec((slab, w_out.shape[1]), lambda bi, i: (bi * nt + i, 0))

    def tok(bi, i):
        return (bi, nt - 1 - i, 0)

    seq = jax.ShapeDtypeStruct((b, n, r), BF16)
    seq_spec = pl.BlockSpec((1, tm, r), tok)
    heads_t = jax.ShapeDtypeStruct((b, RET_HEADS, HEAD_DIM, n), BF16)
    heads_t_spec = pl.BlockSpec((1, RET_HEADS, HEAD_DIM, tm), lambda bi, i: (bi, 0, 0, nt - 1 - i))
    return pl.pallas_call(
        _inproj_kernel,
        grid=(b, nt),
        in_specs=[
            pl.BlockSpec((1, tm, d), tok),
            pl.BlockSpec((MOD_ROWS, d), lambda bi, i: (0, 0)),
            pl.BlockSpec((MOD_ROWS, d), lambda bi, i: (0, 1)),
            pl.BlockSpec((1, d), lambda bi, i: (0, 0)),
            pl.BlockSpec(memory_space=pl.ANY),
            pl.BlockSpec((tm, HEAD_DIM), lambda bi, i: (nt - 1 - i, 0)),
            pl.BlockSpec((tm, HEAD_DIM), lambda bi, i: (nt - 1 - i, 0)),
            pl.BlockSpec((3, CONV_CH), lambda bi, i: (0, 0)),
            pl.BlockSpec(dec_rows.shape, lambda bi, i: (0, 0)),
            pl.BlockSpec((1, RET_HEADS, HEAD_DIM, HEAD_DIM), lambda bi, i: (bi, 0, 0, 0)),
            slab_spec,
        ],
        out_specs=[
            seq_spec, seq_spec, heads_t_spec, seq_spec, seq_spec,
            pl.BlockSpec((1, 1, RET_HEADS, HEAD_DIM, HEAD_DIM),
                         lambda bi, i: (bi, nt - 1 - i, 0, 0, 0)),
            slab_spec,
        ],
        out_shape=[seq, seq, heads_t, seq, seq,
                   jax.ShapeDtypeStruct((b, nt, RET_HEADS, HEAD_DIM, HEAD_DIM), F32),
                   jax.ShapeDtypeStruct(w_out.shape, BF16)],
        scratch_shapes=[
            pltpu.VMEM((RET_HEADS, HEAD_DIM, HEAD_DIM), F32),
            pltpu.VMEM((RET_HEADS, tm), F32),
            pltpu.VMEM((n_groups, d, r), w_in.dtype),
            pltpu.SemaphoreType.DMA((n_groups,)),
        ],
        compiler_params=pltpu.CompilerParams(
            dimension_semantics=("arbitrary", "arbitrary"),
            vmem_limit_bytes=VMEM_LIMIT),
        name="inproj",
    )(x, mod, mod, norm_g, w_in, cos_t, sin_t, conv_w, dec_rows, sb0, w_out)


def _mix_kernel(q_ref, k_ref, vt_ref, gate_ref, conv_ref, x_ref, sbound_ref, sf0_ref,
                wo_r_ref, wo_c_ref, g1_ref, dec_ref, wup_ref, wdn_ref,
                o_ref, wup_b_ref, wdn_b_ref,
                sf_scr, dmat_scr, qdec_scr, kdec_scr, cdec_scr, ret_scr):
    b = pl.program_id(0)
    i = pl.program_id(1)
    tm = ret_scr.shape[0]
    c = dmat_scr.shape[1]
    nc = tm // c

    wup_b_ref[...] = wup_ref[...].astype(wup_b_ref.dtype)
    wdn_b_ref[...] = wdn_ref[...].astype(wdn_b_ref.dtype)

    @pl.when(i == 0)
    def _():
        sf_scr[...] = sf0_ref[0]

    @pl.when((b == 0) & (i == 0))
    def _():
        lgf = _log_sigmoid(dec_ref[:RET_HEADS])
        lgb = _log_sigmoid(dec_ref[RET_HEADS:])
        ri = lax.broadcasted_iota(jnp.int32, (c, c), 0)
        ci = lax.broadcasted_iota(jnp.int32, (c, c), 1)
        rel = (ri - ci).astype(F32)
        t = lax.broadcasted_iota(jnp.int32, (c, HEAD_DIM), 0).astype(F32)
        tl = lax.broadcasted_iota(jnp.int32, (1, c), 1).astype(F32)
        for hd in range(RET_HEADS):
            f = lgf[hd:hd + 1, :]
            g = lgb[hd:hd + 1, :]
            dmat_scr[hd] = jnp.where(rel >= 0, jnp.exp(f * jnp.maximum(rel, 0.0)),
                                     jnp.exp(g * jnp.maximum(-rel, 0.0))).astype(BF16)
            fh = f[:, :HEAD_DIM]
            gh = g[:, :HEAD_DIM]
            qdec_scr[hd, :, :HEAD_DIM] = jnp.exp(fh * (t + 1.0)).astype(BF16)
            qdec_scr[hd, :, HEAD_DIM:] = jnp.exp(gh * (float(c) - t)).astype(BF16)
            kdec_scr[hd, 0:1, :] = jnp.exp(f * (float(c) - 1.0 - tl))
            kdec_scr[hd, 1:2, :] = jnp.exp(g * tl)
            cdec_scr[hd, :, :HEAD_DIM] = jnp.broadcast_to(jnp.exp(fh * float(c)), (8, HEAD_DIM))
            cdec_scr[hd, :, HEAD_DIM:] = jnp.broadcast_to(jnp.exp(gh * float(c)), (8, HEAD_DIM))

    heads = range(RET_HEADS)
    hsl = [slice(hd * HEAD_DIM, (hd + 1) * HEAD_DIM) for hd in heads]
    crows = [slice(ch * c, (ch + 1) * c) for ch in range(nc)]
    ks = [[k_ref[0, crows[ch], hsl[hd]] for ch in range(nc)] for hd in heads]
    vts = [[vt_ref[0, hd, :, crows[ch]] for ch in range(nc)] for hd in heads]

    qs = [[q_ref[0, crows[ch], hsl[hd]] for ch in range(nc)] for hd in heads]

    def masked_scores(ch):
        return [_dot_nt(qs[hd][ch], ks[hd][ch]).astype(BF16) * dmat_scr[hd] for hd in heads]

    ps_first = masked_scores(0)

    s2ts = []
    for hd in heads:
        cf = cdec_scr[hd, 0:1, :HEAD_DIM]
        cb = cdec_scr[hd, 0:1, HEAD_DIM:]
        sb = [None] * nc
        sb[nc - 1] = sbound_ref[0, 0, hd]
        for ch in range(nc - 1, 0, -1):
            vb = vts[hd][ch] * kdec_scr[hd, 1:2, :].astype(BF16)
            sb[ch - 1] = cb * sb[ch] + _dot(vb, ks[hd][ch])
        sf = [sf_scr[hd]]
        for ch in range(nc):
            vf = vts[hd][ch] * kdec_scr[hd, 0:1, :].astype(BF16)
            sf.append(cf * sf[ch] + _dot(vf, ks[hd][ch]))
        sf_scr[hd] = sf[nc]
        s2ts.append([jnp.concatenate([sf[ch], sb[ch]], axis=1).astype(BF16) for ch in range(nc)])

    def cross_terms(ch):
        out = []
        for hd in heads:
            qh = qs[hd][ch]
            q2 = jnp.concatenate([qh, qh], axis=1) * qdec_scr[hd]
            out.append(_dot_nt(s2ts[hd][ch], q2))
        return out

    crosses = [cross_terms(ch) for ch in range(nc)]
    ps = ps_first
    for ch in range(nc):
        inner = [_dot_nt(vts[hd][ch], ps[hd]) for hd in heads]
        cross = crosses[ch]
        for hd in heads:
            o = (inner[hd] + cross[hd]).T
            ms = jnp.mean(o * o, axis=-1, keepdims=True)
            o = o * lax.rsqrt(ms + EPS) * gate_ref[0, crows[ch], hsl[hd]].astype(F32)
            ret_scr[crows[ch], hsl[hd]] = o.astype(BF16)
        if ch + 1 < nc:
            ps = masked_scores(ch + 1)
        mix = _dot(ret_scr[crows[ch], :], wo_r_ref[...]) + _dot(conv_ref[0, crows[ch], :], wo_c_ref[...])
        o_ref[0, crows[ch], :] = x_ref[0, crows[ch], :] + g1_ref[pl.ds(b, 1), :] * mix


def _mix_call(q, k, vt, gate, conv, x, sbound, sf0, w_out, mod, dec_rows, w_up, w_down):
    b, n, d = x.shape
    tm = MIX_TILE
    nt = n // tm
    bound_per_tile = sbound.shape[1] // nt
    r = RET_WIDTH
    c = RET_CHUNK

    def slab_spec_of(w):
        slab = w.shape[0] // (b * nt)
        assert slab * b * nt == w.shape[0] and slab % 16 == 0
        return pl.BlockSpec((slab, w.shape[1]), lambda bi, i: (bi * nt + i, 0))

    def tok(bi, i):
        return (bi, i, 0)

    seq_spec = pl.BlockSpec((1, tm, r), tok)
    return pl.pallas_call(
        _mix_kernel,
        grid=(b, nt),
        in_specs=[
            seq_spec, seq_spec,
            pl.BlockSpec((1, RET_HEADS, HEAD_DIM, tm), lambda bi, i: (bi, 0, 0, i)),
            seq_spec, seq_spec,
            pl.BlockSpec((1, tm, d), tok),
            pl.BlockSpec((1, 1, RET_HEADS, HEAD_DIM, HEAD_DIM),
                         lambda bi, i: (bi, (i + 1) * bound_per_tile - 1, 0, 0, 0)),
            pl.BlockSpec((1, RET_HEADS, HEAD_DIM, HEAD_DIM), lambda bi, i: (bi, 0, 0, 0)),
            pl.BlockSpec((r, d), lambda bi, i: (0, 0)),
            pl.BlockSpec((CONV_CH, d), lambda bi, i: (1, 0)),
            pl.BlockSpec((MOD_ROWS, d), lambda bi, i: (0, 2)),
            pl.BlockSpec(dec_rows.shape, lambda bi, i: (0, 0)),
            slab_spec_of(w_up),
            slab_spec_of(w_down),
        ],
        out_specs=[pl.BlockSpec((1, tm, d), tok), slab_spec_of(w_up), slab_spec_of(w_down)],
        out_shape=[jax.ShapeDtypeStruct((b, n, d), F32),
                   jax.ShapeDtypeStruct(w_up.shape, BF16),
                   jax.ShapeDtypeStruct(w_down.shape, BF16)],
        scratch_shapes=[
            pltpu.VMEM((RET_HEADS, HEAD_DIM, HEAD_DIM), F32),
            pltpu.VMEM((RET_HEADS, c, c), BF16),
            pltpu.VMEM((RET_HEADS, c, 2 * HEAD_DIM), BF16),
            pltpu.VMEM((RET_HEADS, 8, c), F32),
            pltpu.VMEM((RET_HEADS, 8, 2 * HEAD_DIM), F32),
            pltpu.VMEM((tm, r), BF16),
        ],
        compiler_params=pltpu.CompilerParams(
            dimension_semantics=("arbitrary", "arbitrary"), vmem_limit_bytes=VMEM_LIMIT),
        name="mix",
    )(q, k, vt, gate, conv, x, sbound, sf0, w_out, w_out, mod, dec_rows, w_up, w_down)


def _ffn_kernel(x_ref, sh_ref, sc_ref, gt_ref, g2_ref, gf_ref,
                wa_ref, wb_ref, cw_ref, cb_ref, wd_ref,
                o_ref,
                h_scr):
    b = pl.program_id(0)
    f = pl.program_id(2)
    nf = pl.num_programs(2)
    tm, d = h_scr.shape
    tf = wa_ref.shape[1]
    blk = o_ref.shape[1:]
    step = blk[1]

    def body(first, last):
        if first:
            h = _norm_mod(x_ref[0].reshape(tm, d), g2_ref[...], sh_ref[pl.ds(b, 1), :],
                          sc_ref[pl.ds(b, 1), :]).astype(BF16)
            h_scr[...] = h
        else:
            h = h_scr[...]
        a = _dot(h, wa_ref[...])
        zero = jnp.zeros((step, tf), F32)
        a_up = jnp.concatenate([zero, a[:tm - step]], axis=0)
        a_dn = jnp.concatenate([a[step:], zero], axis=0)
        a = cw_ref[0:1, :] * a_up + cw_ref[1:2, :] * a + cw_ref[2:3, :] * a_dn + cb_ref[...]
        act = (a * _sigmoid(a) * _dot(h, wb_ref[...])).astype(BF16)
        if not last:
            y = _dot(act, wd_ref[...])
            o_ref[0] = (y if first else o_ref[0].reshape(tm, d) + y).reshape(blk)
            return
        groups = LAST_STEP_ROW_GROUPS
        gr = blk[0] // groups
        ys = [_dot(act[g * gr * step:(g + 1) * gr * step], wd_ref[...]) for g in range(groups)]
        for g in range(groups):
            rs = slice(g * gr, (g + 1) * gr)
            acc = o_ref[0, rs].reshape(gr * step, d) + ys[g]
            z = x_ref[0, rs].reshape(gr * step, d) + gt_ref[pl.ds(b, 1), :] * acc
            ms = jnp.mean(z * z, axis=-1, keepdims=True)
            o_ref[0, rs] = (z * lax.rsqrt(ms + EPS) * gf_ref[...]).reshape(gr, step, d)

    @pl.when(f == 0)
    def _():
        body(True, False)

    @pl.when((f > 0) & (f < nf - 1))
    def _():
        body(False, False)

    @pl.when(f == nf - 1)
    def _():
        body(False, True)


def _ffn_call(x, mod, norm_g, final_g, w_up, ffn_conv_w, ffn_conv_b, w_down):
    b, n, d = x.shape
    d_ff = w_down.shape[0]
    rows = n // GRID_W
    cols = FFN_COLS
    tf = FF_TILE
    nf = d_ff // tf
    assert nf >= 2
    xg = x.reshape(b, rows, GRID_W, d)
    tile = pl.BlockSpec((1, rows, cols, d), lambda bi, i, f: (bi, 0, i, 0))

    out = pl.pallas_call(
        _ffn_kernel,
        grid=(b, GRID_W // cols, nf),
        in_specs=[
            tile,
            pl.BlockSpec((MOD_ROWS, d), lambda bi, i, f: (0, 3)),
            pl.BlockSpec((MOD_ROWS, d), lambda bi, i, f: (0, 4)),
            pl.BlockSpec((MOD_ROWS, d), lambda bi, i, f: (0, 5)),
            pl.BlockSpec((1, d), lambda bi, i, f: (0, 0)),
            pl.BlockSpec((1, d), lambda bi, i, f: (0, 0)),
            pl.BlockSpec((d, tf), lambda bi, i, f: (0, f)),
            pl.BlockSpec((d, tf), lambda bi, i, f: (0, nf + f)),
            pl.BlockSpec((3, tf), lambda bi, i, f: (0, f)),
            pl.BlockSpec((1, tf), lambda bi, i, f: (0, f)),
            pl.BlockSpec((tf, d), lambda bi, i, f: (f, 0)),
        ],
        out_specs=tile,
        out_shape=jax.ShapeDtypeStruct((b, rows, GRID_W, d), F32),
        scratch_shapes=[
            pltpu.VMEM((rows * cols, d), BF16),
        ],
        compiler_params=pltpu.CompilerParams(
            dimension_semantics=("arbitrary", "arbitrary", "arbitrary"),
            vmem_limit_bytes=VMEM_LIMIT),
        name="ffn",
    )(xg, mod, mod, mod, norm_g, final_g, w_up, w_up, ffn_conv_w, ffn_conv_b, w_down)
    return out.reshape(b, n, d)


def _rope_tables(n):
    pos = np.arange(n, dtype=np.int32)
    row = (pos // GRID_W).astype(np.float32)
    col = (pos % GRID_W).astype(np.float32)
    dh = HEAD_DIM // 2
    freqs = np.float32(ROPE_THETA) ** (-np.arange(0, dh, 2, dtype=np.float32) / np.float32(dh))
    parts_c, parts_s = [], []
    for p in (row, col):
        ang = (p[:, None] * freqs[None, :]).astype(np.float32)
        cos = np.cos(ang).astype(np.float32)
        sin = np.sin(ang).astype(np.float32)
        parts_c += [cos, cos]
        parts_s += [-sin, sin]
    return (jnp.asarray(np.concatenate(parts_c, axis=-1)),
            jnp.asarray(np.concatenate(parts_s, axis=-1)))


def kernel(x, c, ctx, c_ctx, w_mod, b_mod, norm1_g, w_in, ret_decay_fwd, ret_decay_bwd,
           conv_w, w_out, norm2_g, w_up, ffn_conv_w, ffn_conv_b, w_down, final_g):
    b, n, d = x.shape
    depth = w_mod.shape[0]
    assert depth == 1 and n % MIX_TILE == 0 and n % GRID_W == 0
    assert MIX_TILE % INPROJ_TILE == 0 and MIX_TILE % RET_CHUNK == 0
    cos_t, sin_t = _rope_tables(n)
    craw = jnp.concatenate(
        [c, c_ctx[None, :], jnp.zeros((MOD_ROWS - b - 1, d), F32)], axis=0)

    layer = 0
    dec_rows = jnp.broadcast_to(
        jnp.concatenate([ret_decay_fwd[layer], ret_decay_bwd[layer]])[:, None],
        (2 * RET_HEADS, RET_CHUNK))
    g1 = norm1_g[layer][None, :]

    mod, w_in_b, sf0, sb0 = _mod_ctx_call(craw, w_mod[layer], b_mod[layer][None, :], w_in[layer],
                                          ctx, g1, dec_rows, b)
    q, k, vt, gate, conv, sbound, w_out_b = _inproj_call(
        x, mod, g1, w_in_b, cos_t, sin_t, conv_w[layer], dec_rows, sb0, w_out[layer])
    x_mid, w_up_b, w_down_b = _mix_call(q, k, vt, gate, conv, x, sbound, sf0, w_out_b, mod,
                                        dec_rows, w_up[layer], w_down[layer])
    return _ffn_call(x_mid, mod, norm2_g[layer][None, :], final_g[None, :], w_up_b,
                     ffn_conv_w[layer], ffn_conv_b[layer][None, :], w_down_b)
```

```python
import functools

import jax
import jax.numpy as jnp
import numpy as np
from jax import lax
from jax.experimental import pallas as pl
from jax.experimental.pallas import tpu as pltpu

F32 = jnp.float32
BF16 = jnp.bfloat16

GRID_W = 64
RET_HEADS = 8
HEAD_DIM = 128
RET_WIDTH = RET_HEADS * HEAD_DIM
CONV_CH = 1024
EPS = 1e-6
ROPE_THETA = 10000.0

MOD_ROWS = 8
INPROJ_TILE = 512
MIX_TILE = 512
FFN_COLS = 8
RET_CHUNK = 256
FF_TILE = 512
MOD_COL_TILE = 768
INPROJ_GROUP_ORDER = (0, 1, 2, 3, 5, 6, 4)
LAST_STEP_ROW_GROUPS = 4
V7X_VMEM_BYTES = 64 * 1024 * 1024
VMEM_LIMIT = V7X_VMEM_BYTES - 4 * 1024 * 1024


def _sigmoid(x):
    return 1.0 / (1.0 + jnp.exp(-x))


def _log_sigmoid(x):
    return jnp.minimum(x, 0.0) - jnp.log(1.0 + jnp.exp(-jnp.abs(x)))


def _norm_mod(x, gain, shift, scale):
    ms = jnp.mean(x * x, axis=-1, keepdims=True)
    y = x * lax.rsqrt(ms + EPS) * gain
    return y * (1.0 + scale) + shift


def _dot(a, b):
    return jnp.dot(a, b, preferred_element_type=F32)


def _dot_t0(a, b):
    return lax.dot_general(a, b, (((0,), (0,)), ((), ())), preferred_element_type=F32)


def _dot_nt(a, b):
    return lax.dot_general(a, b, (((1,), (1,)), ((), ())), preferred_element_type=F32)


def _head_lanes(rows):
    return jnp.concatenate([rows[hd:hd + 1, :HEAD_DIM] for hd in range(RET_HEADS)], axis=1)


def _mod_ctx_kernel(ctx_row, c_ref, w_ref, b_ref, win_ref, ctx_ref, g_ref, dec_ref,
                    o_ref, win_b_ref, sf_ref, sb_ref,
                    modc_scr, hc_scr, kv_scr):
    j = pl.program_id(0)
    steps = pl.num_programs(0)
    first_slab_step = steps - hc_scr.shape[0]
    n_keep, _, tn = modc_scr.shape
    nb, length, d = ctx_ref.shape
    slab = win_ref.shape[0]
    r = RET_WIDTH

    a = c_ref[...]
    a = (a * _sigmoid(a)).astype(BF16)
    out = _dot(a, w_ref[...].astype(BF16)) + b_ref[...]
    o_ref[...] = out

    @pl.when(j < n_keep)
    def _():
        modc_scr[j] = out

    @pl.when(j == first_slab_step)
    def _():
        row = jnp.concatenate([modc_scr[t][ctx_row:ctx_row + 1, :] for t in range(n_keep)], axis=1)
        x = ctx_ref[...].reshape(nb * length, d)
        hc = _norm_mod(x, g_ref[...], row[:, 0:d], row[:, d:2 * d]).astype(BF16)
        for t in range(hc_scr.shape[0]):
            hc_scr[t] = hc[:, t * slab:(t + 1) * slab]
        kv_scr[...] = jnp.zeros_like(kv_scr)

    @pl.when(j >= first_slab_step)
    def _():
        wb = win_ref[...].astype(BF16)
        for group in range(win_b_ref.shape[0]):
            win_b_ref[group] = wb[:, group * r:(group + 1) * r]
        kv_scr[...] += _dot(hc_scr[j - first_slab_step], wb[:, r:3 * r])

    @pl.when(j == steps - 1)
    def _():
        k = kv_scr[:, 0:r] * (HEAD_DIM ** -0.5)
        v = kv_scr[:, r:2 * r].astype(BF16)
        t = (lax.broadcasted_iota(jnp.int32, k.shape, 0) % length).astype(F32)
        lg = _log_sigmoid(dec_ref[...])
        kf = (k * jnp.exp(_head_lanes(lg[:RET_HEADS]) * (length - 1.0 - t))).astype(BF16)
        kb = (k * jnp.exp(_head_lanes(lg[RET_HEADS:]) * t)).astype(BF16)
        for bi in range(nb):
            rows = slice(bi * length, (bi + 1) * length)
            for hd in range(RET_HEADS):
                sl = slice(hd * HEAD_DIM, (hd + 1) * HEAD_DIM)
                sf_ref[bi, hd] = _dot_t0(v[rows, sl], kf[rows, sl])
                sb_ref[bi, hd] = _dot_t0(v[rows, sl], kb[rows, sl])


def _mod_ctx_call(craw, w_mod, b_mod, w_in, ctx, norm_g, dec_rows, ctx_row):
    d, n = w_mod.shape
    b, length, _ = ctx.shape
    tn = MOD_COL_TILE
    steps = n // tn
    n_slabs = steps // 2
    slab = w_in.shape[0] // n_slabs
    n_keep = -(-2 * d // tn)
    assert steps * tn == n and slab * n_slabs == w_in.shape[0] and slab % 128 == 0
    assert n_keep <= steps - n_slabs

    def slab_idx(j):
        return (jnp.maximum(j - (steps - n_slabs), 0), 0)

    slab_spec = pl.BlockSpec((slab, w_in.shape[1]), slab_idx)
    n_groups = w_in.shape[1] // RET_WIDTH
    grouped_spec = pl.BlockSpec((n_groups, slab, RET_WIDTH), lambda j: (0, slab_idx(j)[0], 0))
    state = jax.ShapeDtypeStruct((b, RET_HEADS, HEAD_DIM, HEAD_DIM), F32)
    state_spec = pl.BlockSpec(state.shape, lambda j: (0, 0, 0, 0))
    return pl.pallas_call(
        functools.partial(_mod_ctx_kernel, ctx_row),
        grid=(steps,),
        in_specs=[
            pl.BlockSpec((MOD_ROWS, d), lambda j: (0, 0)),
            pl.BlockSpec((d, tn), lambda j: (0, j)),
            pl.BlockSpec((1, tn), lambda j: (0, j)),
            slab_spec,
            pl.BlockSpec((b, length, d), lambda j: (0, 0, 0)),
            pl.BlockSpec((1, d), lambda j: (0, 0)),
            pl.BlockSpec(dec_rows.shape, lambda j: (0, 0)),
        ],
        out_specs=[pl.BlockSpec((MOD_ROWS, tn), lambda j: (0, j)), grouped_spec, state_spec,
                   state_spec],
        out_shape=[jax.ShapeDtypeStruct((MOD_ROWS, n), F32),
                   jax.ShapeDtypeStruct((n_groups, w_in.shape[0], RET_WIDTH), BF16), state, state],
        scratch_shapes=[
            pltpu.VMEM((n_keep, MOD_ROWS, tn), F32),
            pltpu.VMEM((n_slabs, b * length, slab), BF16),
            pltpu.VMEM((b * length, 2 * RET_WIDTH), F32),
        ],
        compiler_params=pltpu.CompilerParams(
            dimension_semantics=("arbitrary",), vmem_limit_bytes=VMEM_LIMIT),
        name="mod_ctx",
    )(craw, w_mod, b_mod, w_in, ctx, norm_g, dec_rows)


def _rope(acc, cos, sin, scale):
    lane = lax.broadcasted_iota(jnp.int32, (1, HEAD_DIM), 1)
    lo = (lane % 64) < 32
    outs = []
    for hd in range(RET_HEADS):
        xh = acc[:, hd * HEAD_DIM:(hd + 1) * HEAD_DIM]
        partner = jnp.where(lo, pltpu.roll(xh, 96, 1), pltpu.roll(xh, 32, 1))
        y = xh * cos + partner * sin
        outs.append(y if scale == 1.0 else y * scale)
    return outs


def _inproj_kernel(x_ref, sh_ref, sc_ref, g_ref, w_hbm, cos_ref, sin_ref, cw_ref, dec_ref,
                   sb0_ref, wout_ref,
                   q_ref, k_ref, vt_ref, gate_ref, conv_ref, sbound_ref, wout_b_ref,
                   s_scr, dec_scr, w_scr, w_sem):
    b = pl.program_id(0)
    i = pl.program_id(1)
    tm = dec_scr.shape[1]
    first_step = (b == 0) & (i == 0)

    wout_b_ref[...] = wout_ref[...].astype(wout_b_ref.dtype)

    @pl.when(i == 0)
    def _():
        s_scr[...] = sb0_ref[0]
        t = lax.broadcasted_iota(jnp.int32, dec_scr.shape, 1).astype(F32)
        lgb = _log_sigmoid(dec_ref[RET_HEADS:, 0:1])
        dec_scr[...] = jnp.exp(lgb * t)

    def group_copy(group):
        return pltpu.make_async_copy(w_hbm.at[group], w_scr.at[group], w_sem.at[group])

    def body(wait_for_weights):
        h = _norm_mod(x_ref[0], g_ref[...], sh_ref[pl.ds(b, 1), :],
                      sc_ref[pl.ds(b, 1), :]).astype(BF16)

        def proj(group):
            if wait_for_weights:
                group_copy(group).wait()
            return _dot(h, w_scr[group])

        cos = cos_ref[...]
        sin = sin_ref[...]
        for hd, qh in enumerate(_rope(proj(0), cos, sin, 1.0)):
            q_ref[0, :, hd * HEAD_DIM:(hd + 1) * HEAD_DIM] = qh.astype(q_ref.dtype)
        ks = [kh.astype(BF16) for kh in _rope(proj(1), cos, sin, HEAD_DIM ** -0.5)]
        for hd in range(RET_HEADS):
            k_ref[0, :, hd * HEAD_DIM:(hd + 1) * HEAD_DIM] = ks[hd]
        v = proj(2)
        tile_dec = jnp.exp(_log_sigmoid(dec_ref[RET_HEADS:, :HEAD_DIM]) * float(tm))
        for hd in range(RET_HEADS):
            vt = v[:, hd * HEAD_DIM:(hd + 1) * HEAD_DIM].T
            vt_ref[0, hd] = vt.astype(vt_ref.dtype)
            vtd = (vt * dec_scr[hd:hd + 1, :]).astype(BF16)
            s_old = s_scr[hd]
            sbound_ref[0, 0, hd] = s_old
            s_scr[hd] = tile_dec[hd:hd + 1, :] * s_old + _dot(vtd, ks[hd])

        g = proj(3)
        gate_ref[0] = (g * _sigmoid(g)).astype(gate_ref.dtype)

        u = proj(5) * proj(6)
        col = lax.broadcasted_iota(jnp.int32, (tm, 1), 0) % GRID_W
        up = jnp.where(col == 0, 0.0, pltpu.roll(u, 1, 0))
        un = jnp.where(col == GRID_W - 1, 0.0, pltpu.roll(u, tm - 1, 0))
        y = proj(4) * (cw_ref[0:1, :] * up + cw_ref[1:2, :] * u + cw_ref[2:3, :] * un)
        conv_ref[0] = y.astype(conv_ref.dtype)

    @pl.when(first_step)
    def _():
        for group in INPROJ_GROUP_ORDER:
            group_copy(group).start()
        body(True)

    @pl.when(jnp.logical_not(first_step))
    def _():
        body(False)


def _inproj_call(x, mod, norm_g, w_in, cos_t, sin_t, conv_w, dec_rows, sb0, w_out):
    b, n, d = x.shape
    tm = INPROJ_TILE
    nt = n // tm
    r = RET_WIDTH
    n_groups = w_in.shape[0]
    assert sorted(INPROJ_GROUP_ORDER) == list(range(n_groups)) and w_in.shape[1:] == (d, r)
    slab = w_out.shape[0] // (b * nt)
    assert slab * b * nt == w_out.shape[0] and slab % 16 == 0
    slab_spec = pl.BlockSpec((slab, w_out.shape[1]), lambda bi, i: (bi * nt + i, 0))

    def tok(bi, i):
        return (bi, nt - 1 - i, 0)

    seq = jax.ShapeDtypeStruct((b, n, r), BF16)
    seq_spec = pl.BlockSpec((1, tm, r), tok)
    heads_t = jax.ShapeDtypeStruct((b, RET_HEADS, HEAD_DIM, n), BF16)
    heads_t_spec = pl.BlockSpec((1, RET_HEADS, HEAD_DIM, tm), lambda bi, i: (bi, 0, 0, nt - 1 - i))
    return pl.pallas_call(
        _inproj_kernel,
        grid=(b, nt),
        in_specs=[
            pl.BlockSpec((1, tm, d), tok),
            pl.BlockSpec((MOD_ROWS, d), lambda bi, i: (0, 0)),
            pl.BlockSpec((MOD_ROWS, d), lambda bi, i: (0, 1)),
            pl.BlockSpec((1, d), lambda bi, i: (0, 0)),
            pl.BlockSpec(memory_space=pl.ANY),
            pl.BlockSpec((tm, HEAD_DIM), lambda bi, i: (nt - 1 - i, 0)),
            pl.BlockSpec((tm, HEAD_DIM), lambda bi, i: (nt - 1 - i, 0)),
            pl.BlockSpec((3, CONV_CH), lambda bi, i: (0, 0)),
            pl.BlockSpec(dec_rows.shape, lambda bi, i: (0, 0)),
            pl.BlockSpec((1, RET_HEADS, HEAD_DIM, HEAD_DIM), lambda bi, i: (bi, 0, 0, 0)),
            slab_spec,
        ],
        out_specs=[
            seq_spec, seq_spec, heads_t_spec, seq_spec, seq_spec,
            pl.BlockSpec((1, 1, RET_HEADS, HEAD_DIM, HEAD_DIM),
                         lambda bi, i: (bi, nt - 1 - i, 0, 0, 0)),
            slab_spec,
        ],
        out_shape=[seq, seq, heads_t, seq, seq,
                   jax.ShapeDtypeStruct((b, nt, RET_HEADS, HEAD_DIM, HEAD_DIM), F32),
                   jax.ShapeDtypeStruct(w_out.shape, BF16)],
        scratch_shapes=[
            pltpu.VMEM((RET_HEADS, HEAD_DIM, HEAD_DIM), F32),
            pltpu.VMEM((RET_HEADS, tm), F32),
            pltpu.VMEM((n_groups, d, r), w_in.dtype),
            pltpu.SemaphoreType.DMA((n_groups,)),
        ],
        compiler_params=pltpu.CompilerParams(
            dimension_semantics=("arbitrary", "arbitrary"),
            vmem_limit_bytes=VMEM_LIMIT),
        name="inproj",
    )(x, mod, mod, norm_g, w_in, cos_t, sin_t, conv_w, dec_rows, sb0, w_out)


def _mix_kernel(q_ref, k_ref, vt_ref, gate_ref, conv_ref, x_ref, sbound_ref, sf0_ref,
                wo_r_ref, wo_c_ref, g1_ref, dec_ref, wup_ref, wdn_ref,
                o_ref, wup_b_ref, wdn_b_ref,
                sf_scr, dmat_scr, qdec_scr, kdec_scr, cdec_scr, ret_scr):
    b = pl.program_id(0)
    i = pl.program_id(1)
    tm = ret_scr.shape[0]
    c = dmat_scr.shape[1]
    nc = tm // c

    wup_b_ref[...] = wup_ref[...].astype(wup_b_ref.dtype)
    wdn_b_ref[...] = wdn_ref[...].astype(wdn_b_ref.dtype)

    @pl.when(i == 0)
    def _():
        sf_scr[...] = sf0_ref[0]

    @pl.when((b == 0) & (i == 0))
    def _():
        lgf = _log_sigmoid(dec_ref[:RET_HEADS])
        lgb = _log_sigmoid(dec_ref[RET_HEADS:])
        ri = lax.broadcasted_iota(jnp.int32, (c, c), 0)
        ci = lax.broadcasted_iota(jnp.int32, (c, c), 1)
        rel = (ri - ci).astype(F32)
        t = lax.broadcasted_iota(jnp.int32, (c, HEAD_DIM), 0).astype(F32)
        tl = lax.broadcasted_iota(jnp.int32, (1, c), 1).astype(F32)
        for hd in range(RET_HEADS):
            f = lgf[hd:hd + 1, :]
            g = lgb[hd:hd + 1, :]
            dmat_scr[hd] = jnp.where(rel >= 0, jnp.exp(f * jnp.maximum(rel, 0.0)),
                                     jnp.exp(g * jnp.maximum(-rel, 0.0))).astype(BF16)
            fh = f[:, :HEAD_DIM]
            gh = g[:, :HEAD_DIM]
            qdec_scr[hd, :, :HEAD_DIM] = jnp.exp(fh * (t + 1.0)).astype(BF16)
            qdec_scr[hd, :, HEAD_DIM:] = jnp.exp(gh * (float(c) - t)).astype(BF16)
            kdec_scr[hd, 0:1, :] = jnp.exp(f * (float(c) - 1.0 - tl))
            kdec_scr[hd, 1:2, :] = jnp.exp(g * tl)
            cdec_scr[hd, :, :HEAD_DIM] = jnp.broadcast_to(jnp.exp(fh * float(c)), (8, HEAD_DIM))
            cdec_scr[hd, :, HEAD_DIM:] = jnp.broadcast_to(jnp.exp(gh * float(c)), (8, HEAD_DIM))

    heads = range(RET_HEADS)
    hsl = [slice(hd * HEAD_DIM, (hd + 1) * HEAD_DIM) for hd in heads]
    crows = [slice(ch * c, (ch + 1) * c) for ch in range(nc)]
    ks = [[k_ref[0, crows[ch], hsl[hd]] for ch in range(nc)] for hd in heads]
    vts = [[vt_ref[0, hd, :, crows[ch]] for ch in range(nc)] for hd in heads]

    qs = [[q_ref[0, crows[ch], hsl[hd]] for ch in range(nc)] for hd in heads]

    def masked_scores(ch):
        return [_dot_nt(qs[hd][ch], ks[hd][ch]).astype(BF16) * dmat_scr[hd] for hd in heads]

    ps_first = masked_scores(0)

    s2ts = []
    for hd in heads:
        cf = cdec_scr[hd, 0:1, :HEAD_DIM]
        cb = cdec_scr[hd, 0:1, HEAD_DIM:]
        sb = [None] * nc
        sb[nc - 1] = sbound_ref[0, 0, hd]
        for ch in range(nc - 1, 0, -1):
            vb = vts[hd][ch] * kdec_scr[hd, 1:2, :].astype(BF16)
            sb[ch - 1] = cb * sb[ch] + _dot(vb, ks[hd][ch])
        sf = [sf_scr[hd]]
        for ch in range(nc):
            vf = vts[hd][ch] * kdec_scr[hd, 0:1, :].astype(BF16)
            sf.append(cf * sf[ch] + _dot(vf, ks[hd][ch]))
        sf_scr[hd] = sf[nc]
        s2ts.append([jnp.concatenate([sf[ch], sb[ch]], axis=1).astype(BF16) for ch in range(nc)])

    def cross_terms(ch):
        out = []
        for hd in heads:
            qh = qs[hd][ch]
            q2 = jnp.concatenate([qh, qh], axis=1) * qdec_scr[hd]
            out.append(_dot_nt(s2ts[hd][ch], q2))
        return out

    crosses = [cross_terms(ch) for ch in range(nc)]
    ps = ps_first
    for ch in range(nc):
        inner = [_dot_nt(vts[hd][ch], ps[hd]) for hd in heads]
        cross = crosses[ch]
        for hd in heads:
            o = (inner[hd] + cross[hd]).T
            ms = jnp.mean(o * o, axis=-1, keepdims=True)
            o = o * lax.rsqrt(ms + EPS) * gate_ref[0, crows[ch], hsl[hd]].astype(F32)
            ret_scr[crows[ch], hsl[hd]] = o.astype(BF16)
        if ch + 1 < nc:
            ps = masked_scores(ch + 1)
        mix = _dot(ret_scr[crows[ch], :], wo_r_ref[...]) + _dot(conv_ref[0, crows[ch], :], wo_c_ref[...])
        o_ref[0, crows[ch], :] = x_ref[0, crows[ch], :] + g1_ref[pl.ds(b, 1), :] * mix


def _mix_call(q, k, vt, gate, conv, x, sbound, sf0, w_out, mod, dec_rows, w_up, w_down):
    b, n, d = x.shape
    tm = MIX_TILE
    nt = n // tm
    bound_per_tile = sbound.shape[1] // nt
    r = RET_WIDTH
    c = RET_CHUNK

    def slab_spec_of(w):
        slab = w.shape[0] // (b * nt)
        assert slab * b * nt == w.shape[0] and slab % 16 == 0
        return pl.BlockSpec((slab, w.shape[1]), lambda bi, i: (bi * nt + i, 0))

    def tok(bi, i):
        return (bi, i, 0)

    seq_spec = pl.BlockSpec((1, tm, r), tok)
    return pl.pallas_call(
        _mix_kernel,
        grid=(b, nt),
        in_specs=[
            seq_spec, seq_spec,
            pl.BlockSpec((1, RET_HEADS, HEAD_DIM, tm), lambda bi, i: (bi, 0, 0, i)),
            seq_spec, seq_spec,
            pl.BlockSpec((1, tm, d), tok),
            pl.BlockSpec((1, 1, RET_HEADS, HEAD_DIM, HEAD_DIM),
                         lambda bi, i: (bi, (i + 1) * bound_per_tile - 1, 0, 0, 0)),
            pl.BlockSpec((1, RET_HEADS, HEAD_DIM, HEAD_DIM), lambda bi, i: (bi, 0, 0, 0)),
            pl.BlockSpec((r, d), lambda bi, i: (0, 0)),
            pl.BlockSpec((CONV_CH, d), lambda bi, i: (1, 0)),
            pl.BlockSpec((MOD_ROWS, d), lambda bi, i: (0, 2)),
            pl.BlockSpec(dec_rows.shape, lambda bi, i: (0, 0)),
            slab_spec_of(w_up),
            slab_spec_of(w_down),
        ],
        out_specs=[pl.BlockSpec((1, tm, d), tok), slab_spec_of(w_up), slab_spec_of(w_down)],
        out_shape=[jax.ShapeDtypeStruct((b, n, d), F32),
                   jax.ShapeDtypeStruct(w_up.shape, BF16),
                   jax.ShapeDtypeStruct(w_down.shape, BF16)],
        scratch_shapes=[
            pltpu.VMEM((RET_HEADS, HEAD_DIM, HEAD_DIM), F32),
            pltpu.VMEM((RET_HEADS, c, c), BF16),
            pltpu.VMEM((RET_HEADS, c, 2 * HEAD_DIM), BF16),
            pltpu.VMEM((RET_HEADS, 8, c), F32),
            pltpu.VMEM((RET_HEADS, 8, 2 * HEAD_DIM), F32),
            pltpu.VMEM((tm, r), BF16),
        ],
        compiler_params=pltpu.CompilerParams(
            dimension_semantics=("arbitrary", "arbitrary"), vmem_limit_bytes=VMEM_LIMIT),
        name="mix",
    )(q, k, vt, gate, conv, x, sbound, sf0, w_out, w_out, mod, dec_rows, w_up, w_down)


def _ffn_kernel(x_ref, sh_ref, sc_ref, gt_ref, g2_ref, gf_ref,
                wa_ref, wb_ref, cw_ref, cb_ref, wd_ref,
                o_ref,
                h_scr):
    b = pl.program_id(0)
    f = pl.program_id(2)
    nf = pl.num_programs(2)
    tm, d = h_scr.shape
    tf = wa_ref.shape[1]
    blk = o_ref.shape[1:]
    step = blk[1]

    def body(first, last):
        if first:
            h = _norm_mod(x_ref[0].reshape(tm, d), g2_ref[...], sh_ref[pl.ds(b, 1), :],
                          sc_ref[pl.ds(b, 1), :]).astype(BF16)
            h_scr[...] = h
        else:
            h = h_scr[...]
        a = _dot(h, wa_ref[...])
        zero = jnp.zeros((step, tf), F32)
        a_up = jnp.concatenate([zero, a[:tm - step]], axis=0)
        a_dn = jnp.concatenate([a[step:], zero], axis=0)
        a = cw_ref[0:1, :] * a_up + cw_ref[1:2, :] * a + cw_ref[2:3, :] * a_dn + cb_ref[...]
        act = (a * _sigmoid(a) * _dot(h, wb_ref[...])).astype(BF16)
        if not last:
            y = _dot(act, wd_ref[...])
            o_ref[0] = (y if first else o_ref[0].reshape(tm, d) + y).reshape(blk)
            return
        groups = LAST_STEP_ROW_GROUPS
        gr = blk[0] // groups
        ys = [_dot(act[g * gr * step:(g + 1) * gr * step], wd_ref[...]) for g in range(groups)]
        for g in range(groups):
            rs = slice(g * gr, (g + 1) * gr)
            acc = o_ref[0, rs].reshape(gr * step, d) + ys[g]
            z = x_ref[0, rs].reshape(gr * step, d) + gt_ref[pl.ds(b, 1), :] * acc
            ms = jnp.mean(z * z, axis=-1, keepdims=True)
            o_ref[0, rs] = (z * lax.rsqrt(ms + EPS) * gf_ref[...]).reshape(gr, step, d)

    @pl.when(f == 0)
    def _():
        body(True, False)

    @pl.when((f > 0) & (f < nf - 1))
    def _():
        body(False, False)

    @pl.when(f == nf - 1)
    def _():
        body(False, True)


def _ffn_call(x, mod, norm_g, final_g, w_up, ffn_conv_w, ffn_conv_b, w_down):
    b, n, d = x.shape
    d_ff = w_down.shape[0]
    rows = n // GRID_W
    cols = FFN_COLS
    tf = FF_TILE
    nf = d_ff // tf
    assert nf >= 2
    xg = x.reshape(b, rows, GRID_W, d)
    tile = pl.BlockSpec((1, rows, cols, d), lambda bi, i, f: (bi, 0, i, 0))

    out = pl.pallas_call(
        _ffn_kernel,
        grid=(b, GRID_W // cols, nf),
        in_specs=[
            tile,
            pl.BlockSpec((MOD_ROWS, d), lambda bi, i, f: (0, 3)),
            pl.BlockSpec((MOD_ROWS, d), lambda bi, i, f: (0, 4)),
            pl.BlockSpec((MOD_ROWS, d), lambda bi, i, f: (0, 5)),
            pl.BlockSpec((1, d), lambda bi, i, f: (0, 0)),
            pl.BlockSpec((1, d), lambda bi, i, f: (0, 0)),
            pl.BlockSpec((d, tf), lambda bi, i, f: (0, f)),
            pl.BlockSpec((d, tf), lambda bi, i, f: (0, nf + f)),
            pl.BlockSpec((3, tf), lambda bi, i, f: (0, f)),
            pl.BlockSpec((1, tf), lambda bi, i, f: (0, f)),
            pl.BlockSpec((tf, d), lambda bi, i, f: (f, 0)),
        ],
        out_specs=tile,
        out_shape=jax.ShapeDtypeStruct((b, rows, GRID_W, d), F32),
        scratch_shapes=[
            pltpu.VMEM((rows * cols, d), BF16),
        ],
        compiler_params=pltpu.CompilerParams(
            dimension_semantics=("arbitrary", "arbitrary", "arbitrary"),
            vmem_limit_bytes=VMEM_LIMIT),
        name="ffn",
    )(xg, mod, mod, mod, norm_g, final_g, w_up, w_up, ffn_conv_w, ffn_conv_b, w_down)
    return out.reshape(b, n, d)


def _rope_tables(n):
    pos = np.arange(n, dtype=np.int32)
    row = (pos // GRID_W).astype(np.float32)
    col = (pos % GRID_W).astype(np.float32)
    dh = HEAD_DIM // 2
    freqs = np.float32(ROPE_THETA) ** (-np.arange(0, dh, 2, dtype=np.float32) / np.float32(dh))
    parts_c, parts_s = [], []
    for p in (row, col):
        ang = (p[:, None] * freqs[None, :]).astype(np.float32)
        cos = np.cos(ang).astype(np.float32)
        sin = np.sin(ang).astype(np.float32)
        parts_c += [cos, cos]
        parts_s += [-sin, sin]
    return (jnp.asarray(np.concatenate(parts_c, axis=-1)),
            jnp.asarray(np.concatenate(parts_s, axis=-1)))


def kernel(x, c, ctx, c_ctx, w_mod, b_mod, norm1_g, w_in, ret_decay_fwd, ret_decay_bwd,
           conv_w, w_out, norm2_g, w_up, ffn_conv_w, ffn_conv_b, w_down, final_g):
    b, n, d = x.shape
    depth = w_mod.shape[0]
    assert depth == 1 and n % MIX_TILE == 0 and n % GRID_W == 0
    assert MIX_TILE % INPROJ_TILE == 0 and MIX_TILE % RET_CHUNK == 0
    cos_t, sin_t = _rope_tables(n)
    craw = jnp.concatenate(
        [c, c_ctx[None, :], jnp.zeros((MOD_ROWS - b - 1, d), F32)], axis=0)

    layer = 0
    dec_rows = jnp.broadcast_to(
        jnp.concatenate([ret_decay_fwd[layer], ret_decay_bwd[layer]])[:, None],
        (2 * RET_HEADS, RET_CHUNK))
    g1 = norm1_g[layer][None, :]

    mod, w_in_b, sf0, sb0 = _mod_ctx_call(craw, w_mod[layer], b_mod[layer][None, :], w_in[layer],
                                          ctx, g1, dec_rows, b)
    q, k, vt, gate, conv, sbound, w_out_b = _inproj_call(
        x, mod, g1, w_in_b, cos_t, sin_t, conv_w[layer], dec_rows, sb0, w_out[layer])
    x_mid, w_up_b, w_down_b = _mix_call(q, k, vt, gate, conv, x, sbound, sf0, w_out_b, mod,
                                        dec_rows, w_up[layer], w_down[layer])
    return _ffn_call(x_mid, mod, norm2_g[layer][None, :], final_g[None, :], w_up_b,
                     ffn_conv_w[layer], ffn_conv_b[layer][None, :], w_down_b)
```

```python
import functools

import jax
import jax.numpy as jnp
import numpy as np
from jax import lax
from jax.experimental import pallas as pl
from jax.experimental.pallas import tpu as pltpu

F32 = jnp.float32
BF16 = jnp.bfloat16

GRID_W = 64
RET_HEADS = 8
HEAD_DIM = 128
RET_WIDTH = RET_HEADS * HEAD_DIM
CONV_CH = 1024
EPS = 1e-6
ROPE_THETA = 10000.0

MOD_ROWS = 8
INPROJ_TILE = 512
MIX_TILE = 512
FFN_COLS = 8
RET_CHUNK = 256
FF_TILE = 512
MOD_COL_TILE = 768
LAST_STEP_ROW_GROUPS = 4
V7X_VMEM_BYTES = 64 * 1024 * 1024
VMEM_LIMIT = V7X_VMEM_BYTES - 4 * 1024 * 1024


def _sigmoid(x):
    return 1.0 / (1.0 + jnp.exp(-x))


def _log_sigmoid(x):
    return jnp.minimum(x, 0.0) - jnp.log(1.0 + jnp.exp(-jnp.abs(x)))


def _norm_mod(x, gain, shift, scale):
    ms = jnp.mean(x * x, axis=-1, keepdims=True)
    y = x * lax.rsqrt(ms + EPS) * gain
    return y * (1.0 + scale) + shift


def _dot(a, b):
    return jnp.dot(a, b, preferred_element_type=F32)


def _dot_t0(a, b):
    return lax.dot_general(a, b, (((0,), (0,)), ((), ())), preferred_element_type=F32)


def _dot_nt(a, b):
    return lax.dot_general(a, b, (((1,), (1,)), ((), ())), preferred_element_type=F32)


def _head_lanes(rows):
    return jnp.concatenate([rows[hd:hd + 1, :HEAD_DIM] for hd in range(RET_HEADS)], axis=1)


def _mod_ctx_kernel(ctx_row, c_ref, w_ref, b_ref, win_ref, ctx_ref, g_ref, dec_ref,
                    o_ref, win_b_ref, sf_ref, sb_ref,
                    modc_scr, hc_scr, kv_scr):
    j = pl.program_id(0)
    steps = pl.num_programs(0)
    first_slab_step = steps - hc_scr.shape[0]
    n_keep, _, tn = modc_scr.shape
    nb, length, d = ctx_ref.shape
    slab = win_ref.shape[0]
    r = RET_WIDTH

    a = c_ref[...]
    a = (a * _sigmoid(a)).astype(BF16)
    out = _dot(a, w_ref[...].astype(BF16)) + b_ref[...]
    o_ref[...] = out

    @pl.when(j < n_keep)
    def _():
        modc_scr[j] = out

    @pl.when(j == first_slab_step)
    def _():
        row = jnp.concatenate([modc_scr[t][ctx_row:ctx_row + 1, :] for t in range(n_keep)], axis=1)
        x = ctx_ref[...].reshape(nb * length, d)
        hc = _norm_mod(x, g_ref[...], row[:, 0:d], row[:, d:2 * d]).astype(BF16)
        for t in range(hc_scr.shape[0]):
            hc_scr[t] = hc[:, t * slab:(t + 1) * slab]
        kv_scr[...] = jnp.zeros_like(kv_scr)

    @pl.when(j >= first_slab_step)
    def _():
        wb = win_ref[...].astype(BF16)
        win_b_ref[...] = wb
        kv_scr[...] += _dot(hc_scr[j - first_slab_step], wb[:, r:3 * r])

    @pl.when(j == steps - 1)
    def _():
        k = kv_scr[:, 0:r] * (HEAD_DIM ** -0.5)
        v = kv_scr[:, r:2 * r].astype(BF16)
        t = (lax.broadcasted_iota(jnp.int32, k.shape, 0) % length).astype(F32)
        lg = _log_sigmoid(dec_ref[...])
        kf = (k * jnp.exp(_head_lanes(lg[:RET_HEADS]) * (length - 1.0 - t))).astype(BF16)
        kb = (k * jnp.exp(_head_lanes(lg[RET_HEADS:]) * t)).astype(BF16)
        for bi in range(nb):
            rows = slice(bi * length, (bi + 1) * length)
            for hd in range(RET_HEADS):
                sl = slice(hd * HEAD_DIM, (hd + 1) * HEAD_DIM)
                sf_ref[bi, hd] = _dot_t0(v[rows, sl], kf[rows, sl])
                sb_ref[bi, hd] = _dot_t0(v[rows, sl], kb[rows, sl])


def _mod_ctx_call(craw, w_mod, b_mod, w_in, ctx, norm_g, dec_rows, ctx_row):
    d, n = w_mod.shape
    b, length, _ = ctx.shape
    tn = MOD_COL_TILE
    steps = n // tn
    n_slabs = steps // 2
    slab = w_in.shape[0] // n_slabs
    n_keep = -(-2 * d // tn)
    assert steps * tn == n and slab * n_slabs == w_in.shape[0] and slab % 128 == 0
    assert n_keep <= steps - n_slabs

    def slab_idx(j):
        return (jnp.maximum(j - (steps - n_slabs), 0), 0)

    slab_spec = pl.BlockSpec((slab, w_in.shape[1]), slab_idx)
    state = jax.ShapeDtypeStruct((b, RET_HEADS, HEAD_DIM, HEAD_DIM), F32)
    state_spec = pl.BlockSpec(state.shape, lambda j: (0, 0, 0, 0))
    return pl.pallas_call(
        functools.partial(_mod_ctx_kernel, ctx_row),
        grid=(steps,),
        in_specs=[
            pl.BlockSpec((MOD_ROWS, d), lambda j: (0, 0)),
            pl.BlockSpec((d, tn), lambda j: (0, j)),
            pl.BlockSpec((1, tn), lambda j: (0, j)),
            slab_spec,
            pl.BlockSpec((b, length, d), lambda j: (0, 0, 0)),
            pl.BlockSpec((1, d), lambda j: (0, 0)),
            pl.BlockSpec(dec_rows.shape, lambda j: (0, 0)),
        ],
        out_specs=[pl.BlockSpec((MOD_ROWS, tn), lambda j: (0, j)), slab_spec, state_spec, state_spec],
        out_shape=[jax.ShapeDtypeStruct((MOD_ROWS, n), F32),
                   jax.ShapeDtypeStruct(w_in.shape, BF16), state, state],
        scratch_shapes=[
            pltpu.VMEM((n_keep, MOD_ROWS, tn), F32),
            pltpu.VMEM((n_slabs, b * length, slab), BF16),
            pltpu.VMEM((b * length, 2 * RET_WIDTH), F32),
        ],
        compiler_params=pltpu.CompilerParams(
            dimension_semantics=("arbitrary",), vmem_limit_bytes=VMEM_LIMIT),
        name="mod_ctx",
    )(craw, w_mod, b_mod, w_in, ctx, norm_g, dec_rows)


def _rope(acc, cos, sin, scale):
    lane = lax.broadcasted_iota(jnp.int32, (1, HEAD_DIM), 1)
    lo = (lane % 64) < 32
    outs = []
    for hd in range(RET_HEADS):
        xh = acc[:, hd * HEAD_DIM:(hd + 1) * HEAD_DIM]
        partner = jnp.where(lo, pltpu.roll(xh, 96, 1), pltpu.roll(xh, 32, 1))
        y = xh * cos + partner * sin
        outs.append(y if scale == 1.0 else y * scale)
    return outs


def _inproj_kernel(x_ref, sh_ref, sc_ref, g_ref, w_ref, cos_ref, sin_ref, cw_ref, dec_ref,
                   sb0_ref, wout_ref, wdn_ref,
                   q_ref, k_ref, vt_ref, gate_ref, conv_ref, sbound_ref, wout_b_ref, wdn_b_ref,
                   s_scr, dec_scr):
    b = pl.program_id(0)
    i = pl.program_id(1)
    tm = dec_scr.shape[1]
    r = RET_WIDTH

    wout_b_ref[...] = wout_ref[...].astype(wout_b_ref.dtype)
    wdn_b_ref[...] = wdn_ref[...].astype(wdn_b_ref.dtype)

    @pl.when(i == 0)
    def _():
        s_scr[...] = sb0_ref[0]
        t = lax.broadcasted_iota(jnp.int32, dec_scr.shape, 1).astype(F32)
        lgb = _log_sigmoid(dec_ref[RET_HEADS:, 0:1])
        dec_scr[...] = jnp.exp(lgb * t)

    h = _norm_mod(x_ref[0], g_ref[...], sh_ref[pl.ds(b, 1), :], sc_ref[pl.ds(b, 1), :]).astype(BF16)

    def proj(group):
        return _dot(h, w_ref[:, group * r:(group + 1) * r])

    cos = cos_ref[...]
    sin = sin_ref[...]
    for hd, qh in enumerate(_rope(proj(0), cos, sin, 1.0)):
        q_ref[0, :, hd * HEAD_DIM:(hd + 1) * HEAD_DIM] = qh.astype(q_ref.dtype)
    ks = [kh.astype(BF16) for kh in _rope(proj(1), cos, sin, HEAD_DIM ** -0.5)]
    for hd in range(RET_HEADS):
        k_ref[0, :, hd * HEAD_DIM:(hd + 1) * HEAD_DIM] = ks[hd]
    v = proj(2)
    tile_dec = jnp.exp(_log_sigmoid(dec_ref[RET_HEADS:, :HEAD_DIM]) * float(tm))
    for hd in range(RET_HEADS):
        vt = v[:, hd * HEAD_DIM:(hd + 1) * HEAD_DIM].T
        vt_ref[0, hd] = vt.astype(vt_ref.dtype)
        vtd = (vt * dec_scr[hd:hd + 1, :]).astype(BF16)
        s_old = s_scr[hd]
        sbound_ref[0, 0, hd] = s_old
        s_scr[hd] = tile_dec[hd:hd + 1, :] * s_old + _dot(vtd, ks[hd])

    g = proj(3)
    gate_ref[0] = (g * _sigmoid(g)).astype(gate_ref.dtype)

    u = proj(5) * proj(6)
    col = lax.broadcasted_iota(jnp.int32, (tm, 1), 0) % GRID_W
    up = jnp.where(col == 0, 0.0, pltpu.roll(u, 1, 0))
    un = jnp.where(col == GRID_W - 1, 0.0, pltpu.roll(u, tm - 1, 0))
    y = proj(4) * (cw_ref[0:1, :] * up + cw_ref[1:2, :] * u + cw_ref[2:3, :] * un)
    conv_ref[0] = y.astype(conv_ref.dtype)


def _inproj_call(x, mod, norm_g, w_in, cos_t, sin_t, conv_w, dec_rows, sb0, w_out, w_down):
    b, n, d = x.shape
    tm = INPROJ_TILE
    nt = n // tm
    r = RET_WIDTH

    def slab_spec_of(w):
        slab = w.shape[0] // (b * nt)
        assert slab * b * nt == w.shape[0] and slab % 16 == 0
        return pl.BlockSpec((slab, w.shape[1]), lambda bi, i: (bi * nt + i, 0))

    later_weights = (w_out, w_down)

    def tok(bi, i):
        return (bi, nt - 1 - i, 0)

    seq = jax.ShapeDtypeStruct((b, n, r), BF16)
    seq_spec = pl.BlockSpec((1, tm, r), tok)
    heads_t = jax.ShapeDtypeStruct((b, RET_HEADS, HEAD_DIM, n), BF16)
    heads_t_spec = pl.BlockSpec((1, RET_HEADS, HEAD_DIM, tm), lambda bi, i: (bi, 0, 0, nt - 1 - i))
    return pl.pallas_call(
        _inproj_kernel,
        grid=(b, nt),
        in_specs=[
            pl.BlockSpec((1, tm, d), tok),
            pl.BlockSpec((MOD_ROWS, d), lambda bi, i: (0, 0)),
            pl.BlockSpec((MOD_ROWS, d), lambda bi, i: (0, 1)),
            pl.BlockSpec((1, d), lambda bi, i: (0, 0)),
            pl.BlockSpec(w_in.shape, lambda bi, i: (0, 0), pipeline_mode=pl.Buffered(1)),
            pl.BlockSpec((tm, HEAD_DIM), lambda bi, i: (nt - 1 - i, 0)),
            pl.BlockSpec((tm, HEAD_DIM), lambda bi, i: (nt - 1 - i, 0)),
            pl.BlockSpec((3, CONV_CH), lambda bi, i: (0, 0)),
            pl.BlockSpec(dec_rows.shape, lambda bi, i: (0, 0)),
            pl.BlockSpec((1, RET_HEADS, HEAD_DIM, HEAD_DIM), lambda bi, i: (bi, 0, 0, 0)),
            *[slab_spec_of(w) for w in later_weights],
        ],
        out_specs=[
            seq_spec, seq_spec, heads_t_spec, seq_spec, seq_spec,
            pl.BlockSpec((1, 1, RET_HEADS, HEAD_DIM, HEAD_DIM),
                         lambda bi, i: (bi, nt - 1 - i, 0, 0, 0)),
            *[slab_spec_of(w) for w in later_weights],
        ],
        out_shape=[seq, seq, heads_t, seq, seq,
                   jax.ShapeDtypeStruct((b, nt, RET_HEADS, HEAD_DIM, HEAD_DIM), F32),
                   *[jax.ShapeDtypeStruct(w.shape, BF16) for w in later_weights]],
        scratch_shapes=[
            pltpu.VMEM((RET_HEADS, HEAD_DIM, HEAD_DIM), F32),
            pltpu.VMEM((RET_HEADS, tm), F32),
        ],
        compiler_params=pltpu.CompilerParams(
            dimension_semantics=("arbitrary", "arbitrary"),
            vmem_limit_bytes=V7X_VMEM_BYTES - 1024 * 1024),
        name="inproj",
    )(x, mod, mod, norm_g, w_in, cos_t, sin_t, conv_w, dec_rows, sb0, *later_weights)


def _mix_kernel(q_ref, k_ref, vt_ref, gate_ref, conv_ref, x_ref, sbound_ref, sf0_ref,
                wo_r_ref, wo_c_ref, g1_ref, dec_ref, wup_ref,
                o_ref, wup_b_ref,
                sf_scr, dmat_scr, qdec_scr, kdec_scr, cdec_scr, ret_scr):
    b = pl.program_id(0)
    i = pl.program_id(1)
    tm = ret_scr.shape[0]
    c = dmat_scr.shape[1]
    nc = tm // c

    wup_b_ref[...] = wup_ref[...].astype(wup_b_ref.dtype)

    @pl.when(i == 0)
    def _():
        sf_scr[...] = sf0_ref[0]

    @pl.when((b == 0) & (i == 0))
    def _():
        lgf = _log_sigmoid(dec_ref[:RET_HEADS])
        lgb = _log_sigmoid(dec_ref[RET_HEADS:])
        ri = lax.broadcasted_iota(jnp.int32, (c, c), 0)
        ci = lax.broadcasted_iota(jnp.int32, (c, c), 1)
        rel = (ri - ci).astype(F32)
        t = lax.broadcasted_iota(jnp.int32, (c, HEAD_DIM), 0).astype(F32)
        tl = lax.broadcasted_iota(jnp.int32, (1, c), 1).astype(F32)
        for hd in range(RET_HEADS):
            f = lgf[hd:hd + 1, :]
            g = lgb[hd:hd + 1, :]
            dmat_scr[hd] = jnp.where(rel >= 0, jnp.exp(f * jnp.maximum(rel, 0.0)),
                                     jnp.exp(g * jnp.maximum(-rel, 0.0))).astype(BF16)
            fh = f[:, :HEAD_DIM]
            gh = g[:, :HEAD_DIM]
            qdec_scr[hd, :, :HEAD_DIM] = jnp.exp(fh * (t + 1.0)).astype(BF16)
            qdec_scr[hd, :, HEAD_DIM:] = jnp.exp(gh * (float(c) - t)).astype(BF16)
            kdec_scr[hd, 0:1, :] = jnp.exp(f * (float(c) - 1.0 - tl))
            kdec_scr[hd, 1:2, :] = jnp.exp(g * tl)
            cdec_scr[hd, :, :HEAD_DIM] = jnp.broadcast_to(jnp.exp(fh * float(c)), (8, HEAD_DIM))
            cdec_scr[hd, :, HEAD_DIM:] = jnp.broadcast_to(jnp.exp(gh * float(c)), (8, HEAD_DIM))

    heads = range(RET_HEADS)
    hsl = [slice(hd * HEAD_DIM, (hd + 1) * HEAD_DIM) for hd in heads]
    crows = [slice(ch * c, (ch + 1) * c) for ch in range(nc)]
    ks = [[k_ref[0, crows[ch], hsl[hd]] for ch in range(nc)] for hd in heads]
    vts = [[vt_ref[0, hd, :, crows[ch]] for ch in range(nc)] for hd in heads]

    qs = [[q_ref[0, crows[ch], hsl[hd]] for ch in range(nc)] for hd in heads]

    def masked_scores(ch):
        return [_dot_nt(qs[hd][ch], ks[hd][ch]).astype(BF16) * dmat_scr[hd] for hd in heads]

    ps_first = masked_scores(0)

    s2ts = []
    for hd in heads:
        cf = cdec_scr[hd, 0:1, :HEAD_DIM]
        cb = cdec_scr[hd, 0:1, HEAD_DIM:]
        sb = [None] * nc
        sb[nc - 1] = sbound_ref[0, 0, hd]
        for ch in range(nc - 1, 0, -1):
            vb = vts[hd][ch] * kdec_scr[hd, 1:2, :].astype(BF16)
            sb[ch - 1] = cb * sb[ch] + _dot(vb, ks[hd][ch])
        sf = [sf_scr[hd]]
        for ch in range(nc):
            vf = vts[hd][ch] * kdec_scr[hd, 0:1, :].astype(BF16)
            sf.append(cf * sf[ch] + _dot(vf, ks[hd][ch]))
        sf_scr[hd] = sf[nc]
        s2ts.append([jnp.concatenate([sf[ch], sb[ch]], axis=1).astype(BF16) for ch in range(nc)])

    def cross_terms(ch):
        out = []
        for hd in heads:
            qh = qs[hd][ch]
            q2 = jnp.concatenate([qh, qh], axis=1) * qdec_scr[hd]
            out.append(_dot_nt(s2ts[hd][ch], q2))
        return out

    crosses = [cross_terms(ch) for ch in range(nc)]
    ps = ps_first
    for ch in range(nc):
        inner = [_dot_nt(vts[hd][ch], ps[hd]) for hd in heads]
        cross = crosses[ch]
        for hd in heads:
            o = (inner[hd] + cross[hd]).T
            ms = jnp.mean(o * o, axis=-1, keepdims=True)
            o = o * lax.rsqrt(ms + EPS) * gate_ref[0, crows[ch], hsl[hd]].astype(F32)
            ret_scr[crows[ch], hsl[hd]] = o.astype(BF16)
        if ch + 1 < nc:
            ps = masked_scores(ch + 1)
        mix = _dot(ret_scr[crows[ch], :], wo_r_ref[...]) + _dot(conv_ref[0, crows[ch], :], wo_c_ref[...])
        o_ref[0, crows[ch], :] = x_ref[0, crows[ch], :] + g1_ref[pl.ds(b, 1), :] * mix


def _mix_call(q, k, vt, gate, conv, x, sbound, sf0, w_out, mod, dec_rows, w_up):
    b, n, d = x.shape
    tm = MIX_TILE
    nt = n // tm
    bound_per_tile = sbound.shape[1] // nt
    r = RET_WIDTH
    c = RET_CHUNK
    slab = w_up.shape[0] // (b * nt)
    assert slab * b * nt == w_up.shape[0] and slab % 16 == 0
    slab_spec = pl.BlockSpec((slab, w_up.shape[1]), lambda bi, i: (bi * nt + i, 0))

    def tok(bi, i):
        return (bi, i, 0)

    seq_spec = pl.BlockSpec((1, tm, r), tok)
    return pl.pallas_call(
        _mix_kernel,
        grid=(b, nt),
        in_specs=[
            seq_spec, seq_spec,
            pl.BlockSpec((1, RET_HEADS, HEAD_DIM, tm), lambda bi, i: (bi, 0, 0, i)),
            seq_spec, seq_spec,
            pl.BlockSpec((1, tm, d), tok),
            pl.BlockSpec((1, 1, RET_HEADS, HEAD_DIM, HEAD_DIM),
                         lambda bi, i: (bi, (i + 1) * bound_per_tile - 1, 0, 0, 0)),
            pl.BlockSpec((1, RET_HEADS, HEAD_DIM, HEAD_DIM), lambda bi, i: (bi, 0, 0, 0)),
            pl.BlockSpec((r, d), lambda bi, i: (0, 0)),
            pl.BlockSpec((CONV_CH, d), lambda bi, i: (1, 0)),
            pl.BlockSpec((MOD_ROWS, d), lambda bi, i: (0, 2)),
            pl.BlockSpec(dec_rows.shape, lambda bi, i: (0, 0)),
            slab_spec,
        ],
        out_specs=[pl.BlockSpec((1, tm, d), tok), slab_spec],
        out_shape=[jax.ShapeDtypeStruct((b, n, d), F32), jax.ShapeDtypeStruct(w_up.shape, BF16)],
        scratch_shapes=[
            pltpu.VMEM((RET_HEADS, HEAD_DIM, HEAD_DIM), F32),
            pltpu.VMEM((RET_HEADS, c, c), BF16),
            pltpu.VMEM((RET_HEADS, c, 2 * HEAD_DIM), BF16),
            pltpu.VMEM((RET_HEADS, 8, c), F32),
            pltpu.VMEM((RET_HEADS, 8, 2 * HEAD_DIM), F32),
            pltpu.VMEM((tm, r), BF16),
        ],
        compiler_params=pltpu.CompilerParams(
            dimension_semantics=("arbitrary", "arbitrary"), vmem_limit_bytes=VMEM_LIMIT),
        name="mix",
    )(q, k, vt, gate, conv, x, sbound, sf0, w_out, w_out, mod, dec_rows, w_up)


def _ffn_kernel(x_ref, sh_ref, sc_ref, gt_ref, g2_ref, gf_ref,
                wa_ref, wb_ref, cw_ref, cb_ref, wd_ref,
                o_ref,
                h_scr):
    b = pl.program_id(0)
    f = pl.program_id(2)
    nf = pl.num_programs(2)
    tm, d = h_scr.shape
    tf = wa_ref.shape[1]
    blk = o_ref.shape[1:]
    step = blk[1]

    def body(first, last):
        if first:
            h = _norm_mod(x_ref[0].reshape(tm, d), g2_ref[...], sh_ref[pl.ds(b, 1), :],
                          sc_ref[pl.ds(b, 1), :]).astype(BF16)
            h_scr[...] = h
        else:
            h = h_scr[...]
        a = _dot(h, wa_ref[...])
        zero = jnp.zeros((step, tf), F32)
        a_up = jnp.concatenate([zero, a[:tm - step]], axis=0)
        a_dn = jnp.concatenate([a[step:], zero], axis=0)
        a = cw_ref[0:1, :] * a_up + cw_ref[1:2, :] * a + cw_ref[2:3, :] * a_dn + cb_ref[...]
        act = (a * _sigmoid(a) * _dot(h, wb_ref[...])).astype(BF16)
        if not last:
            y = _dot(act, wd_ref[...])
            o_ref[0] = (y if first else o_ref[0].reshape(tm, d) + y).reshape(blk)
            return
        groups = LAST_STEP_ROW_GROUPS
        gr = blk[0] // groups
        ys = [_dot(act[g * gr * step:(g + 1) * gr * step], wd_ref[...]) for g in range(groups)]
        for g in range(groups):
            rs = slice(g * gr, (g + 1) * gr)
            acc = o_ref[0, rs].reshape(gr * step, d) + ys[g]
            z = x_ref[0, rs].reshape(gr * step, d) + gt_ref[pl.ds(b, 1), :] * acc
            ms = jnp.mean(z * z, axis=-1, keepdims=True)
            o_ref[0, rs] = (z * lax.rsqrt(ms + EPS) * gf_ref[...]).reshape(gr, step, d)

    @pl.when(f == 0)
    def _():
        body(True, False)

    @pl.when((f > 0) & (f < nf - 1))
    def _():
        body(False, False)

    @pl.when(f == nf - 1)
    def _():
        body(False, True)


def _ffn_call(x, mod, norm_g, final_g, w_up, ffn_conv_w, ffn_conv_b, w_down):
    b, n, d = x.shape
    d_ff = w_down.shape[0]
    rows = n // GRID_W
    cols = FFN_COLS
    tf = FF_TILE
    nf = d_ff // tf
    assert nf >= 2
    xg = x.reshape(b, rows, GRID_W, d)
    tile = pl.BlockSpec((1, rows, cols, d), lambda bi, i, f: (bi, 0, i, 0))

    out = pl.pallas_call(
        _ffn_kernel,
        grid=(b, GRID_W // cols, nf),
        in_specs=[
            tile,
            pl.BlockSpec((MOD_ROWS, d), lambda bi, i, f: (0, 3)),
            pl.BlockSpec((MOD_ROWS, d), lambda bi, i, f: (0, 4)),
            pl.BlockSpec((MOD_ROWS, d), lambda bi, i, f: (0, 5)),
            pl.BlockSpec((1, d), lambda bi, i, f: (0, 0)),
            pl.BlockSpec((1, d), lambda bi, i, f: (0, 0)),
            pl.BlockSpec((d, tf), lambda bi, i, f: (0, f)),
            pl.BlockSpec((d, tf), lambda bi, i, f: (0, nf + f)),
            pl.BlockSpec((3, tf), lambda bi, i, f: (0, f)),
            pl.BlockSpec((1, tf), lambda bi, i, f: (0, f)),
            pl.BlockSpec((tf, d), lambda bi, i, f: (f, 0)),
        ],
        out_specs=tile,
        out_shape=jax.ShapeDtypeStruct((b, rows, GRID_W, d), F32),
        scratch_shapes=[
            pltpu.VMEM((rows * cols, d), BF16),
        ],
        compiler_params=pltpu.CompilerParams(
            dimension_semantics=("arbitrary", "arbitrary", "arbitrary"),
            vmem_limit_bytes=VMEM_LIMIT),
        name="ffn",
    )(xg, mod, mod, mod, norm_g, final_g, w_up, w_up, ffn_conv_w, ffn_conv_b, w_down)
    return out.reshape(b, n, d)


def _rope_tables(n):
    pos = np.arange(n, dtype=np.int32)
    row = (pos // GRID_W).astype(np.float32)
    col = (pos % GRID_W).astype(np.float32)
    dh = HEAD_DIM // 2
    freqs = np.float32(ROPE_THETA) ** (-np.arange(0, dh, 2, dtype=np.float32) / np.float32(dh))
    parts_c, parts_s = [], []
    for p in (row, col):
        ang = (p[:, None] * freqs[None, :]).astype(np.float32)
        cos = np.cos(ang).astype(np.float32)
        sin = np.sin(ang).astype(np.float32)
        parts_c += [cos, cos]
        parts_s += [-sin, sin]
    return (jnp.asarray(np.concatenate(parts_c, axis=-1)),
            jnp.asarray(np.concatenate(parts_s, axis=-1)))


def kernel(x, c, ctx, c_ctx, w_mod, b_mod, norm1_g, w_in, ret_decay_fwd, ret_decay_bwd,
           conv_w, w_out, norm2_g, w_up, ffn_conv_w, ffn_conv_b, w_down, final_g):
    b, n, d = x.shape
    depth = w_mod.shape[0]
    assert depth == 1 and n % MIX_TILE == 0 and n % GRID_W == 0
    assert MIX_TILE % INPROJ_TILE == 0 and MIX_TILE % RET_CHUNK == 0
    cos_t, sin_t = _rope_tables(n)
    craw = jnp.concatenate(
        [c, c_ctx[None, :], jnp.zeros((MOD_ROWS - b - 1, d), F32)], axis=0)

    layer = 0
    dec_rows = jnp.broadcast_to(
        jnp.concatenate([ret_decay_fwd[layer], ret_decay_bwd[layer]])[:, None],
        (2 * RET_HEADS, RET_CHUNK))
    g1 = norm1_g[layer][None, :]

    mod, w_in_b, sf0, sb0 = _mod_ctx_call(craw, w_mod[layer], b_mod[layer][None, :], w_in[layer],
                                          ctx, g1, dec_rows, b)
    q, k, vt, gate, conv, sbound, w_out_b, w_down_b = _inproj_call(
        x, mod, g1, w_in_b, cos_t, sin_t, conv_w[layer], dec_rows, sb0, w_out[layer],
        w_down[layer])
    x_mid, w_up_b = _mix_call(q, k, vt, gate, conv, x, sbound, sf0, w_out_b, mod, dec_rows,
                              w_up[layer])
    return _ffn_call(x_mid, mod, norm2_g[layer][None, :], final_g[None, :], w_up_b,
                     ffn_conv_w[layer], ffn_conv_b[layer][None, :], w_down_b)
```

```python
import functools

import jax
import jax.numpy as jnp
import numpy as np
from jax import lax
from jax.experimental import pallas as pl
from jax.experimental.pallas import tpu as pltpu

F32 = jnp.float32
BF16 = jnp.bfloat16

GRID_W = 64
RET_HEADS = 8
HEAD_DIM = 128
RET_WIDTH = RET_HEADS * HEAD_DIM
CONV_CH = 1024
EPS = 1e-6
ROPE_THETA = 10000.0

MOD_ROWS = 8
INPROJ_TILE = 512
MIX_TILE = 512
FFN_COLS = 8
RET_CHUNK = 256
FF_TILE = 512
MOD_COL_TILE = 768
LAST_STEP_ROW_GROUPS = 4
V7X_VMEM_BYTES = 64 * 1024 * 1024
VMEM_LIMIT = V7X_VMEM_BYTES - 4 * 1024 * 1024


def _sigmoid(x):
    return 1.0 / (1.0 + jnp.exp(-x))


def _log_sigmoid(x):
    return jnp.minimum(x, 0.0) - jnp.log(1.0 + jnp.exp(-jnp.abs(x)))


def _norm_mod(x, gain, shift, scale):
    ms = jnp.mean(x * x, axis=-1, keepdims=True)
    y = x * lax.rsqrt(ms + EPS) * gain
    return y * (1.0 + scale) + shift


def _dot(a, b):
    return jnp.dot(a, b, preferred_element_type=F32)


def _dot_t0(a, b):
    return lax.dot_general(a, b, (((0,), (0,)), ((), ())), preferred_element_type=F32)


def _dot_nt(a, b):
    return lax.dot_general(a, b, (((1,), (1,)), ((), ())), preferred_element_type=F32)


def _head_lanes(rows):
    return jnp.concatenate([rows[hd:hd + 1, :HEAD_DIM] for hd in range(RET_HEADS)], axis=1)


def _mod_ctx_kernel(ctx_row, c_ref, w_ref, b_ref, win_ref, ctx_ref, g_ref, dec_ref,
                    o_ref, win_b_ref, sf_ref, sb_ref,
                    modc_scr, hc_scr, kv_scr):
    j = pl.program_id(0)
    steps = pl.num_programs(0)
    first_slab_step = steps - hc_scr.shape[0]
    n_keep, _, tn = modc_scr.shape
    nb, length, d = ctx_ref.shape
    slab = win_ref.shape[0]
    r = RET_WIDTH

    a = c_ref[...]
    a = (a * _sigmoid(a)).astype(BF16)
    out = _dot(a, w_ref[...].astype(BF16)) + b_ref[...]
    o_ref[...] = out

    @pl.when(j < n_keep)
    def _():
        modc_scr[j] = out

    @pl.when(j == first_slab_step)
    def _():
        row = jnp.concatenate([modc_scr[t][ctx_row:ctx_row + 1, :] for t in range(n_keep)], axis=1)
        x = ctx_ref[...].reshape(nb * length, d)
        hc = _norm_mod(x, g_ref[...], row[:, 0:d], row[:, d:2 * d]).astype(BF16)
        for t in range(hc_scr.shape[0]):
            hc_scr[t] = hc[:, t * slab:(t + 1) * slab]
        kv_scr[...] = jnp.zeros_like(kv_scr)

    @pl.when(j >= first_slab_step)
    def _():
        wb = win_ref[...].astype(BF16)
        win_b_ref[...] = wb
        kv_scr[...] += _dot(hc_scr[j - first_slab_step], wb[:, r:3 * r])

    @pl.when(j == steps - 1)
    def _():
        k = kv_scr[:, 0:r] * (HEAD_DIM ** -0.5)
        v = kv_scr[:, r:2 * r].astype(BF16)
        t = (lax.broadcasted_iota(jnp.int32, k.shape, 0) % length).astype(F32)
        lg = _log_sigmoid(dec_ref[...])
        kf = (k * jnp.exp(_head_lanes(lg[:RET_HEADS]) * (length - 1.0 - t))).astype(BF16)
        kb = (k * jnp.exp(_head_lanes(lg[RET_HEADS:]) * t)).astype(BF16)
        for bi in range(nb):
            rows = slice(bi * length, (bi + 1) * length)
            for hd in range(RET_HEADS):
                sl = slice(hd * HEAD_DIM, (hd + 1) * HEAD_DIM)
                sf_ref[bi, hd] = _dot_t0(v[rows, sl], kf[rows, sl])
                sb_ref[bi, hd] = _dot_t0(v[rows, sl], kb[rows, sl])


def _mod_ctx_call(craw, w_mod, b_mod, w_in, ctx, norm_g, dec_rows, ctx_row):
    d, n = w_mod.shape
    b, length, _ = ctx.shape
    tn = MOD_COL_TILE
    steps = n // tn
    n_slabs = steps // 2
    slab = w_in.shape[0] // n_slabs
    n_keep = -(-2 * d // tn)
    assert steps * tn == n and slab * n_slabs == w_in.shape[0] and slab % 128 == 0
    assert n_keep <= steps - n_slabs

    def slab_idx(j):
        return (jnp.maximum(j - (steps - n_slabs), 0), 0)

    slab_spec = pl.BlockSpec((slab, w_in.shape[1]), slab_idx)
    state = jax.ShapeDtypeStruct((b, RET_HEADS, HEAD_DIM, HEAD_DIM), F32)
    state_spec = pl.BlockSpec(state.shape, lambda j: (0, 0, 0, 0))
    return pl.pallas_call(
        functools.partial(_mod_ctx_kernel, ctx_row),
        grid=(steps,),
        in_specs=[
            pl.BlockSpec((MOD_ROWS, d), lambda j: (0, 0)),
            pl.BlockSpec((d, tn), lambda j: (0, j)),
            pl.BlockSpec((1, tn), lambda j: (0, j)),
            slab_spec,
            pl.BlockSpec((b, length, d), lambda j: (0, 0, 0)),
            pl.BlockSpec((1, d), lambda j: (0, 0)),
            pl.BlockSpec(dec_rows.shape, lambda j: (0, 0)),
        ],
        out_specs=[pl.BlockSpec((MOD_ROWS, tn), lambda j: (0, j)), slab_spec, state_spec, state_spec],
        out_shape=[jax.ShapeDtypeStruct((MOD_ROWS, n), F32),
                   jax.ShapeDtypeStruct(w_in.shape, BF16), state, state],
        scratch_shapes=[
            pltpu.VMEM((n_keep, MOD_ROWS, tn), F32),
            pltpu.VMEM((n_slabs, b * length, slab), BF16),
            pltpu.VMEM((b * length, 2 * RET_WIDTH), F32),
        ],
        compiler_params=pltpu.CompilerParams(
            dimension_semantics=("arbitrary",), vmem_limit_bytes=VMEM_LIMIT),
        name="mod_ctx",
    )(craw, w_mod, b_mod, w_in, ctx, norm_g, dec_rows)


def _rope(acc, cos, sin, scale):
    lane = lax.broadcasted_iota(jnp.int32, (1, HEAD_DIM), 1)
    lo = (lane % 64) < 32
    outs = []
    for hd in range(RET_HEADS):
        xh = acc[:, hd * HEAD_DIM:(hd + 1) * HEAD_DIM]
        partner = jnp.where(lo, pltpu.roll(xh, 96, 1), pltpu.roll(xh, 32, 1))
        y = xh * cos + partner * sin
        outs.append(y if scale == 1.0 else y * scale)
    return outs


def _inproj_kernel(x_ref, sh_ref, sc_ref, g_ref, w_ref, cos_ref, sin_ref, cw_ref, dec_ref,
                   sb0_ref, wout_ref, wdn_ref,
                   q_ref, k_ref, vt_ref, gate_ref, conv_ref, sbound_ref, wout_b_ref, wdn_b_ref,
                   s_scr, dec_scr):
    b = pl.program_id(0)
    i = pl.program_id(1)
    tm = dec_scr.shape[1]
    r = RET_WIDTH

    wout_b_ref[...] = wout_ref[...].astype(wout_b_ref.dtype)
    wdn_b_ref[...] = wdn_ref[...].astype(wdn_b_ref.dtype)

    @pl.when(i == 0)
    def _():
        s_scr[...] = sb0_ref[0]
        t = lax.broadcasted_iota(jnp.int32, dec_scr.shape, 1).astype(F32)
        lgb = _log_sigmoid(dec_ref[RET_HEADS:, 0:1])
        dec_scr[...] = jnp.exp(lgb * t)

    h = _norm_mod(x_ref[0], g_ref[...], sh_ref[pl.ds(b, 1), :], sc_ref[pl.ds(b, 1), :]).astype(BF16)

    def proj(group):
        return _dot(h, w_ref[:, group * r:(group + 1) * r])

    cos = cos_ref[...]
    sin = sin_ref[...]
    for hd, qh in enumerate(_rope(proj(0), cos, sin, 1.0)):
        q_ref[0, :, hd * HEAD_DIM:(hd + 1) * HEAD_DIM] = qh.astype(q_ref.dtype)
    ks = [kh.astype(BF16) for kh in _rope(proj(1), cos, sin, HEAD_DIM ** -0.5)]
    for hd in range(RET_HEADS):
        k_ref[0, :, hd * HEAD_DIM:(hd + 1) * HEAD_DIM] = ks[hd]
    v = proj(2)
    tile_dec = jnp.exp(_log_sigmoid(dec_ref[RET_HEADS:, :HEAD_DIM]) * float(tm))
    for hd in range(RET_HEADS):
        vt = v[:, hd * HEAD_DIM:(hd + 1) * HEAD_DIM].T
        vt_ref[0, hd] = vt.astype(vt_ref.dtype)
        vtd = (vt * dec_scr[hd:hd + 1, :]).astype(BF16)
        s_old = s_scr[hd]
        sbound_ref[0, 0, hd] = s_old
        s_scr[hd] = tile_dec[hd:hd + 1, :] * s_old + _dot(vtd, ks[hd])

    g = proj(3)
    gate_ref[0] = (g * _sigmoid(g)).astype(gate_ref.dtype)

    u = proj(5) * proj(6)
    col = lax.broadcasted_iota(jnp.int32, (tm, 1), 0) % GRID_W
    up = jnp.where(col == 0, 0.0, pltpu.roll(u, 1, 0))
    un = jnp.where(col == GRID_W - 1, 0.0, pltpu.roll(u, tm - 1, 0))
    y = proj(4) * (cw_ref[0:1, :] * up + cw_ref[1:2, :] * u + cw_ref[2:3, :] * un)
    conv_ref[0] = y.astype(conv_ref.dtype)


def _inproj_call(x, mod, norm_g, w_in, cos_t, sin_t, conv_w, dec_rows, sb0, w_out, w_down):
    b, n, d = x.shape
    tm = INPROJ_TILE
    nt = n // tm
    r = RET_WIDTH

    def slab_spec_of(w):
        slab = w.shape[0] // (b * nt)
        assert slab * b * nt == w.shape[0] and slab % 16 == 0
        return pl.BlockSpec((slab, w.shape[1]), lambda bi, i: (bi * nt + i, 0))

    later_weights = (w_out, w_down)

    def tok(bi, i):
        return (bi, nt - 1 - i, 0)

    seq = jax.ShapeDtypeStruct((b, n, r), BF16)
    seq_spec = pl.BlockSpec((1, tm, r), tok)
    heads_t = jax.ShapeDtypeStruct((b, RET_HEADS, HEAD_DIM, n), BF16)
    heads_t_spec = pl.BlockSpec((1, RET_HEADS, HEAD_DIM, tm), lambda bi, i: (bi, 0, 0, nt - 1 - i))
    return pl.pallas_call(
        _inproj_kernel,
        grid=(b, nt),
        in_specs=[
            pl.BlockSpec((1, tm, d), tok),
            pl.BlockSpec((MOD_ROWS, d), lambda bi, i: (0, 0)),
            pl.BlockSpec((MOD_ROWS, d), lambda bi, i: (0, 1)),
            pl.BlockSpec((1, d), lambda bi, i: (0, 0)),
            pl.BlockSpec(w_in.shape, lambda bi, i: (0, 0), pipeline_mode=pl.Buffered(1)),
            pl.BlockSpec((tm, HEAD_DIM), lambda bi, i: (nt - 1 - i, 0)),
            pl.BlockSpec((tm, HEAD_DIM), lambda bi, i: (nt - 1 - i, 0)),
            pl.BlockSpec((3, CONV_CH), lambda bi, i: (0, 0)),
            pl.BlockSpec(dec_rows.shape, lambda bi, i: (0, 0)),
            pl.BlockSpec((1, RET_HEADS, HEAD_DIM, HEAD_DIM), lambda bi, i: (bi, 0, 0, 0)),
            *[slab_spec_of(w) for w in later_weights],
        ],
        out_specs=[
            seq_spec, seq_spec, heads_t_spec, seq_spec, seq_spec,
            pl.BlockSpec((1, 1, RET_HEADS, HEAD_DIM, HEAD_DIM),
                         lambda bi, i: (bi, nt - 1 - i, 0, 0, 0)),
            *[slab_spec_of(w) for w in later_weights],
        ],
        out_shape=[seq, seq, heads_t, seq, seq,
                   jax.ShapeDtypeStruct((b, nt, RET_HEADS, HEAD_DIM, HEAD_DIM), F32),
                   *[jax.ShapeDtypeStruct(w.shape, BF16) for w in later_weights]],
        scratch_shapes=[
            pltpu.VMEM((RET_HEADS, HEAD_DIM, HEAD_DIM), F32),
            pltpu.VMEM((RET_HEADS, tm), F32),
        ],
        compiler_params=pltpu.CompilerParams(
            dimension_semantics=("arbitrary", "arbitrary"),
            vmem_limit_bytes=V7X_VMEM_BYTES - 1024 * 1024),
        name="inproj",
    )(x, mod, mod, norm_g, w_in, cos_t, sin_t, conv_w, dec_rows, sb0, *later_weights)


def _mix_kernel(q_ref, k_ref, vt_ref, gate_ref, conv_ref, x_ref, sbound_ref, sf0_ref,
                wo_r_ref, wo_c_ref, g1_ref, dec_ref, wup_ref,
                o_ref, wup_b_ref,
                sf_scr, dmat_scr, qdec_scr, kdec_scr, cdec_scr, ret_scr):
    b = pl.program_id(0)
    i = pl.program_id(1)
    tm = ret_scr.shape[0]
    c = dmat_scr.shape[1]
    nc = tm // c

    wup_b_ref[...] = wup_ref[...].astype(wup_b_ref.dtype)

    @pl.when(i == 0)
    def _():
        sf_scr[...] = sf0_ref[0]

    @pl.when((b == 0) & (i == 0))
    def _():
        lgf = _log_sigmoid(dec_ref[:RET_HEADS])
        lgb = _log_sigmoid(dec_ref[RET_HEADS:])
        ri = lax.broadcasted_iota(jnp.int32, (c, c), 0)
        ci = lax.broadcasted_iota(jnp.int32, (c, c), 1)
        rel = (ri - ci).astype(F32)
        t = lax.broadcasted_iota(jnp.int32, (c, HEAD_DIM), 0).astype(F32)
        tl = lax.broadcasted_iota(jnp.int32, (1, c), 1).astype(F32)
        for hd in range(RET_HEADS):
            f = lgf[hd:hd + 1, :]
            g = lgb[hd:hd + 1, :]
            dmat_scr[hd] = jnp.where(rel >= 0, jnp.exp(f * jnp.maximum(rel, 0.0)),
                                     jnp.exp(g * jnp.maximum(-rel, 0.0))).astype(BF16)
            fh = f[:, :HEAD_DIM]
            gh = g[:, :HEAD_DIM]
            qdec_scr[hd, :, :HEAD_DIM] = jnp.exp(fh * (t + 1.0)).astype(BF16)
            qdec_scr[hd, :, HEAD_DIM:] = jnp.exp(gh * (float(c) - t)).astype(BF16)
            kdec_scr[hd, 0:1, :] = jnp.exp(f * (float(c) - 1.0 - tl))
            kdec_scr[hd, 1:2, :] = jnp.exp(g * tl)
            cdec_scr[hd, :, :HEAD_DIM] = jnp.broadcast_to(jnp.exp(fh * float(c)), (8, HEAD_DIM))
            cdec_scr[hd, :, HEAD_DIM:] = jnp.broadcast_to(jnp.exp(gh * float(c)), (8, HEAD_DIM))

    heads = range(RET_HEADS)
    hsl = [slice(hd * HEAD_DIM, (hd + 1) * HEAD_DIM) for hd in heads]
    crows = [slice(ch * c, (ch + 1) * c) for ch in range(nc)]
    ks = [[k_ref[0, crows[ch], hsl[hd]] for ch in range(nc)] for hd in heads]
    vts = [[vt_ref[0, hd, :, crows[ch]] for ch in range(nc)] for hd in heads]

    qs = [[q_ref[0, crows[ch], hsl[hd]] for ch in range(nc)] for hd in heads]

    def masked_scores(ch):
        return [_dot_nt(qs[hd][ch], ks[hd][ch]).astype(BF16) * dmat_scr[hd] for hd in heads]

    ps_first = masked_scores(0)

    s2ts = []
    for hd in heads:
        cf = cdec_scr[hd, 0:1, :HEAD_DIM]
        cb = cdec_scr[hd, 0:1, HEAD_DIM:]
        sb = [None] * nc
        sb[nc - 1] = sbound_ref[0, 0, hd]
        for ch in range(nc - 1, 0, -1):
            vb = vts[hd][ch] * kdec_scr[hd, 1:2, :].astype(BF16)
            sb[ch - 1] = cb * sb[ch] + _dot(vb, ks[hd][ch])
        sf = [sf_scr[hd]]
        for ch in range(nc):
            vf = vts[hd][ch] * kdec_scr[hd, 0:1, :].astype(BF16)
            sf.append(cf * sf[ch] + _dot(vf, ks[hd][ch]))
        sf_scr[hd] = sf[nc]
        s2ts.append([jnp.concatenate([sf[ch], sb[ch]], axis=1).astype(BF16) for ch in range(nc)])

    def cross_terms(ch):
        out = []
        for hd in heads:
            qh = qs[hd][ch]
            q2 = jnp.concatenate([qh, qh], axis=1) * qdec_scr[hd]
            out.append(_dot_nt(s2ts[hd][ch], q2))
        return out

    crosses = [cross_terms(ch) for ch in range(nc)]
    ps = ps_first
    for ch in range(nc):
        inner = [_dot_nt(vts[hd][ch], ps[hd]) for hd in heads]
        cross = crosses[ch]
        for hd in heads:
            o = (inner[hd] + cross[hd]).T
            ms = jnp.mean(o * o, axis=-1, keepdims=True)
            o = o * lax.rsqrt(ms + EPS) * gate_ref[0, crows[ch], hsl[hd]].astype(F32)
            ret_scr[crows[ch], hsl[hd]] = o.astype(BF16)
        if ch + 1 < nc:
            ps = masked_scores(ch + 1)
        mix = _dot(ret_scr[crows[ch], :], wo_r_ref[...]) + _dot(conv_ref[0, crows[ch], :], wo_c_ref[...])
        o_ref[0, crows[ch], :] = x_ref[0, crows[ch], :] + g1_ref[pl.ds(b, 1), :] * mix


def _mix_call(q, k, vt, gate, conv, x, sbound, sf0, w_out, mod, dec_rows, w_up):
    b, n, d = x.shape
    tm = MIX_TILE
    nt = n // tm
    bound_per_tile = sbound.shape[1] // nt
    r = RET_WIDTH
    c = RET_CHUNK
    slab = w_up.shape[0] // (b * nt)
    assert slab * b * nt == w_up.shape[0] and slab % 16 == 0
    slab_spec = pl.BlockSpec((slab, w_up.shape[1]), lambda bi, i: (bi * nt + i, 0))

    def tok(bi, i):
        return (bi, i, 0)

    seq_spec = pl.BlockSpec((1, tm, r), tok)
    return pl.pallas_call(
        _mix_kernel,
        grid=(b, nt),
        in_specs=[
            seq_spec, seq_spec,
            pl.BlockSpec((1, RET_HEADS, HEAD_DIM, tm), lambda bi, i: (bi, 0, 0, i)),
            seq_spec, seq_spec,
            pl.BlockSpec((1, tm, d), tok),
            pl.BlockSpec((1, 1, RET_HEADS, HEAD_DIM, HEAD_DIM),
                         lambda bi, i: (bi, (i + 1) * bound_per_tile - 1, 0, 0, 0)),
            pl.BlockSpec((1, RET_HEADS, HEAD_DIM, HEAD_DIM), lambda bi, i: (bi, 0, 0, 0)),
            pl.BlockSpec((r, d), lambda bi, i: (0, 0)),
            pl.BlockSpec((CONV_CH, d), lambda bi, i: (1, 0)),
            pl.BlockSpec((MOD_ROWS, d), lambda bi, i: (0, 2)),
            pl.BlockSpec(dec_rows.shape, lambda bi, i: (0, 0)),
            slab_spec,
        ],
        out_specs=[pl.BlockSpec((1, tm, d), tok), slab_spec],
        out_shape=[jax.ShapeDtypeStruct((b, n, d), F32), jax.ShapeDtypeStruct(w_up.shape, BF16)],
        scratch_shapes=[
            pltpu.VMEM((RET_HEADS, HEAD_DIM, HEAD_DIM), F32),
            pltpu.VMEM((RET_HEADS, c, c), BF16),
            pltpu.VMEM((RET_HEADS, c, 2 * HEAD_DIM), BF16),
            pltpu.VMEM((RET_HEADS, 8, c), F32),
            pltpu.VMEM((RET_HEADS, 8, 2 * HEAD_DIM), F32),
            pltpu.VMEM((tm, r), BF16),
        ],
        compiler_params=pltpu.CompilerParams(
            dimension_semantics=("arbitrary", "arbitrary"), vmem_limit_bytes=VMEM_LIMIT),
        name="mix",
    )(q, k, vt, gate, conv, x, sbound, sf0, w_out, w_out, mod, dec_rows, w_up)


def _ffn_step(nf, tiles_per_batch, x_ref, sh_ref, sc_ref, gt_ref, g2_ref, gf_ref,
              wa_ref, wb_ref, cw_ref, cb_ref, wd_ref,
              o_ref,
              h_scr, cnt_ref):
    count = cnt_ref[0]
    cnt_ref[0] = count + 1
    f = count % nf
    b = count // (nf * tiles_per_batch)
    tm, d = h_scr.shape
    tf = wa_ref.shape[1]
    blk = o_ref.shape[1:]
    step = blk[1]

    def body(first, last):
        if first:
            h = _norm_mod(x_ref[0].reshape(tm, d), g2_ref[...], sh_ref[pl.ds(b, 1), :],
                          sc_ref[pl.ds(b, 1), :]).astype(BF16)
            h_scr[...] = h
        else:
            h = h_scr[...]
        a = _dot(h, wa_ref[...])
        zero = jnp.zeros((step, tf), F32)
        a_up = jnp.concatenate([zero, a[:tm - step]], axis=0)
        a_dn = jnp.concatenate([a[step:], zero], axis=0)
        a = cw_ref[0:1, :] * a_up + cw_ref[1:2, :] * a + cw_ref[2:3, :] * a_dn + cb_ref[...]
        act = (a * _sigmoid(a) * _dot(h, wb_ref[...])).astype(BF16)
        if not last:
            y = _dot(act, wd_ref[...])
            o_ref[0] = (y if first else o_ref[0].reshape(tm, d) + y).reshape(blk)
            return
        groups = LAST_STEP_ROW_GROUPS
        gr = blk[0] // groups
        ys = [_dot(act[g * gr * step:(g + 1) * gr * step], wd_ref[...]) for g in range(groups)]
        for g in range(groups):
            rs = slice(g * gr, (g + 1) * gr)
            acc = o_ref[0, rs].reshape(gr * step, d) + ys[g]
            z = x_ref[0, rs].reshape(gr * step, d) + gt_ref[pl.ds(b, 1), :] * acc
            ms = jnp.mean(z * z, axis=-1, keepdims=True)
            o_ref[0, rs] = (z * lax.rsqrt(ms + EPS) * gf_ref[...]).reshape(gr, step, d)

    @pl.when(f == 0)
    def _():
        body(True, False)

    @pl.when((f > 0) & (f < nf - 1))
    def _():
        body(False, False)

    @pl.when(f == nf - 1)
    def _():
        body(False, True)


def _ffn_call(x, mod, norm_g, final_g, w_up, ffn_conv_w, ffn_conv_b, w_down):
    b, n, d = x.shape
    d_ff = w_down.shape[0]
    rows = n // GRID_W
    cols = FFN_COLS
    tf = FF_TILE
    nf = d_ff // tf
    assert nf >= 2
    xg = x.reshape(b, rows, GRID_W, d)
    tile = pl.BlockSpec((1, rows, cols, d), lambda bi, i, f: (bi, 0, i, 0))

    def ffn_kernel(*refs):
        *operands, h_scr, cnt_ref = refs
        cnt_ref[0] = 0
        pltpu.emit_pipeline(
            functools.partial(_ffn_step, nf, GRID_W // cols),
            grid=(b, GRID_W // cols, nf),
            in_specs=step_in_specs,
            out_specs=[tile],
            dimension_semantics=("arbitrary", "arbitrary", "arbitrary"),
        )(*operands, scratches=(h_scr, cnt_ref))

    step_in_specs = [
            tile,
            pl.BlockSpec((MOD_ROWS, d), lambda bi, i, f: (0, 3)),
            pl.BlockSpec((MOD_ROWS, d), lambda bi, i, f: (0, 4)),
            pl.BlockSpec((MOD_ROWS, d), lambda bi, i, f: (0, 5)),
            pl.BlockSpec((1, d), lambda bi, i, f: (0, 0)),
            pl.BlockSpec((1, d), lambda bi, i, f: (0, 0)),
            pl.BlockSpec((d, tf), lambda bi, i, f: (0, f)),
            pl.BlockSpec((d, tf), lambda bi, i, f: (0, nf + f)),
            pl.BlockSpec((3, tf), lambda bi, i, f: (0, f)),
            pl.BlockSpec((1, tf), lambda bi, i, f: (0, f)),
            pl.BlockSpec((tf, d), lambda bi, i, f: (f, 0)),
    ]
    in_hbm = pl.BlockSpec(memory_space=pl.ANY)
    out = pl.pallas_call(
        ffn_kernel,
        in_specs=[in_hbm] * len(step_in_specs),
        out_specs=in_hbm,
        out_shape=jax.ShapeDtypeStruct((b, rows, GRID_W, d), F32),
        scratch_shapes=[
            pltpu.VMEM((rows * cols, d), BF16),
            pltpu.SMEM((1,), jnp.int32),
        ],
        compiler_params=pltpu.CompilerParams(vmem_limit_bytes=VMEM_LIMIT),
        name="ffn",
    )(xg, mod, mod, mod, norm_g, final_g, w_up, w_up, ffn_conv_w, ffn_conv_b, w_down)
    return out.reshape(b, n, d)


def _rope_tables(n):
    pos = np.arange(n, dtype=np.int32)
    row = (pos // GRID_W).astype(np.float32)
    col = (pos % GRID_W).astype(np.float32)
    dh = HEAD_DIM // 2
    freqs = np.float32(ROPE_THETA) ** (-np.arange(0, dh, 2, dtype=np.float32) / np.float32(dh))
    parts_c, parts_s = [], []
    for p in (row, col):
        ang = (p[:, None] * freqs[None, :]).astype(np.float32)
        cos = np.cos(ang).astype(np.float32)
        sin = np.sin(ang).astype(np.float32)
        parts_c += [cos, cos]
        parts_s += [-sin, sin]
    return (jnp.asarray(np.concatenate(parts_c, axis=-1)),
            jnp.asarray(np.concatenate(parts_s, axis=-1)))


def kernel(x, c, ctx, c_ctx, w_mod, b_mod, norm1_g, w_in, ret_decay_fwd, ret_decay_bwd,
           conv_w, w_out, norm2_g, w_up, ffn_conv_w, ffn_conv_b, w_down, final_g):
    b, n, d = x.shape
    depth = w_mod.shape[0]
    assert depth == 1 and n % MIX_TILE == 0 and n % GRID_W == 0
    assert MIX_TILE % INPROJ_TILE == 0 and MIX_TILE % RET_CHUNK == 0
    cos_t, sin_t = _rope_tables(n)
    craw = jnp.concatenate(
        [c, c_ctx[None, :], jnp.zeros((MOD_ROWS - b - 1, d), F32)], axis=0)

    layer = 0
    dec_rows = jnp.broadcast_to(
        jnp.concatenate([ret_decay_fwd[layer], ret_decay_bwd[layer]])[:, None],
        (2 * RET_HEADS, RET_CHUNK))
    g1 = norm1_g[layer][None, :]

    mod, w_in_b, sf0, sb0 = _mod_ctx_call(craw, w_mod[layer], b_mod[layer][None, :], w_in[layer],
                                          ctx, g1, dec_rows, b)
    q, k, vt, gate, conv, sbound, w_out_b, w_down_b = _inproj_call(
        x, mod, g1, w_in_b, cos_t, sin_t, conv_w[layer], dec_rows, sb0, w_out[layer],
        w_down[layer])
    x_mid, w_up_b = _mix_call(q, k, vt, gate, conv, x, sbound, sf0, w_out_b, mod, dec_rows,
                              w_up[layer])
    return _ffn_call(x_mid, mod, norm2_g[layer][None, :], final_g[None, :], w_up_b,
                     ffn_conv_w[layer], ffn_conv_b[layer][None, :], w_down_b)
```
